```python
import math
import jax, jax.numpy as jnp
from jax import lax
import numpy as np

D_MODEL = 2048
BATCH = 2
SEQ = 8192
DEPTH = 1

CHUNK = 64
Q_BLOCK = 128
EPS = 1e-6

DA_HEADS = 8
DA_HEAD_DIM = 128
DA_V_DIM = 2 * DA_HEAD_DIM
DA_WIDTH = DA_HEADS * DA_V_DIM

MLA_HEADS = 16
MLA_Q_RANK = 768
MLA_KV_RANK = 512
MLA_NOPE_DIM = 128
MLA_ROPE_DIM = 64
MLA_V_DIM = 128
MLA_QK_DIM = MLA_NOPE_DIM + MLA_ROPE_DIM
MLA_WIDTH = MLA_HEADS * MLA_V_DIM
ROPE_THETA = 10000.0

D_FF = 5632

IN_SPLITS = (
    DA_HEADS * 2 * DA_HEAD_DIM,
    DA_HEADS * 2 * DA_HEAD_DIM,
    DA_WIDTH,
    MLA_Q_RANK,
    MLA_KV_RANK,
    MLA_ROPE_DIM,
    D_MODEL,
    D_MODEL,
)
N_IN = sum(IN_SPLITS)

kernel_name = 'hybrid_diffattn_mla_macaron'


def rmsnorm(x, g):
    xf = x.astype(jnp.float32)
    y = xf * lax.rsqrt(jnp.mean(xf * xf, axis=-1, keepdims=True) + EPS)
    return y.astype(x.dtype) * g


def swiglu(h, w1, w3, w2):
    return (jax.nn.silu(h @ w1) * (h @ w3)) @ w2


def chunk_mask(q0):
    t = q0 + jnp.arange(Q_BLOCK)
    s = jnp.arange(q0 + Q_BLOCK)
    return (s // CHUNK)[None, :] <= (t // CHUNK)[:, None]


def alibi_dist(q0):
    t = q0 + jnp.arange(Q_BLOCK)
    s = jnp.arange(q0 + Q_BLOCK)
    return jnp.abs(t[:, None] - s[None, :]).astype(jnp.float32)


def apply_rope(x):
    S, R = x.shape[1], x.shape[-1]
    inv = ROPE_THETA ** (-jnp.arange(0, R, 2, dtype=jnp.float32) / R)
    ang = jnp.arange(S, dtype=jnp.float32)[:, None] * inv[None, :]
    cos = jnp.cos(ang)[:, None, :].astype(x.dtype)
    sin = jnp.sin(ang)[:, None, :].astype(x.dtype)
    x1, x2 = x[..., : R // 2], x[..., R // 2:]
    return jnp.concatenate([x1 * cos - x2 * sin, x2 * cos + x1 * sin], axis=-1)


def diff_attention_core(q, k, v, lam):
    S = q.shape[1]
    scale = DA_HEAD_DIM ** -0.5
    slopes = 2.0 ** (-8.0 * jnp.arange(1, DA_HEADS + 1, dtype=jnp.float32) / DA_HEADS)
    outs = []
    for i in range(S // Q_BLOCK):
        q0 = i * Q_BLOCK
        kend = q0 + Q_BLOCK
        s = jnp.einsum('bqhmd,bkhmd->bhmqk', q[:, q0:kend], k[:, :kend],
                       preferred_element_type=jnp.float32) * scale
        s = s - slopes[None, :, None, None, None] * alibi_dist(q0)
        s = jnp.where(chunk_mask(q0), s, -jnp.inf)
        p = jax.nn.softmax(s, axis=-1)
        pd = p[:, :, 0] - lam * p[:, :, 1]
        outs.append(jnp.einsum('bhqk,bkhe->bqhe', pd, v[:, :kend].astype(jnp.float32)))
    return jnp.concatenate(outs, axis=1)


def mla_attention_core(q, k, v):
    S = q.shape[1]
    scale = MLA_QK_DIM ** -0.5
    outs = []
    for i in range(S // Q_BLOCK):
        q0 = i * Q_BLOCK
        kend = q0 + Q_BLOCK
        s = jnp.einsum('bqhd,bkhd->bhqk', q[:, q0:kend], k[:, :kend],
                       preferred_element_type=jnp.float32) * scale
        s = jnp.where(chunk_mask(q0), s, -jnp.inf)
        p = jax.nn.softmax(s, axis=-1)
        outs.append(jnp.einsum('bhqk,bkhe->bqhe', p, v[:, :kend].astype(jnp.float32)))
    return jnp.concatenate(outs, axis=1)


def diff_attention_branch(qa, ka, va, q_norm_g, k_norm_g, lq1, lk1, lq2, lk2, subln_g, lambda_init):
    B, S = qa.shape[:2]
    q = rmsnorm(qa.reshape(B, S, DA_HEADS, 2, DA_HEAD_DIM), q_norm_g)
    k = rmsnorm(ka.reshape(B, S, DA_HEADS, 2, DA_HEAD_DIM), k_norm_g)
    v = va.reshape(B, S, DA_HEADS, DA_V_DIM)
    f32 = jnp.float32
    lam = (jnp.exp(jnp.sum(lq1.astype(f32) * lk1.astype(f32)))
           - jnp.exp(jnp.sum(lq2.astype(f32) * lk2.astype(f32))) + lambda_init)
    o = diff_attention_core(q, k, v, lam)
    o = rmsnorm(o, subln_g) * (1.0 - lambda_init)
    return o.astype(qa.dtype).reshape(B, S, DA_WIDTH)


def mla_branch(cq, ckv, k_rope, q_a_norm_g, w_qb, kv_a_norm_g, w_kvb, q_norm_g, k_norm_g):
    B, S = cq.shape[:2]
    q = (rmsnorm(cq, q_a_norm_g) @ w_qb).reshape(B, S, MLA_HEADS, MLA_QK_DIM)
    kv = (rmsnorm(ckv, kv_a_norm_g) @ w_kvb).reshape(B, S, MLA_HEADS, MLA_NOPE_DIM + MLA_V_DIM)
    k_nope, v = kv[..., :MLA_NOPE_DIM], kv[..., MLA_NOPE_DIM:]
    k_rope_h = jnp.broadcast_to(k_rope[:, :, None, :], (B, S, MLA_HEADS, MLA_ROPE_DIM))
    k = jnp.concatenate([k_nope, k_rope_h], axis=-1)
    q = rmsnorm(q, q_norm_g)
    k = rmsnorm(k, k_norm_g)
    q = jnp.concatenate([q[..., :MLA_NOPE_DIM], apply_rope(q[..., MLA_NOPE_DIM:])], axis=-1)
    k = jnp.concatenate([k[..., :MLA_NOPE_DIM], apply_rope(k[..., MLA_NOPE_DIM:])], axis=-1)
    o = mla_attention_core(q, k, v)
    return o.astype(cq.dtype).reshape(B, S, MLA_WIDTH)


def setup_inputs(seed: int = 0) -> dict:
    key = jax.random.key(seed)
    ks = list(jax.random.split(key, 32))
    f32 = jnp.float32

    def w(shape, fan_in):
        return jax.random.normal(ks.pop(), shape, f32) * fan_in ** -0.5

    def g(shape):
        return 1.0 + 0.02 * jax.random.normal(ks.pop(), shape, f32)

    def lam():
        return 0.1 * jax.random.normal(ks.pop(), (DEPTH, DA_HEAD_DIM), f32)

    L = DEPTH
    return {
        'x': jax.random.normal(ks.pop(), (BATCH, SEQ, D_MODEL), f32),
        'ffn1_norm_g': g((L, D_MODEL)),
        'ffn1_w1': w((L, D_MODEL, D_FF), D_MODEL),
        'ffn1_w3': w((L, D_MODEL, D_FF), D_MODEL),
        'ffn1_w2': w((L, D_FF, D_MODEL), D_FF),
        'mix_norm_g': g((L, D_MODEL)),
        'w_in': w((L, D_MODEL, N_IN), D_MODEL),
        'da_q_norm_g': g((L, DA_HEAD_DIM)),
        'da_k_norm_g': g((L, DA_HEAD_DIM)),
        'da_lambda_q1': lam(),
        'da_lambda_k1': lam(),
        'da_lambda_q2': lam(),
        'da_lambda_k2': lam(),
        'da_subln_g': g((L, DA_V_DIM)),
        'mla_q_a_norm_g': g((L, MLA_Q_RANK)),
        'mla_w_qb': w((L, MLA_Q_RANK, MLA_HEADS * MLA_QK_DIM), MLA_Q_RANK),
        'mla_kv_a_norm_g': g((L, MLA_KV_RANK)),
        'mla_w_kvb': w((L, MLA_KV_RANK, MLA_HEADS * (MLA_NOPE_DIM + MLA_V_DIM)), MLA_KV_RANK),
        'mla_q_norm_g': g((L, MLA_QK_DIM)),
        'mla_k_norm_g': g((L, MLA_QK_DIM)),
        'w_branch_a': w((L, DA_WIDTH, D_MODEL), DA_WIDTH),
        'w_branch_b': w((L, MLA_WIDTH, D_MODEL), MLA_WIDTH),
        'w_out': w((L, D_MODEL, D_MODEL), D_MODEL),
        'ffn2_norm_g': g((L, D_MODEL)),
        'ffn2_w1': w((L, D_MODEL, D_FF), D_MODEL),
        'ffn2_w3': w((L, D_MODEL, D_FF), D_MODEL),
        'ffn2_w2': w((L, D_FF, D_MODEL), D_FF),
    }


def reference(x, ffn1_norm_g, ffn1_w1, ffn1_w3, ffn1_w2, mix_norm_g, w_in,
              da_q_norm_g, da_k_norm_g, da_lambda_q1, da_lambda_k1, da_lambda_q2, da_lambda_k2,
              da_subln_g, mla_q_a_norm_g, mla_w_qb, mla_kv_a_norm_g, mla_w_kvb,
              mla_q_norm_g, mla_k_norm_g, w_branch_a, w_branch_b, w_out,
              ffn2_norm_g, ffn2_w1, ffn2_w3, ffn2_w2):
    offsets = [int(o) for o in np.cumsum(IN_SPLITS)[:-1]]
    for l in range(DEPTH):
        lambda_init = 0.8 - 0.6 * math.exp(-0.3 * l)
        x = x + 0.5 * swiglu(rmsnorm(x, ffn1_norm_g[l]), ffn1_w1[l], ffn1_w3[l], ffn1_w2[l])
        h = rmsnorm(x, mix_norm_g[l])
        proj = h @ w_in[l]
        qa, ka, va, cq, ckv, k_rope, ga, gb = jnp.split(proj, offsets, axis=-1)
        y_a = diff_attention_branch(qa, ka, va, da_q_norm_g[l], da_k_norm_g[l],
                                    da_lambda_q1[l], da_lambda_k1[l], da_lambda_q2[l], da_lambda_k2[l],
                                    da_subln_g[l], lambda_init)
        y_b = mla_branch(cq, ckv, k_rope, mla_q_a_norm_g[l], mla_w_qb[l], mla_kv_a_norm_g[l],
                         mla_w_kvb[l], mla_q_norm_g[l], mla_k_norm_g[l])
        merged = jax.nn.sigmoid(ga) * (y_a @ w_branch_a[l]) + jax.nn.sigmoid(gb) * (y_b @ w_branch_b[l])
        x = x + merged @ w_out[l]
        x = x + 0.5 * swiglu(rmsnorm(x, ffn2_norm_g[l]), ffn2_w1[l], ffn2_w3[l], ffn2_w2[l])
    return x
```

```python
import functools
import math

import jax
import jax.numpy as jnp
from jax import lax
from jax.experimental import pallas as pl
from jax.experimental.pallas import tpu as pltpu

F32 = jnp.float32
BF16 = jnp.bfloat16

EPS = 1e-6
CHUNK = 64
DA_HEADS = 8
DA_HEAD_DIM = 128
DA_V_DIM = 2 * DA_HEAD_DIM
MLA_HEADS = 16
MLA_Q_RANK = 768
MLA_KV_RANK = 512
MLA_NOPE_DIM = 128
MLA_ROPE_DIM = 64
MLA_V_DIM = 128
MLA_QK_DIM = MLA_NOPE_DIM + MLA_ROPE_DIM
ROPE_THETA = 10000.0
LOG2E = math.log2(math.e)
MASK_VALUE = -1e30

LANES = 128
V7X_VMEM_BYTES = 64 * 1024 * 1024
VMEM_LIMIT = V7X_VMEM_BYTES - 8 * 1024 * 1024


def _cparams(sem):
    return pltpu.CompilerParams(dimension_semantics=sem, vmem_limit_bytes=VMEM_LIMIT)


def _rms(x, g):
    ms = jnp.mean(x * x, axis=-1, keepdims=True)
    return x * lax.rsqrt(ms + EPS) * g


def _nt_dot(a, b):
    return lax.dot_general(a, b, (((1,), (1,)), ((), ())), preferred_element_type=F32)


def _ffn_body(x_ref, g_ref, w1_ref, w3_ref, w2_ref, o_ref, h_ref):
    @pl.when(pl.program_id(1) == 0)
    def _():
        x = x_ref[...]
        h_ref[...] = _rms(x, g_ref[...]).astype(BF16)
        o_ref[...] = x

    h = h_ref[...]
    a = jnp.dot(h, w1_ref[...], preferred_element_type=F32)
    b = jnp.dot(h, w3_ref[...], preferred_element_type=F32)
    u = (a * jax.nn.sigmoid(a) * b).astype(BF16)
    o_ref[...] += jnp.dot(u, w2_ref[...], preferred_element_type=F32)


def _ffn(x, g, w1, w3, w2_half, *, tm, tf):
    t, d = x.shape
    f = w1.shape[1]
    return pl.pallas_call(
        _ffn_body,
        grid=(t // tm, f // tf),
        in_specs=[
            pl.BlockSpec((tm, d), lambda i, j: (i, 0)),
            pl.BlockSpec((1, d), lambda i, j: (0, 0)),
            pl.BlockSpec((d, tf), lambda i, j: (0, j)),
            pl.BlockSpec((d, tf), lambda i, j: (0, j)),
            pl.BlockSpec((tf, d), lambda i, j: (j, 0)),
        ],
        out_specs=pl.BlockSpec((tm, d), lambda i, j: (i, 0)),
        out_shape=jax.ShapeDtypeStruct((t, d), F32),
        scratch_shapes=[pltpu.VMEM((tm, d), BF16)],
        compiler_params=_cparams(("parallel", "arbitrary")),
        name="ffn",
    )(x, g, w1, w3, w2_half)


PROJ_TN = 512
_Q0, _K0, _V0, _C0, _G0, _NJ = 0, 4, 8, 12, 15, 23


def _head_norm(acc, g):
    parts = []
    for c in range(acc.shape[1] // LANES):
        blk = acc[:, c * LANES:(c + 1) * LANES]
        ms = jnp.mean(blk * blk, axis=-1, keepdims=True)
        parts.append(blk * lax.rsqrt(ms + EPS) * g)
    return jnp.concatenate(parts, axis=-1)


def _proj_body(x_ref, g_ref, w_ref, qg_ref, kg_ref,
               q_ref, k_ref, v_ref, c_ref, gate_ref, h_ref):
    j = pl.program_id(1)

    @pl.when(j == 0)
    def _():
        h_ref[...] = _rms(x_ref[...], g_ref[...]).astype(BF16)

    acc = jnp.dot(h_ref[...], w_ref[...], preferred_element_type=F32)

    @pl.when(j < _K0)
    def _():
        q_ref[...] = _head_norm(acc, qg_ref[...]).astype(BF16)

    @pl.when((j >= _K0) & (j < _V0))
    def _():
        k_ref[...] = _head_norm(acc, kg_ref[...]).astype(BF16)

    @pl.when((j >= _V0) & (j < _C0))
    def _():
        v_ref[...] = acc.astype(BF16)

    @pl.when((j >= _C0) & (j < _G0))
    def _():
        c_ref[...] = acc

    @pl.when(j >= _G0)
    def _():
        gate_ref[...] = jax.nn.sigmoid(acc).astype(BF16)


def _in_proj(x, g, w_cat, qg, kg, *, tm):
    t, d = x.shape
    tn = PROJ_TN

    def sect(j0, n):
        return pl.BlockSpec((tm, tn), lambda i, j: (i, jnp.clip(j - j0, 0, n - 1)))

    return pl.pallas_call(
        _proj_body,
        grid=(t // tm, _NJ),
        in_specs=[
            pl.BlockSpec((tm, d), lambda i, j: (i, 0)),
            pl.BlockSpec((1, d), lambda i, j: (0, 0)),
            pl.BlockSpec((d, tn), lambda i, j: (0, j)),
            pl.BlockSpec((1, LANES), lambda i, j: (0, 0)),
            pl.BlockSpec((1, LANES), lambda i, j: (0, 0)),
        ],
        out_specs=[
            sect(_Q0, _K0 - _Q0), sect(_K0, _V0 - _K0), sect(_V0, _C0 - _V0),
            sect(_C0, _G0 - _C0), sect(_G0, _NJ - _G0),
        ],
        out_shape=[
            jax.ShapeDtypeStruct((t, (_K0 - _Q0) * tn), BF16),
            jax.ShapeDtypeStruct((t, (_V0 - _K0) * tn), BF16),
            jax.ShapeDtypeStruct((t, (_C0 - _V0) * tn), BF16),
            jax.ShapeDtypeStruct((t, (_G0 - _C0) * tn), F32),
            jax.ShapeDtypeStruct((t, (_NJ - _G0) * tn), BF16),
        ],
        scratch_shapes=[pltpu.VMEM((tm, d), BF16)],
        compiler_params=_cparams(("parallel", "arbitrary")),
        name="in_proj",
    )(x, g, w_cat, qg, kg)


def _mla_proj_body(c_ref, qa_g_ref, kva_g_ref, wq_ref, wkv_ref,
                   gq_nope_ref, gq_rope_ref, gq_rot_ref,
                   gk_nope_ref, gk_rope_ref, gk_rot_ref, cos_ref, sin_ref,
                   qn_ref, qr_ref, kn_ref, kr_ref, v_ref):
    c = c_ref[...]
    o_kv = MLA_Q_RANK
    o_kr = MLA_Q_RANK + MLA_KV_RANK
    cqn = _rms(c[:, :o_kv], qa_g_ref[...]).astype(BF16)
    ckvn = _rms(c[:, o_kv:o_kr], kva_g_ref[...]).astype(BF16)
    k_rope = c[:, o_kr:o_kr + LANES]
    k_rot = c[:, o_kr + LANES:o_kr + 2 * LANES]
    cos = cos_ref[...]
    sin = sin_ref[...]
    first = lax.broadcasted_iota(jnp.int32, (1, LANES), 1) < MLA_ROPE_DIM
    k_roped = k_rope * gk_rope_ref[...] * cos + k_rot * gk_rot_ref[...] * sin
    k_rope_sq = 0.5 * jnp.sum(k_rope * k_rope, axis=-1, keepdims=True)
    inv_d = 1.0 / MLA_QK_DIM

    for p in range(MLA_HEADS // 2):
        cols = slice(p * 4 * LANES, (p + 1) * 4 * LANES)
        qraw = jnp.dot(cqn, wq_ref[:, cols], preferred_element_type=F32)
        na, nb = qraw[:, :LANES], qraw[:, LANES:2 * LANES]
        rp, rt = qraw[:, 2 * LANES:3 * LANES], qraw[:, 3 * LANES:]
        rp2 = rp * rp
        sa = jnp.sum(jnp.where(first, rp2, 0.0), axis=-1, keepdims=True)
        sb = jnp.sum(jnp.where(first, 0.0, rp2), axis=-1, keepdims=True)
        ra = lax.rsqrt((jnp.sum(na * na, axis=-1, keepdims=True) + sa) * inv_d + EPS)
        rb = lax.rsqrt((jnp.sum(nb * nb, axis=-1, keepdims=True) + sb) * inv_d + EPS)
        qn_ref[:, (2 * p) * LANES:(2 * p + 1) * LANES] = (na * ra * gq_nope_ref[...]).astype(BF16)
        qn_ref[:, (2 * p + 1) * LANES:(2 * p + 2) * LANES] = (nb * rb * gq_nope_ref[...]).astype(BF16)
        roped = rp * gq_rope_ref[...] * cos + rt * gq_rot_ref[...] * sin
        qr_ref[:, p * LANES:(p + 1) * LANES] = (roped * jnp.where(first, ra, rb)).astype(BF16)

        kvraw = jnp.dot(ckvn, wkv_ref[:, cols], preferred_element_type=F32)
        kna, va = kvraw[:, :LANES], kvraw[:, LANES:2 * LANES]
        knb, vb = kvraw[:, 2 * LANES:3 * LANES], kvraw[:, 3 * LANES:]
        rka = lax.rsqrt((jnp.sum(kna * kna, axis=-1, keepdims=True) + k_rope_sq) * inv_d + EPS)
        rkb = lax.rsqrt((jnp.sum(knb * knb, axis=-1, keepdims=True) + k_rope_sq) * inv_d + EPS)
        ca = slice((2 * p) * LANES, (2 * p + 1) * LANES)
        cb = slice((2 * p + 1) * LANES, (2 * p + 2) * LANES)
        kn_ref[:, ca] = (kna * rka * gk_nope_ref[...]).astype(BF16)
        kn_ref[:, cb] = (knb * rkb * gk_nope_ref[...]).astype(BF16)
        kr_ref[:, ca] = jnp.where(first, k_roped * rka, 0.0).astype(BF16)
        kr_ref[:, cb] = jnp.where(first, 0.0, k_roped * rkb).astype(BF16)
        v_ref[:, ca] = va.astype(BF16)
        v_ref[:, cb] = vb.astype(BF16)


def _mla_proj(c, qa_g, kva_g, wq, wkv, gains, cos2, sin2, *, tm, seq):
    t = c.shape[0]
    n_pos = seq // tm
    const = lambda shape: pl.BlockSpec(shape, lambda i: (0, 0))
    hw = MLA_HEADS * LANES
    tok = lambda w: pl.BlockSpec((tm, w), lambda i: (i, 0))
    return pl.pallas_call(
        _mla_proj_body,
        grid=(t // tm,),
        in_specs=[
            tok(c.shape[1]), const(qa_g.shape), const(kva_g.shape),
            const(wq.shape), const(wkv.shape),
        ] + [const((1, LANES))] * 6 + [
            pl.BlockSpec((tm, LANES), lambda i: (i % n_pos, 0)),
            pl.BlockSpec((tm, LANES), lambda i: (i % n_pos, 0)),
        ],
        out_specs=[tok(hw), tok(hw // 2), tok(hw), tok(hw), tok(hw)],
        out_shape=[
            jax.ShapeDtypeStruct((t, hw), BF16),
            jax.ShapeDtypeStruct((t, hw // 2), BF16),
            jax.ShapeDtypeStruct((t, hw), BF16),
            jax.ShapeDtypeStruct((t, hw), BF16),
            jax.ShapeDtypeStruct((t, hw), BF16),
        ],
        compiler_params=_cparams(("parallel",)),
        name="mla_proj",
    )(c, qa_g, kva_g, wq, wkv, *gains, cos2, sin2)


def _chunk_mask(tq, tk):
    row = lax.broadcasted_iota(jnp.int32, (tq, tk), 0)
    col = lax.broadcasted_iota(jnp.int32, (tq, tk), 1)
    return (col // CHUNK) <= (row // CHUNK)


def _online_update(s, m, l, acc_ref, v):
    m_new = jnp.maximum(m, jnp.max(s, axis=-1, keepdims=True))
    alpha = jnp.exp2(m - m_new)
    p = jnp.exp2(s - m_new)
    l_new = alpha * l + jnp.sum(p, axis=-1, keepdims=True)
    acc_ref[...] = alpha * acc_ref[...] + jnp.dot(p.astype(BF16), v, preferred_element_type=F32)
    return m_new, l_new


def _da_attn_body(slopes_ref, lq1_ref, lk1_ref, lq2_ref, lk2_ref, sg_ref,
                  q_ref, k_ref, v_ref, o_ref, acc1_ref, acc2_ref, *, tq, lambda_init):
    h = pl.program_id(1)
    qi = pl.program_id(2)
    slope = slopes_ref[h]
    q = q_ref[0]
    q1, q2 = q[:, :DA_HEAD_DIM], q[:, DA_HEAD_DIM:]
    row = lax.broadcasted_iota(jnp.int32, (tq, tq), 0)
    col = lax.broadcasted_iota(jnp.int32, (tq, tq), 1)
    rel = (row - col).astype(F32)
    acc1_ref[...] = jnp.zeros_like(acc1_ref)
    acc2_ref[...] = jnp.zeros_like(acc2_ref)

    def block(k0, offset, masked, carry):
        m1, l1, m2, l2 = carry
        kb = k_ref[0, pl.ds(k0, tq), :]
        vb = v_ref[0, pl.ds(k0, tq), :]
        bias = slope * jnp.abs(rel + offset)
        s1 = _nt_dot(q1, kb[:, :DA_HEAD_DIM]) - bias
        s2 = _nt_dot(q2, kb[:, DA_HEAD_DIM:]) - bias
        if masked:
            keep = _chunk_mask(tq, tq)
            s1 = jnp.where(keep, s1, MASK_VALUE)
            s2 = jnp.where(keep, s2, MASK_VALUE)
        m1, l1 = _online_update(s1, m1, l1, acc1_ref, vb)
        m2, l2 = _online_update(s2, m2, l2, acc2_ref, vb)
        return m1, l1, m2, l2

    def interior(j, carry):
        k0 = pl.multiple_of(j * tq, tq)
        return block(k0, ((qi - j) * tq).astype(F32), False, carry)

    neg = jnp.full((tq, 1), MASK_VALUE, F32)
    zero = jnp.zeros((tq, 1), F32)
    carry = lax.fori_loop(0, qi, interior, (neg, zero, neg, zero))
    q0 = pl.multiple_of(qi * tq, tq)
    m1, l1, m2, l2 = block(q0, 0.0, True, carry)

    lam = (jnp.exp(jnp.sum(lq1_ref[...] * lk1_ref[...], axis=-1, keepdims=True))
           - jnp.exp(jnp.sum(lq2_ref[...] * lk2_ref[...], axis=-1, keepdims=True))
           + lambda_init)
    o = acc1_ref[...] / l1 - lam * (acc2_ref[...] / l2)
    o_ref[0] = (_rms(o, sg_ref[...]) * (1.0 - lambda_init)).astype(BF16)


def _da_attn(slopes, lams, sg, q, k, v, *, tq, lambda_init):
    b, s, _ = q.shape
    w = DA_V_DIM
    vec = pl.BlockSpec((1, DA_HEAD_DIM), lambda bi, h, i: (0, 0))
    return pl.pallas_call(
        functools.partial(_da_attn_body, tq=tq, lambda_init=lambda_init),
        grid=(b, DA_HEADS, s // tq),
        in_specs=[
            pl.BlockSpec(memory_space=pltpu.SMEM),
            vec, vec, vec, vec,
            pl.BlockSpec((1, w), lambda bi, h, i: (0, 0)),
            pl.BlockSpec((1, tq, w), lambda bi, h, i: (bi, i, h)),
            pl.BlockSpec((1, s, w), lambda bi, h, i: (bi, 0, h)),
            pl.BlockSpec((1, s, w), lambda bi, h, i: (bi, 0, h)),
        ],
        out_specs=pl.BlockSpec((1, tq, w), lambda bi, h, i: (bi, i, h)),
        out_shape=jax.ShapeDtypeStruct((b, s, DA_HEADS * w), BF16),
        scratch_shapes=[pltpu.VMEM((tq, w), F32), pltpu.VMEM((tq, w), F32)],
        compiler_params=_cparams(("parallel", "parallel", "arbitrary")),
        name="da_attn",
    )(slopes, *lams, sg, q, k, v)


def _mla_attn_body(qn_ref, qr_ref, kn_ref, kr_ref, v_ref, o_ref, acc_ref, *, tq):
    qi = pl.program_id(2)
    q = jnp.concatenate([qn_ref[0], qr_ref[0]], axis=-1)
    acc_ref[...] = jnp.zeros_like(acc_ref)

    def block(k0, masked, carry):
        m, l = carry
        kb = jnp.concatenate([kn_ref[0, pl.ds(k0, tq), :], kr_ref[0, pl.ds(k0, tq), :]], axis=-1)
        vb = v_ref[0, pl.ds(k0, tq), :]
        s = _nt_dot(q, kb)
        if masked:
            s = jnp.where(_chunk_mask(tq, tq), s, MASK_VALUE)
        return _online_update(s, m, l, acc_ref, vb)

    def interior(j, carry):
        return block(pl.multiple_of(j * tq, tq), False, carry)

    neg = jnp.full((tq, 1), MASK_VALUE, F32)
    zero = jnp.zeros((tq, 1), F32)
    carry = lax.fori_loop(0, qi, interior, (neg, zero))
    m, l = block(pl.multiple_of(qi * tq, tq), True, carry)
    o_ref[0] = (acc_ref[...] / l).astype(BF16)


def _mla_attn(qn, qr, kn, kr, v, *, tq):
    b, s, _ = qn.shape
    w = LANES
    q_spec = pl.BlockSpec((1, tq, w), lambda bi, h, i: (bi, i, h))
    kv_spec = pl.BlockSpec((1, s, w), lambda bi, h, i: (bi, 0, h))
    return pl.pallas_call(
        functools.partial(_mla_attn_body, tq=tq),
        grid=(b, MLA_HEADS, s // tq),
        in_specs=[
            q_spec,
            pl.BlockSpec((1, tq, w), lambda bi, h, i: (bi, i, h // 2)),
            kv_spec, kv_spec, kv_spec,
        ],
        out_specs=q_spec,
        out_shape=jax.ShapeDtypeStruct((b, s, MLA_HEADS * w), BF16),
        scratch_shapes=[pltpu.VMEM((tq, w), F32)],
        compiler_params=_cparams(("parallel", "parallel", "arbitrary")),
        name="mla_attn",
    )(qn, qr, kn, kr, v)


def _merge_body(x_ref, ya_ref, yb_ref, gate_a_ref, gate_b_ref, wa_ref, wb_ref, wo_ref, o_ref):
    @pl.when(pl.program_id(1) == 0)
    def _():
        o_ref[...] = x_ref[...]

    a = jnp.dot(ya_ref[...], wa_ref[...], preferred_element_type=F32)
    b = jnp.dot(yb_ref[...], wb_ref[...], preferred_element_type=F32)
    m = (gate_a_ref[...].astype(F32) * a + gate_b_ref[...].astype(F32) * b).astype(BF16)
    o_ref[...] += jnp.dot(m, wo_ref[...], preferred_element_type=F32)


def _merge(x, ya, yb, gates, wa, wb, wo, *, tm, tn):
    t, d = x.shape
    nj = d // tn
    tok = pl.BlockSpec((tm, d), lambda i, j: (i, 0))
    return pl.pallas_call(
        _merge_body,
        grid=(t // tm, nj),
        in_specs=[
            tok, tok, tok,
            pl.BlockSpec((tm, tn), lambda i, j: (i, j)),
            pl.BlockSpec((tm, tn), lambda i, j: (i, j + nj)),
            pl.BlockSpec((d, tn), lambda i, j: (0, j)),
            pl.BlockSpec((d, tn), lambda i, j: (0, j)),
            pl.BlockSpec((tn, d), lambda i, j: (j, 0)),
        ],
        out_specs=tok,
        out_shape=jax.ShapeDtypeStruct((t, d), F32),
        compiler_params=_cparams(("parallel", "arbitrary")),
        name="merge",
    )(x, ya, yb, gates, gates, wa, wb, wo)


def _rot_cols(w):
    half = w.shape[-1] // 2
    return jnp.concatenate([-w[..., half:], w[..., :half]], axis=-1)


def _swap_halves(g):
    half = g.shape[-1] // 2
    return jnp.concatenate([g[..., half:], g[..., :half]], axis=-1)


def _pick_tile(n, pref):
    return pref if n % pref == 0 else n


def kernel(x, ffn1_norm_g, ffn1_w1, ffn1_w3, ffn1_w2, mix_norm_g, w_in, da_q_norm_g, da_k_norm_g, da_lambda_q1, da_lambda_k1, da_lambda_q2, da_lambda_k2, da_subln_g, mla_q_a_norm_g, mla_w_qb, mla_kv_a_norm_g, mla_w_kvb, mla_q_norm_g, mla_k_norm_g, w_branch_a, w_branch_b, w_out, ffn2_norm_g, ffn2_w1, ffn2_w3, ffn2_w2):
    bsz, seq, d = x.shape
    t = bsz * seq
    depth = ffn1_norm_g.shape[0]
    tm = _pick_tile(t, 512)
    tq = _pick_tile(seq, 256)
    row = lambda v: v.reshape(1, -1).astype(F32)

    inv = ROPE_THETA ** (-jnp.arange(0, MLA_ROPE_DIM, 2, dtype=F32) / MLA_ROPE_DIM)
    ang = jnp.arange(seq, dtype=F32)[:, None] * inv[None, :]
    cos2 = jnp.tile(jnp.cos(ang), (1, 4))
    sin2 = jnp.tile(jnp.sin(ang), (1, 4))
    slopes = (2.0 ** (-8.0 * jnp.arange(1, DA_HEADS + 1, dtype=F32) / DA_HEADS)) * LOG2E

    xt = x.reshape(t, d)
    for l in range(depth):
        lambda_init = 0.8 - 0.6 * math.exp(-0.3 * l)

        xt = _ffn(xt, row(ffn1_norm_g[l]), ffn1_w1[l].astype(BF16), ffn1_w3[l].astype(BF16),
                  (0.5 * ffn1_w2[l]).astype(BF16), tm=tm, tf=512)

        w = w_in[l]
        o = [0]
        for n in (2 * DA_HEADS * DA_HEAD_DIM, 2 * DA_HEADS * DA_HEAD_DIM, DA_HEADS * DA_V_DIM,
                  MLA_Q_RANK, MLA_KV_RANK, MLA_ROPE_DIM, d, d):
            o.append(o[-1] + n)
        w_kr = w[:, o[5]:o[6]]
        w_kr_rot = _rot_cols(w_kr)
        w_cat = jnp.concatenate(
            [w[:, o[0]:o[5]], w_kr, w_kr, w_kr_rot, w_kr_rot, w[:, o[6]:]], axis=1).astype(BF16)
        q_scale = DA_HEAD_DIM ** -0.5 * LOG2E
        q_da, k_da, v_da, lat, gates = _in_proj(
            xt, row(mix_norm_g[l]), w_cat, row(da_q_norm_g[l]) * q_scale, row(da_k_norm_g[l]), tm=tm)

        wq = mla_w_qb[l].reshape(MLA_Q_RANK, MLA_HEADS, MLA_QK_DIM)
        wq_nope = wq[:, :, :MLA_NOPE_DIM].reshape(MLA_Q_RANK, MLA_HEADS // 2, 2 * LANES)
        wq_rope = wq[:, :, MLA_NOPE_DIM:]
        wq_rot = _rot_cols(wq_rope).reshape(MLA_Q_RANK, MLA_HEADS // 2, LANES)
        wq_rope = wq_rope.reshape(MLA_Q_RANK, MLA_HEADS // 2, LANES)
        wq_packed = jnp.concatenate([wq_nope, wq_rope, wq_rot], axis=-1).reshape(
            MLA_Q_RANK, MLA_HEADS * 2 * LANES).astype(BF16)
        mla_scale = MLA_QK_DIM ** -0.5 * LOG2E
        gq, gk = mla_q_norm_g[l], mla_k_norm_g[l]
        gains = (
            row(gq[:MLA_NOPE_DIM]) * mla_scale,
            row(jnp.tile(gq[MLA_NOPE_DIM:], 2)) * mla_scale,
            row(jnp.tile(_swap_halves(gq[MLA_NOPE_DIM:]), 2)) * mla_scale,
            row(gk[:MLA_NOPE_DIM]),
            row(jnp.tile(gk[MLA_NOPE_DIM:], 2)),
            row(jnp.tile(_swap_halves(gk[MLA_NOPE_DIM:]), 2)),
        )
        qn, qr, kn, kr, v_mla = _mla_proj(
            lat, row(mla_q_a_norm_g[l]), row(mla_kv_a_norm_g[l]), wq_packed,
            mla_w_kvb[l].astype(BF16), gains, cos2, sin2, tm=_pick_tile(seq, 256), seq=seq)

        b3 = lambda a: a.reshape(bsz, seq, a.shape[-1])
        lams = (row(da_lambda_q1[l]), row(da_lambda_k1[l]), row(da_lambda_q2[l]), row(da_lambda_k2[l]))
        y_a = _da_attn(slopes, lams, row(da_subln_g[l]), b3(q_da), b3(k_da), b3(v_da),
                       tq=tq, lambda_init=lambda_init)
        y_b = _mla_attn(b3(qn), b3(qr), b3(kn), b3(kr), b3(v_mla), tq=tq)

        xt = _merge(xt, y_a.reshape(t, -1), y_b.reshape(t, -1), gates,
                    w_branch_a[l].astype(BF16), w_branch_b[l].astype(BF16), w_out[l].astype(BF16),
                    tm=tm, tn=512)

        xt = _ffn(xt, row(ffn2_norm_g[l]), ffn2_w1[l].astype(BF16), ffn2_w3[l].astype(BF16),
                  (0.5 * ffn2_w2[l]).astype(BF16), tm=tm, tf=512)
    return xt.reshape(bsz, seq, d)
```

```python
import functools
import math

import jax
import jax.numpy as jnp
from jax import lax
from jax.experimental import pallas as pl
from jax.experimental.pallas import tpu as pltpu

F32 = jnp.float32
BF16 = jnp.bfloat16

EPS = 1e-6
CHUNK = 64
DA_HEADS = 8
DA_HEAD_DIM = 128
DA_V_DIM = 2 * DA_HEAD_DIM
MLA_HEADS = 16
MLA_Q_RANK = 768
MLA_KV_RANK = 512
MLA_NOPE_DIM = 128
MLA_ROPE_DIM = 64
MLA_V_DIM = 128
MLA_QK_DIM = MLA_NOPE_DIM + MLA_ROPE_DIM
ROPE_THETA = 10000.0
LOG2E = math.log2(math.e)
MASK_VALUE = -1e30

LANES = 128
V7X_VMEM_BYTES = 64 * 1024 * 1024
VMEM_LIMIT = V7X_VMEM_BYTES - 8 * 1024 * 1024


def _cparams(sem):
    return pltpu.CompilerParams(dimension_semantics=sem, vmem_limit_bytes=VMEM_LIMIT)


def _rms(x, g):
    ms = jnp.mean(x * x, axis=-1, keepdims=True)
    return x * lax.rsqrt(ms + EPS) * g


def _nt_dot(a, b):
    return lax.dot_general(a, b, (((1,), (1,)), ((), ())), preferred_element_type=F32)


def _ffn_body(x_ref, g_ref, w1_ref, w3_ref, w2_ref, o_ref, h_ref):
    @pl.when(pl.program_id(1) == 0)
    def _():
        x = x_ref[...]
        h_ref[...] = _rms(x, g_ref[...]).astype(BF16)
        o_ref[...] = x

    h = h_ref[...]
    a = jnp.dot(h, w1_ref[...], preferred_element_type=F32)
    b = jnp.dot(h, w3_ref[...], preferred_element_type=F32)
    u = (a * jax.nn.sigmoid(a) * b).astype(BF16)
    o_ref[...] += jnp.dot(u, w2_ref[...], preferred_element_type=F32)


def _ffn(x, g, w1, w3, w2_half, *, tm, tf):
    t, d = x.shape
    f = w1.shape[1]
    return pl.pallas_call(
        _ffn_body,
        grid=(t // tm, f // tf),
        in_specs=[
            pl.BlockSpec((tm, d), lambda i, j: (i, 0)),
            pl.BlockSpec((1, d), lambda i, j: (0, 0)),
            pl.BlockSpec((d, tf), lambda i, j: (0, j)),
            pl.BlockSpec((d, tf), lambda i, j: (0, j)),
            pl.BlockSpec((tf, d), lambda i, j: (j, 0)),
        ],
        out_specs=pl.BlockSpec((tm, d), lambda i, j: (i, 0)),
        out_shape=jax.ShapeDtypeStruct((t, d), F32),
        scratch_shapes=[pltpu.VMEM((tm, d), BF16)],
        compiler_params=_cparams(("parallel", "arbitrary")),
        name="ffn",
    )(x, g, w1, w3, w2_half)


PROJ_TN = 512
_Q0, _K0, _V0, _C0, _G0, _NJ = 0, 4, 8, 12, 15, 23


def _head_norm(acc, g):
    parts = []
    for c in range(acc.shape[1] // LANES):
        blk = acc[:, c * LANES:(c + 1) * LANES]
        ms = jnp.mean(blk * blk, axis=-1, keepdims=True)
        parts.append(blk * lax.rsqrt(ms + EPS) * g)
    return jnp.concatenate(parts, axis=-1)


def _proj_body(x_ref, g_ref, w_ref, qg_ref, kg_ref,
               q_ref, k_ref, v_ref, c_ref, gate_ref, h_ref):
    j = pl.program_id(1)

    @pl.when(j == 0)
    def _():
        h_ref[...] = _rms(x_ref[...], g_ref[...]).astype(BF16)

    acc = jnp.dot(h_ref[...], w_ref[...], preferred_element_type=F32)

    @pl.when(j < _K0)
    def _():
        q_ref[...] = _head_norm(acc, qg_ref[...]).astype(BF16)

    @pl.when((j >= _K0) & (j < _V0))
    def _():
        k_ref[...] = _head_norm(acc, kg_ref[...]).astype(BF16)

    @pl.when((j >= _V0) & (j < _C0))
    def _():
        v_ref[...] = acc.astype(BF16)

    @pl.when((j >= _C0) & (j < _G0))
    def _():
        c_ref[...] = acc

    @pl.when(j >= _G0)
    def _():
        gate_ref[...] = jax.nn.sigmoid(acc).astype(BF16)


def _in_proj(x, g, w_cat, qg, kg, *, tm):
    t, d = x.shape
    tn = PROJ_TN

    def sect(j0, n):
        return pl.BlockSpec((tm, tn), lambda i, j: (i, jnp.clip(j - j0, 0, n - 1)))

    return pl.pallas_call(
        _proj_body,
        grid=(t // tm, _NJ),
        in_specs=[
            pl.BlockSpec((tm, d), lambda i, j: (i, 0)),
            pl.BlockSpec((1, d), lambda i, j: (0, 0)),
            pl.BlockSpec((d, tn), lambda i, j: (0, j)),
            pl.BlockSpec((1, LANES), lambda i, j: (0, 0)),
            pl.BlockSpec((1, LANES), lambda i, j: (0, 0)),
        ],
        out_specs=[
            sect(_Q0, _K0 - _Q0), sect(_K0, _V0 - _K0), sect(_V0, _C0 - _V0),
            sect(_C0, _G0 - _C0), sect(_G0, _NJ - _G0),
        ],
        out_shape=[
            jax.ShapeDtypeStruct((t, (_K0 - _Q0) * tn), BF16),
            jax.ShapeDtypeStruct((t, (_V0 - _K0) * tn), BF16),
            jax.ShapeDtypeStruct((t, (_C0 - _V0) * tn), BF16),
            jax.ShapeDtypeStruct((t, (_G0 - _C0) * tn), F32),
            jax.ShapeDtypeStruct((t, (_NJ - _G0) * tn), BF16),
        ],
        scratch_shapes=[pltpu.VMEM((tm, d), BF16)],
        compiler_params=_cparams(("parallel", "arbitrary")),
        name="in_proj",
    )(x, g, w_cat, qg, kg)


def _mla_proj_body(c_ref, qa_g_ref, kva_g_ref, wq_ref, wkv_ref,
                   gq_nope_ref, gq_rope_ref, gq_rot_ref,
                   gk_nope_ref, gk_rope_ref, gk_rot_ref, cos_ref, sin_ref,
                   qn_ref, qr_ref, kn_ref, kr_ref, v_ref):
    c = c_ref[...]
    o_kv = MLA_Q_RANK
    o_kr = MLA_Q_RANK + MLA_KV_RANK
    cqn = _rms(c[:, :o_kv], qa_g_ref[...]).astype(BF16)
    ckvn = _rms(c[:, o_kv:o_kr], kva_g_ref[...]).astype(BF16)
    k_rope = c[:, o_kr:o_kr + LANES]
    k_rot = c[:, o_kr + LANES:o_kr + 2 * LANES]
    cos = cos_ref[...]
    sin = sin_ref[...]
    first = lax.broadcasted_iota(jnp.int32, (1, LANES), 1) < MLA_ROPE_DIM
    k_roped = k_rope * gk_rope_ref[...] * cos + k_rot * gk_rot_ref[...] * sin
    k_rope_sq = 0.5 * jnp.sum(k_rope * k_rope, axis=-1, keepdims=True)
    inv_d = 1.0 / MLA_QK_DIM

    for p in range(MLA_HEADS // 2):
        cols = slice(p * 4 * LANES, (p + 1) * 4 * LANES)
        qraw = jnp.dot(cqn, wq_ref[:, cols], preferred_element_type=F32)
        na, nb = qraw[:, :LANES], qraw[:, LANES:2 * LANES]
        rp, rt = qraw[:, 2 * LANES:3 * LANES], qraw[:, 3 * LANES:]
        rp2 = rp * rp
        sa = jnp.sum(jnp.where(first, rp2, 0.0), axis=-1, keepdims=True)
        sb = jnp.sum(jnp.where(first, 0.0, rp2), axis=-1, keepdims=True)
        ra = lax.rsqrt((jnp.sum(na * na, axis=-1, keepdims=True) + sa) * inv_d + EPS)
        rb = lax.rsqrt((jnp.sum(nb * nb, axis=-1, keepdims=True) + sb) * inv_d + EPS)
        qn_ref[:, (2 * p) * LANES:(2 * p + 1) * LANES] = (na * ra * gq_nope_ref[...]).astype(BF16)
        qn_ref[:, (2 * p + 1) * LANES:(2 * p + 2) * LANES] = (nb * rb * gq_nope_ref[...]).astype(BF16)
        roped = rp * gq_rope_ref[...] * cos + rt * gq_rot_ref[...] * sin
        qr_ref[:, p * LANES:(p + 1) * LANES] = (roped * jnp.where(first, ra, rb)).astype(BF16)

        kvraw = jnp.dot(ckvn, wkv_ref[:, cols], preferred_element_type=F32)
        kna, va = kvraw[:, :LANES], kvraw[:, LANES:2 * LANES]
        knb, vb = kvraw[:, 2 * LANES:3 * LANES], kvraw[:, 3 * LANES:]
        rka = lax.rsqrt((jnp.sum(kna * kna, axis=-1, keepdims=True) + k_rope_sq) * inv_d + EPS)
        rkb = lax.rsqrt((jnp.sum(knb * knb, axis=-1, keepdims=True) + k_rope_sq) * inv_d + EPS)
        ca = slice((2 * p) * LANES, (2 * p + 1) * LANES)
        cb = slice((2 * p + 1) * LANES, (2 * p + 2) * LANES)
        kn_ref[:, ca] = (kna * rka * gk_nope_ref[...]).astype(BF16)
        kn_ref[:, cb] = (knb * rkb * gk_nope_ref[...]).astype(BF16)
        kr_ref[:, ca] = jnp.where(first, k_roped * rka, 0.0).astype(BF16)
        kr_ref[:, cb] = jnp.where(first, 0.0, k_roped * rkb).astype(BF16)
        v_ref[:, ca] = va.astype(BF16)
        v_ref[:, cb] = vb.astype(BF16)


def _mla_proj(c, qa_g, kva_g, wq, wkv, gains, cos2, sin2, *, tm, seq):
    t = c.shape[0]
    n_pos = seq // tm
    const = lambda shape: pl.BlockSpec(shape, lambda i: (0, 0))
    hw = MLA_HEADS * LANES
    tok = lambda w: pl.BlockSpec((tm, w), lambda i: (i, 0))
    return pl.pallas_call(
        _mla_proj_body,
        grid=(t // tm,),
        in_specs=[
            tok(c.shape[1]), const(qa_g.shape), const(kva_g.shape),
            const(wq.shape), const(wkv.shape),
        ] + [const((1, LANES))] * 6 + [
            pl.BlockSpec((tm, LANES), lambda i: (i % n_pos, 0)),
            pl.BlockSpec((tm, LANES), lambda i: (i % n_pos, 0)),
        ],
        out_specs=[tok(hw), tok(hw // 2), tok(hw), tok(hw), tok(hw)],
        out_shape=[
            jax.ShapeDtypeStruct((t, hw), BF16),
            jax.ShapeDtypeStruct((t, hw // 2), BF16),
            jax.ShapeDtypeStruct((t, hw), BF16),
            jax.ShapeDtypeStruct((t, hw), BF16),
            jax.ShapeDtypeStruct((t, hw), BF16),
        ],
        compiler_params=_cparams(("parallel",)),
        name="mla_proj",
    )(c, qa_g, kva_g, wq, wkv, *gains, cos2, sin2)


CHUNK_SHIFT = CHUNK.bit_length() - 1


def _chunk_mask(q0, k0, tq, tk):
    row = lax.broadcasted_iota(jnp.int32, (tq, 1), 0)
    col = lax.broadcasted_iota(jnp.int32, (1, tk), 1)
    return ((k0 + col) >> CHUNK_SHIFT) <= ((q0 + row) >> CHUNK_SHIFT)


def _online_update(s, m, l, acc_ref, v):
    m_new = jnp.maximum(m, jnp.max(s, axis=-1, keepdims=True))
    alpha = jnp.exp2(m - m_new)
    p = jnp.exp2(s - m_new)
    l_new = alpha * l + jnp.sum(p, axis=-1, keepdims=True)
    acc_ref[...] = alpha * acc_ref[...] + jnp.dot(p.astype(BF16), v, preferred_element_type=F32)
    return m_new, l_new


def _da_attn_body(slopes_ref, lq1_ref, lk1_ref, lq2_ref, lk2_ref, sg_ref,
                  q_ref, k_ref, v_ref, o_ref, acc1_ref, acc2_ref, *, tq, tk, lambda_init):
    h = pl.program_id(1)
    qi = pl.program_id(2)
    slope = slopes_ref[h]
    q = q_ref[0]
    q1, q2 = q[:, :DA_HEAD_DIM], q[:, DA_HEAD_DIM:]
    q0 = qi * tq
    n_full = qi // (tk // tq)
    row = lax.broadcasted_iota(jnp.int32, (tq, 1), 0)
    col = lax.broadcasted_iota(jnp.int32, (1, tk), 1)
    acc1_ref[...] = jnp.zeros_like(acc1_ref)
    acc2_ref[...] = jnp.zeros_like(acc2_ref)

    def block(j, last, carry):
        m1, l1, m2, l2 = carry
        k0 = pl.multiple_of(j * tk, tk)
        kb = k_ref[0, pl.ds(k0, tk), :]
        vb = v_ref[0, pl.ds(k0, tk), :]
        if last:
            dist = jnp.abs((q0 - k0) + row - col)
            bias = slope * (row - dist).astype(F32)
        else:
            bias = slope * ((k0 - q0) + col).astype(F32)
        s1 = _nt_dot(q1, kb[:, :DA_HEAD_DIM]) + bias
        s2 = _nt_dot(q2, kb[:, DA_HEAD_DIM:]) + bias
        if last:
            keep = _chunk_mask(q0, k0, tq, tk)
            s1 = jnp.where(keep, s1, MASK_VALUE)
            s2 = jnp.where(keep, s2, MASK_VALUE)
        m1, l1 = _online_update(s1, m1, l1, acc1_ref, vb)
        m2, l2 = _online_update(s2, m2, l2, acc2_ref, vb)
        return m1, l1, m2, l2

    neg = jnp.full((tq, 1), MASK_VALUE, F32)
    zero = jnp.zeros((tq, 1), F32)
    carry = lax.fori_loop(0, n_full, lambda j, c: block(j, False, c), (neg, zero, neg, zero))
    m1, l1, m2, l2 = block(n_full, True, carry)

    lam = (jnp.exp(jnp.sum(lq1_ref[...] * lk1_ref[...], axis=-1, keepdims=True))
           - jnp.exp(jnp.sum(lq2_ref[...] * lk2_ref[...], axis=-1, keepdims=True))
           + lambda_init)
    o = acc1_ref[...] / l1 - lam * (acc2_ref[...] / l2)
    o_ref[0] = (_rms(o, sg_ref[...]) * (1.0 - lambda_init)).astype(BF16)


def _da_attn(slopes, lams, sg, q, k, v, *, tq, tk, lambda_init):
    b, s, _ = q.shape
    w = DA_V_DIM
    vec = pl.BlockSpec((1, DA_HEAD_DIM), lambda bi, h, i: (0, 0))
    return pl.pallas_call(
        functools.partial(_da_attn_body, tq=tq, tk=tk, lambda_init=lambda_init),
        grid=(b, DA_HEADS, s // tq),
        in_specs=[
            pl.BlockSpec(memory_space=pltpu.SMEM),
            vec, vec, vec, vec,
            pl.BlockSpec((1, w), lambda bi, h, i: (0, 0)),
            pl.BlockSpec((1, tq, w), lambda bi, h, i: (bi, i, h)),
            pl.BlockSpec((1, s, w), lambda bi, h, i: (bi, 0, h)),
            pl.BlockSpec((1, s, w), lambda bi, h, i: (bi, 0, h)),
        ],
        out_specs=pl.BlockSpec((1, tq, w), lambda bi, h, i: (bi, i, h)),
        out_shape=jax.ShapeDtypeStruct((b, s, DA_HEADS * w), BF16),
        scratch_shapes=[pltpu.VMEM((tq, w), F32), pltpu.VMEM((tq, w), F32)],
        compiler_params=_cparams(("parallel", "parallel", "arbitrary")),
        name="da_attn",
    )(slopes, *lams, sg, q, k, v)


def _mla_attn_body(qn_ref, qr_ref, kn_ref, kr_ref, v_ref, o_ref, acc_a_ref, acc_b_ref, *, tq, tk):
    qi = pl.program_id(2)
    q0 = qi * tq
    n_full = qi // (tk // tq)
    qn = qn_ref[0]
    qr = qr_ref[0]
    qa = jnp.concatenate([qn[:, :LANES], qr], axis=-1)
    qb = jnp.concatenate([qn[:, LANES:], qr], axis=-1)
    acc_a_ref[...] = jnp.zeros_like(acc_a_ref)
    acc_b_ref[...] = jnp.zeros_like(acc_b_ref)

    def block(j, last, carry):
        ma, la, mb, lb = carry
        k0 = pl.multiple_of(j * tk, tk)
        kn = kn_ref[0, pl.ds(k0, tk), :]
        kr = kr_ref[0, pl.ds(k0, tk), :]
        vb = v_ref[0, pl.ds(k0, tk), :]
        sa = _nt_dot(qa, jnp.concatenate([kn[:, :LANES], kr[:, :LANES]], axis=-1))
        sb = _nt_dot(qb, jnp.concatenate([kn[:, LANES:], kr[:, LANES:]], axis=-1))
        if last:
            keep = _chunk_mask(q0, k0, tq, tk)
            sa = jnp.where(keep, sa, MASK_VALUE)
            sb = jnp.where(keep, sb, MASK_VALUE)
        ma, la = _online_update(sa, ma, la, acc_a_ref, vb[:, :LANES])
        mb, lb = _online_update(sb, mb, lb, acc_b_ref, vb[:, LANES:])
        return ma, la, mb, lb

    neg = jnp.full((tq, 1), MASK_VALUE, F32)
    zero = jnp.zeros((tq, 1), F32)
    carry = lax.fori_loop(0, n_full, lambda j, c: block(j, False, c), (neg, zero, neg, zero))
    ma, la, mb, lb = block(n_full, True, carry)
    o_ref[0] = jnp.concatenate([acc_a_ref[...] / la, acc_b_ref[...] / lb], axis=-1).astype(BF16)


def _mla_attn(qn, qr, kn, kr, v, *, tq, tk):
    b, s, _ = qn.shape
    w = 2 * LANES
    q_spec = pl.BlockSpec((1, tq, w), lambda bi, h, i: (bi, i, h))
    kv_spec = pl.BlockSpec((1, s, w), lambda bi, h, i: (bi, 0, h))
    return pl.pallas_call(
        functools.partial(_mla_attn_body, tq=tq, tk=tk),
        grid=(b, MLA_HEADS // 2, s // tq),
        in_specs=[
            q_spec,
            pl.BlockSpec((1, tq, LANES), lambda bi, h, i: (bi, i, h)),
            kv_spec, kv_spec, kv_spec,
        ],
        out_specs=q_spec,
        out_shape=jax.ShapeDtypeStruct((b, s, MLA_HEADS * LANES), BF16),
        scratch_shapes=[pltpu.VMEM((tq, LANES), F32), pltpu.VMEM((tq, LANES), F32)],
        compiler_params=_cparams(("parallel", "parallel", "arbitrary")),
        name="mla_attn",
    )(qn, qr, kn, kr, v)


def _merge_body(x_ref, ya_ref, yb_ref, gate_a_ref, gate_b_ref, wa_ref, wb_ref, wo_ref, o_ref):
    @pl.when(pl.program_id(1) == 0)
    def _():
        o_ref[...] = x_ref[...]

    a = jnp.dot(ya_ref[...], wa_ref[...], preferred_element_type=F32)
    b = jnp.dot(yb_ref[...], wb_ref[...], preferred_element_type=F32)
    m = (gate_a_ref[...].astype(F32) * a + gate_b_ref[...].astype(F32) * b).astype(BF16)
    o_ref[...] += jnp.dot(m, wo_ref[...], preferred_element_type=F32)


def _merge(x, ya, yb, gates, wa, wb, wo, *, tm, tn):
    t, d = x.shape
    nj = d // tn
    tok = pl.BlockSpec((tm, d), lambda i, j: (i, 0))
    return pl.pallas_call(
        _merge_body,
        grid=(t // tm, nj),
        in_specs=[
            tok, tok, tok,
            pl.BlockSpec((tm, tn), lambda i, j: (i, j)),
            pl.BlockSpec((tm, tn), lambda i, j: (i, j + nj)),
            pl.BlockSpec((d, tn), lambda i, j: (0, j)),
            pl.BlockSpec((d, tn), lambda i, j: (0, j)),
            pl.BlockSpec((tn, d), lambda i, j: (j, 0)),
        ],
        out_specs=tok,
        out_shape=jax.ShapeDtypeStruct((t, d), F32),
        compiler_params=_cparams(("parallel", "arbitrary")),
        name="merge",
    )(x, ya, yb, gates, gates, wa, wb, wo)


def _rot_cols(w):
    half = w.shape[-1] // 2
    return jnp.concatenate([-w[..., half:], w[..., :half]], axis=-1)


def _swap_halves(g):
    half = g.shape[-1] // 2
    return jnp.concatenate([g[..., half:], g[..., :half]], axis=-1)


def _pick_tile(n, pref):
    return pref if n % pref == 0 else n


def kernel(x, ffn1_norm_g, ffn1_w1, ffn1_w3, ffn1_w2, mix_norm_g, w_in, da_q_norm_g, da_k_norm_g, da_lambda_q1, da_lambda_k1, da_lambda_q2, da_lambda_k2, da_subln_g, mla_q_a_norm_g, mla_w_qb, mla_kv_a_norm_g, mla_w_kvb, mla_q_norm_g, mla_k_norm_g, w_branch_a, w_branch_b, w_out, ffn2_norm_g, ffn2_w1, ffn2_w3, ffn2_w2):
    bsz, seq, d = x.shape
    t = bsz * seq
    depth = ffn1_norm_g.shape[0]
    tm = _pick_tile(t, 512)
    tq = _pick_tile(seq, 256)
    tk = _pick_tile(seq, 1024)
    row = lambda v: v.reshape(1, -1).astype(F32)

    inv = ROPE_THETA ** (-jnp.arange(0, MLA_ROPE_DIM, 2, dtype=F32) / MLA_ROPE_DIM)
    ang = jnp.arange(seq, dtype=F32)[:, None] * inv[None, :]
    cos2 = jnp.tile(jnp.cos(ang), (1, 4))
    sin2 = jnp.tile(jnp.sin(ang), (1, 4))
    slopes = (2.0 ** (-8.0 * jnp.arange(1, DA_HEADS + 1, dtype=F32) / DA_HEADS)) * LOG2E

    xt = x.reshape(t, d)
    for l in range(depth):
        lambda_init = 0.8 - 0.6 * math.exp(-0.3 * l)

        xt = _ffn(xt, row(ffn1_norm_g[l]), ffn1_w1[l].astype(BF16), ffn1_w3[l].astype(BF16),
                  (0.5 * ffn1_w2[l]).astype(BF16), tm=tm, tf=512)

        w = w_in[l]
        o = [0]
        for n in (2 * DA_HEADS * DA_HEAD_DIM, 2 * DA_HEADS * DA_HEAD_DIM, DA_HEADS * DA_V_DIM,
                  MLA_Q_RANK, MLA_KV_RANK, MLA_ROPE_DIM, d, d):
            o.append(o[-1] + n)
        w_kr = w[:, o[5]:o[6]]
        w_kr_rot = _rot_cols(w_kr)
        w_cat = jnp.concatenate(
            [w[:, o[0]:o[5]], w_kr, w_kr, w_kr_rot, w_kr_rot, w[:, o[6]:]], axis=1).astype(BF16)
        q_scale = DA_HEAD_DIM ** -0.5 * LOG2E
        q_da, k_da, v_da, lat, gates = _in_proj(
            xt, row(mix_norm_g[l]), w_cat, row(da_q_norm_g[l]) * q_scale, row(da_k_norm_g[l]), tm=tm)

        wq = mla_w_qb[l].reshape(MLA_Q_RANK, MLA_HEADS, MLA_QK_DIM)
        wq_nope = wq[:, :, :MLA_NOPE_DIM].reshape(MLA_Q_RANK, MLA_HEADS // 2, 2 * LANES)
        wq_rope = wq[:, :, MLA_NOPE_DIM:]
        wq_rot = _rot_cols(wq_rope).reshape(MLA_Q_RANK, MLA_HEADS // 2, LANES)
        wq_rope = wq_rope.reshape(MLA_Q_RANK, MLA_HEADS // 2, LANES)
        wq_packed = jnp.concatenate([wq_nope, wq_rope, wq_rot], axis=-1).reshape(
            MLA_Q_RANK, MLA_HEADS * 2 * LANES).astype(BF16)
        mla_scale = MLA_QK_DIM ** -0.5 * LOG2E
        gq, gk = mla_q_norm_g[l], mla_k_norm_g[l]
        gains = (
            row(gq[:MLA_NOPE_DIM]) * mla_scale,
            row(jnp.tile(gq[MLA_NOPE_DIM:], 2)) * mla_scale,
            row(jnp.tile(_swap_halves(gq[MLA_NOPE_DIM:]), 2)) * mla_scale,
            row(gk[:MLA_NOPE_DIM]),
            row(jnp.tile(gk[MLA_NOPE_DIM:], 2)),
            row(jnp.tile(_swap_halves(gk[MLA_NOPE_DIM:]), 2)),
        )
        qn, qr, kn, kr, v_mla = _mla_proj(
            lat, row(mla_q_a_norm_g[l]), row(mla_kv_a_norm_g[l]), wq_packed,
            mla_w_kvb[l].astype(BF16), gains, cos2, sin2, tm=_pick_tile(seq, 256), seq=seq)

        b3 = lambda a: a.reshape(bsz, seq, a.shape[-1])
        lams = (row(da_lambda_q1[l]), row(da_lambda_k1[l]), row(da_lambda_q2[l]), row(da_lambda_k2[l]))
        y_a = _da_attn(slopes, lams, row(da_subln_g[l]), b3(q_da), b3(k_da), b3(v_da),
                       tq=tq, tk=tk, lambda_init=lambda_init)
        y_b = _mla_attn(b3(qn), b3(qr), b3(kn), b3(kr), b3(v_mla), tq=tq, tk=tk)

        xt = _merge(xt, y_a.reshape(t, -1), y_b.reshape(t, -1), gates,
                    w_branch_a[l].astype(BF16), w_branch_b[l].astype(BF16), w_out[l].astype(BF16),
                    tm=tm, tn=512)

        xt = _ffn(xt, row(ffn2_norm_g[l]), ffn2_w1[l].astype(BF16), ffn2_w3[l].astype(BF16),
                  (0.5 * ffn2_w2[l]).astype(BF16), tm=tm, tf=512)
    return xt.reshape(bsz, seq, d)
```

```python
import functools
import math

import jax
import jax.numpy as jnp
from jax import lax
from jax.experimental import pallas as pl
from jax.experimental.pallas import tpu as pltpu

F32 = jnp.float32
BF16 = jnp.bfloat16

EPS = 1e-6
CHUNK = 64
DA_HEADS = 8
DA_HEAD_DIM = 128
DA_V_DIM = 2 * DA_HEAD_DIM
MLA_HEADS = 16
MLA_Q_RANK = 768
MLA_KV_RANK = 512
MLA_NOPE_DIM = 128
MLA_ROPE_DIM = 64
MLA_V_DIM = 128
MLA_QK_DIM = MLA_NOPE_DIM + MLA_ROPE_DIM
ROPE_THETA = 10000.0
LOG2E = math.log2(math.e)
MASK_VALUE = -1e30

LANES = 128
V7X_VMEM_BYTES = 64 * 1024 * 1024
VMEM_LIMIT = V7X_VMEM_BYTES - 8 * 1024 * 1024


def _cparams(sem):
    return pltpu.CompilerParams(dimension_semantics=sem, vmem_limit_bytes=VMEM_LIMIT)


def _rms(x, g):
    ms = jnp.mean(x * x, axis=-1, keepdims=True)
    return x * lax.rsqrt(ms + EPS) * g


def _nt_dot(a, b):
    return lax.dot_general(a, b, (((1,), (1,)), ((), ())), preferred_element_type=F32)


def _ffn_body(x_ref, g_ref, w1_ref, w3_ref, w2_ref, o_ref, h_ref):
    @pl.when(pl.program_id(1) == 0)
    def _():
        x = x_ref[...]
        h_ref[...] = _rms(x, g_ref[...]).astype(BF16)
        o_ref[...] = x

    h = h_ref[...]
    a = jnp.dot(h, w1_ref[...], preferred_element_type=F32)
    b = jnp.dot(h, w3_ref[...], preferred_element_type=F32)
    u = (a * jax.nn.sigmoid(a) * b).astype(BF16)
    o_ref[...] += jnp.dot(u, w2_ref[...], preferred_element_type=F32)


def _ffn(x, g, w1, w3, w2_half, *, tm, tf):
    t, d = x.shape
    f = w1.shape[1]
    return pl.pallas_call(
        _ffn_body,
        grid=(t // tm, f // tf),
        in_specs=[
            pl.BlockSpec((tm, d), lambda i, j: (i, 0)),
            pl.BlockSpec((1, d), lambda i, j: (0, 0)),
            pl.BlockSpec((d, tf), lambda i, j: (0, j)),
            pl.BlockSpec((d, tf), lambda i, j: (0, j)),
            pl.BlockSpec((tf, d), lambda i, j: (j, 0)),
        ],
        out_specs=pl.BlockSpec((tm, d), lambda i, j: (i, 0)),
        out_shape=jax.ShapeDtypeStruct((t, d), F32),
        scratch_shapes=[pltpu.VMEM((tm, d), BF16)],
        compiler_params=_cparams(("parallel", "arbitrary")),
        name="ffn",
    )(x, g, w1, w3, w2_half)


PROJ_TN = 512
_Q0, _K0, _V0, _C0, _G0, _NJ = 0, 4, 8, 12, 15, 23


def _head_norm(acc, g):
    parts = []
    for c in range(acc.shape[1] // LANES):
        blk = acc[:, c * LANES:(c + 1) * LANES]
        ms = jnp.mean(blk * blk, axis=-1, keepdims=True)
        parts.append(blk * lax.rsqrt(ms + EPS) * g)
    return jnp.concatenate(parts, axis=-1)


def _proj_body(x_ref, g_ref, w_ref, qg_ref, kg_ref,
               q_ref, k_ref, v_ref, c_ref, gate_ref, h_ref):
    j = pl.program_id(1)

    @pl.when(j == 0)
    def _():
        h_ref[...] = _rms(x_ref[...], g_ref[...]).astype(BF16)

    acc = jnp.dot(h_ref[...], w_ref[...], preferred_element_type=F32)

    @pl.when(j < _K0)
    def _():
        q_ref[...] = _head_norm(acc, qg_ref[...]).astype(BF16)

    @pl.when((j >= _K0) & (j < _V0))
    def _():
        k_ref[...] = _head_norm(acc, kg_ref[...]).astype(BF16)

    @pl.when((j >= _V0) & (j < _C0))
    def _():
        v_ref[...] = acc.astype(BF16)

    @pl.when((j >= _C0) & (j < _G0))
    def _():
        c_ref[...] = acc

    @pl.when(j >= _G0)
    def _():
        gate_ref[...] = jax.nn.sigmoid(acc).astype(BF16)


def _in_proj(x, g, w_cat, qg, kg, *, tm):
    t, d = x.shape
    tn = PROJ_TN

    def sect(j0, n):
        return pl.BlockSpec((tm, tn), lambda i, j: (i, jnp.clip(j - j0, 0, n - 1)))

    return pl.pallas_call(
        _proj_body,
        grid=(t // tm, _NJ),
        in_specs=[
            pl.BlockSpec((tm, d), lambda i, j: (i, 0)),
            pl.BlockSpec((1, d), lambda i, j: (0, 0)),
            pl.BlockSpec((d, tn), lambda i, j: (0, j)),
            pl.BlockSpec((1, LANES), lambda i, j: (0, 0)),
            pl.BlockSpec((1, LANES), lambda i, j: (0, 0)),
        ],
        out_specs=[
            sect(_Q0, _K0 - _Q0), sect(_K0, _V0 - _K0), sect(_V0, _C0 - _V0),
            sect(_C0, _G0 - _C0), sect(_G0, _NJ - _G0),
        ],
        out_shape=[
            jax.ShapeDtypeStruct((t, (_K0 - _Q0) * tn), BF16),
            jax.ShapeDtypeStruct((t, (_V0 - _K0) * tn), BF16),
            jax.ShapeDtypeStruct((t, (_C0 - _V0) * tn), BF16),
            jax.ShapeDtypeStruct((t, (_G0 - _C0) * tn), F32),
            jax.ShapeDtypeStruct((t, (_NJ - _G0) * tn), BF16),
        ],
        scratch_shapes=[pltpu.VMEM((tm, d), BF16)],
        compiler_params=_cparams(("parallel", "arbitrary")),
        name="in_proj",
    )(x, g, w_cat, qg, kg)


def _mla_proj_body(c_ref, qa_g_ref, kva_g_ref, wq_ref, wkv_ref,
                   gq_nope_ref, gq_rope_ref, gq_rot_ref,
                   gk_nope_ref, gk_rope_ref, gk_rot_ref, cos_ref, sin_ref,
                   qn_ref, qr_ref, kn_ref, kr_ref, v_ref):
    c = c_ref[...]
    o_kv = MLA_Q_RANK
    o_kr = MLA_Q_RANK + MLA_KV_RANK
    cqn = _rms(c[:, :o_kv], qa_g_ref[...]).astype(BF16)
    ckvn = _rms(c[:, o_kv:o_kr], kva_g_ref[...]).astype(BF16)
    k_rope = c[:, o_kr:o_kr + LANES]
    k_rot = c[:, o_kr + LANES:o_kr + 2 * LANES]
    cos = cos_ref[...]
    sin = sin_ref[...]
    first = lax.broadcasted_iota(jnp.int32, (1, LANES), 1) < MLA_ROPE_DIM
    k_roped = k_rope * gk_rope_ref[...] * cos + k_rot * gk_rot_ref[...] * sin
    k_rope_sq = 0.5 * jnp.sum(k_rope * k_rope, axis=-1, keepdims=True)
    inv_d = 1.0 / MLA_QK_DIM

    for p in range(MLA_HEADS // 2):
        cols = slice(p * 4 * LANES, (p + 1) * 4 * LANES)
        qraw = jnp.dot(cqn, wq_ref[:, cols], preferred_element_type=F32)
        na, nb = qraw[:, :LANES], qraw[:, LANES:2 * LANES]
        rp, rt = qraw[:, 2 * LANES:3 * LANES], qraw[:, 3 * LANES:]
        rp2 = rp * rp
        sa = jnp.sum(jnp.where(first, rp2, 0.0), axis=-1, keepdims=True)
        sb = jnp.sum(jnp.where(first, 0.0, rp2), axis=-1, keepdims=True)
        ra = lax.rsqrt((jnp.sum(na * na, axis=-1, keepdims=True) + sa) * inv_d + EPS)
        rb = lax.rsqrt((jnp.sum(nb * nb, axis=-1, keepdims=True) + sb) * inv_d + EPS)
        qn_ref[:, (2 * p) * LANES:(2 * p + 1) * LANES] = (na * ra * gq_nope_ref[...]).astype(BF16)
        qn_ref[:, (2 * p + 1) * LANES:(2 * p + 2) * LANES] = (nb * rb * gq_nope_ref[...]).astype(BF16)
        roped = rp * gq_rope_ref[...] * cos + rt * gq_rot_ref[...] * sin
        qr_ref[:, p * LANES:(p + 1) * LANES] = (roped * jnp.where(first, ra, rb)).astype(BF16)

        kvraw = jnp.dot(ckvn, wkv_ref[:, cols], preferred_element_type=F32)
        kna, va = kvraw[:, :LANES], kvraw[:, LANES:2 * LANES]
        knb, vb = kvraw[:, 2 * LANES:3 * LANES], kvraw[:, 3 * LANES:]
        rka = lax.rsqrt((jnp.sum(kna * kna, axis=-1, keepdims=True) + k_rope_sq) * inv_d + EPS)
        rkb = lax.rsqrt((jnp.sum(knb * knb, axis=-1, keepdims=True) + k_rope_sq) * inv_d + EPS)
        ca = slice((2 * p) * LANES, (2 * p + 1) * LANES)
        cb = slice((2 * p + 1) * LANES, (2 * p + 2) * LANES)
        kn_ref[:, ca] = (kna * rka * gk_nope_ref[...]).astype(BF16)
        kn_ref[:, cb] = (knb * rkb * gk_nope_ref[...]).astype(BF16)
        kr_ref[:, ca] = jnp.where(first, k_roped * rka, 0.0).astype(BF16)
        kr_ref[:, cb] = jnp.where(first, 0.0, k_roped * rkb).astype(BF16)
        v_ref[:, ca] = va.astype(BF16)
        v_ref[:, cb] = vb.astype(BF16)


def _mla_proj(c, qa_g, kva_g, wq, wkv, gains, cos2, sin2, *, tm, seq):
    t = c.shape[0]
    n_pos = seq // tm
    const = lambda shape: pl.BlockSpec(shape, lambda i: (0, 0))
    hw = MLA_HEADS * LANES
    tok = lambda w: pl.BlockSpec((tm, w), lambda i: (i, 0))
    return pl.pallas_call(
        _mla_proj_body,
        grid=(t // tm,),
        in_specs=[
            tok(c.shape[1]), const(qa_g.shape), const(kva_g.shape),
            const(wq.shape), const(wkv.shape),
        ] + [const((1, LANES))] * 6 + [
            pl.BlockSpec((tm, LANES), lambda i: (i % n_pos, 0)),
            pl.BlockSpec((tm, LANES), lambda i: (i % n_pos, 0)),
        ],
        out_specs=[tok(hw), tok(hw // 2), tok(hw), tok(hw), tok(hw)],
        out_shape=[
            jax.ShapeDtypeStruct((t, hw), BF16),
            jax.ShapeDtypeStruct((t, hw // 2), BF16),
            jax.ShapeDtypeStruct((t, hw), BF16),
            jax.ShapeDtypeStruct((t, hw), BF16),
            jax.ShapeDtypeStruct((t, hw), BF16),
        ],
        compiler_params=_cparams(("parallel",)),
        name="mla_proj",
    )(c, qa_g, kva_g, wq, wkv, *gains, cos2, sin2)


CHUNK_SHIFT = CHUNK.bit_length() - 1


def _chunk_mask(q0, k0, tq, tk):
    row = lax.broadcasted_iota(jnp.int32, (tq, 1), 0)
    col = lax.broadcasted_iota(jnp.int32, (1, tk), 1)
    return ((k0 + col) >> CHUNK_SHIFT) <= ((q0 + row) >> CHUNK_SHIFT)


MXU_WIDTH = 256


def _score_pass(q, k_tile, bias, keep, s_ref, tk):
    mx = None
    for n in range(tk // MXU_WIDTH):
        s = _nt_dot(q, k_tile(n))
        if bias is not None:
            s = s + bias(n)
        if keep is not None:
            s = jnp.where(keep(n), s, MASK_VALUE)
        s_ref[:, n * MXU_WIDTH:(n + 1) * MXU_WIDTH] = s
        t = jnp.maximum(s[:, :LANES], s[:, LANES:])
        mx = t if mx is None else jnp.maximum(mx, t)
    return jnp.max(mx, axis=-1, keepdims=True)


def _prob_pass(s_ref, p_ref, m, l, m_blk, tk):
    m_new = jnp.maximum(m, m_blk)
    alpha = jnp.exp2(m - m_new)
    ps = None
    for n in range(tk // MXU_WIDTH):
        cols = slice(n * MXU_WIDTH, (n + 1) * MXU_WIDTH)
        p = jnp.exp2(s_ref[:, cols] - m_new)
        t = p[:, :LANES] + p[:, LANES:]
        ps = t if ps is None else ps + t
        p_ref[:, cols] = p.astype(BF16)
    l_new = alpha * l + jnp.sum(ps, axis=-1, keepdims=True)
    return m_new, l_new, alpha


def _pipelined_key_blocks(n_full, score, prob_pv, finalize, tq):
    neg = jnp.full((tq, 1), MASK_VALUE, F32)
    zero = jnp.zeros((tq, 1), F32)
    init = (neg, zero, neg, zero)

    @pl.when(n_full == 0)
    def _():
        finalize(prob_pv(0, init, score(0, True)))

    @pl.when(n_full > 0)
    def _():
        def body(j, carry):
            state, mx = carry
            state = prob_pv(j, state, mx)
            return state, score(j + 1, False)

        state, mx = lax.fori_loop(0, n_full - 1, body, (init, score(0, False)))
        state = prob_pv(n_full - 1, state, mx)
        mx_last = score(n_full, True)
        finalize(prob_pv(n_full, state, mx_last))


def _da_attn_body(slopes_ref, lq1_ref, lk1_ref, lq2_ref, lk2_ref, sg_ref,
                  q_ref, k_ref, v_ref, o_ref, acc1_ref, acc2_ref, s_ref, p_ref,
                  *, tq, tk, lambda_init):
    h = pl.program_id(1)
    qi = pl.program_id(2)
    slope = slopes_ref[h]
    q = q_ref[0]
    q1, q2 = q[:, :DA_HEAD_DIM], q[:, DA_HEAD_DIM:]
    q0 = qi * tq
    n_full = qi // (tk // tq)
    row = lax.broadcasted_iota(jnp.int32, (tq, 1), 0)
    col = lax.broadcasted_iota(jnp.int32, (1, MXU_WIDTH), 1)
    acc1_ref[...] = jnp.zeros_like(acc1_ref)
    acc2_ref[...] = jnp.zeros_like(acc2_ref)

    def score(j, last):
        k0 = pl.multiple_of(j * tk, tk)

        def k_tile(lo):
            return lambda n: k_ref[0, pl.ds(pl.multiple_of(k0 + n * MXU_WIDTH, MXU_WIDTH), MXU_WIDTH),
                                   lo:lo + DA_HEAD_DIM]

        if last:
            def bias(n):
                dist = jnp.abs((q0 - k0 - n * MXU_WIDTH) + row - col)
                return slope * (row - dist).astype(F32)
            keep = lambda n: _chunk_mask(q0, k0 + n * MXU_WIDTH, tq, MXU_WIDTH)
        else:
            bias = lambda n: slope * ((k0 + n * MXU_WIDTH - q0) + col).astype(F32)
            keep = None
        return (_score_pass(q1, k_tile(0), bias, keep, s_ref.at[0], tk),
                _score_pass(q2, k_tile(DA_HEAD_DIM), bias, keep, s_ref.at[1], tk))

    def prob_pv(j, state, mx):
        m1, l1, m2, l2 = state
        m1, l1, a1 = _prob_pass(s_ref.at[0], p_ref.at[0], m1, l1, mx[0], tk)
        m2, l2, a2 = _prob_pass(s_ref.at[1], p_ref.at[1], m2, l2, mx[1], tk)
        vb = v_ref[0, pl.ds(pl.multiple_of(j * tk, tk), tk), :]
        acc1_ref[...] = a1 * acc1_ref[...] + jnp.dot(p_ref[0], vb, preferred_element_type=F32)
        acc2_ref[...] = a2 * acc2_ref[...] + jnp.dot(p_ref[1], vb, preferred_element_type=F32)
        return m1, l1, m2, l2

    def finalize(state):
        _, l1, _, l2 = state
        lam = (jnp.exp(jnp.sum(lq1_ref[...] * lk1_ref[...], axis=-1, keepdims=True))
               - jnp.exp(jnp.sum(lq2_ref[...] * lk2_ref[...], axis=-1, keepdims=True))
               + lambda_init)
        o = acc1_ref[...] / l1 - lam * (acc2_ref[...] / l2)
        o_ref[0] = (_rms(o, sg_ref[...]) * (1.0 - lambda_init)).astype(BF16)

    _pipelined_key_blocks(n_full, score, prob_pv, finalize, tq)


def _da_attn(slopes, lams, sg, q, k, v, *, tq, tk, lambda_init):
    b, s, _ = q.shape
    w = DA_V_DIM
    vec = pl.BlockSpec((1, DA_HEAD_DIM), lambda bi, h, i: (0, 0))
    return pl.pallas_call(
        functools.partial(_da_attn_body, tq=tq, tk=tk, lambda_init=lambda_init),
        grid=(b, DA_HEADS, s // tq),
        in_specs=[
            pl.BlockSpec(memory_space=pltpu.SMEM),
            vec, vec, vec, vec,
            pl.BlockSpec((1, w), lambda bi, h, i: (0, 0)),
            pl.BlockSpec((1, tq, w), lambda bi, h, i: (bi, i, h)),
            pl.BlockSpec((1, s, w), lambda bi, h, i: (bi, 0, h)),
            pl.BlockSpec((1, s, w), lambda bi, h, i: (bi, 0, h)),
        ],
        out_specs=pl.BlockSpec((1, tq, w), lambda bi, h, i: (bi, i, h)),
        out_shape=jax.ShapeDtypeStruct((b, s, DA_HEADS * w), BF16),
        scratch_shapes=[pltpu.VMEM((tq, w), F32), pltpu.VMEM((tq, w), F32),
                        pltpu.VMEM((2, tq, tk), F32), pltpu.VMEM((2, tq, tk), BF16)],
        compiler_params=_cparams(("parallel", "parallel", "arbitrary")),
        name="da_attn",
    )(slopes, *lams, sg, q, k, v)


def _mla_attn_body(qn_ref, qr_ref, kn_ref, kr_ref, v_ref, o_ref, acc_a_ref, acc_b_ref,
                   s_ref, p_ref, *, tq, tk):
    qi = pl.program_id(2)
    q0 = qi * tq
    n_full = qi // (tk // tq)
    qn = qn_ref[0]
    qr = qr_ref[0]
    qa = jnp.concatenate([qn[:, :LANES], qr], axis=-1)
    qb = jnp.concatenate([qn[:, LANES:], qr], axis=-1)
    acc_a_ref[...] = jnp.zeros_like(acc_a_ref)
    acc_b_ref[...] = jnp.zeros_like(acc_b_ref)

    def score(j, last):
        k0 = pl.multiple_of(j * tk, tk)

        def k_tile(lo):
            def tile(n):
                rows = pl.ds(pl.multiple_of(k0 + n * MXU_WIDTH, MXU_WIDTH), MXU_WIDTH)
                return jnp.concatenate([kn_ref[0, rows, lo:lo + LANES],
                                        kr_ref[0, rows, lo:lo + LANES]], axis=-1)
            return tile

        keep = (lambda n: _chunk_mask(q0, k0 + n * MXU_WIDTH, tq, MXU_WIDTH)) if last else None
        return (_score_pass(qa, k_tile(0), None, keep, s_ref.at[0], tk),
                _score_pass(qb, k_tile(LANES), None, keep, s_ref.at[1], tk))

    def prob_pv(j, state, mx):
        ma, la, mb, lb = state
        ma, la, aa = _prob_pass(s_ref.at[0], p_ref.at[0], ma, la, mx[0], tk)
        mb, lb, ab = _prob_pass(s_ref.at[1], p_ref.at[1], mb, lb, mx[1], tk)
        vb = v_ref[0, pl.ds(pl.multiple_of(j * tk, tk), tk), :]
        acc_a_ref[...] = aa * acc_a_ref[...] + jnp.dot(p_ref[0], vb[:, :LANES], preferred_element_type=F32)
        acc_b_ref[...] = ab * acc_b_ref[...] + jnp.dot(p_ref[1], vb[:, LANES:], preferred_element_type=F32)
        return ma, la, mb, lb

    def finalize(state):
        _, la, _, lb = state
        o_ref[0] = jnp.concatenate([acc_a_ref[...] / la, acc_b_ref[...] / lb], axis=-1).astype(BF16)

    _pipelined_key_blocks(n_full, score, prob_pv, finalize, tq)


def _mla_attn(qn, qr, kn, kr, v, *, tq, tk):
    b, s, _ = qn.shape
    w = 2 * LANES
    q_spec = pl.BlockSpec((1, tq, w), lambda bi, h, i: (bi, i, h))
    kv_spec = pl.BlockSpec((1, s, w), lambda bi, h, i: (bi, 0, h))
    return pl.pallas_call(
        functools.partial(_mla_attn_body, tq=tq, tk=tk),
        grid=(b, MLA_HEADS // 2, s // tq),
        in_specs=[
            q_spec,
            pl.BlockSpec((1, tq, LANES), lambda bi, h, i: (bi, i, h)),
            kv_spec, kv_spec, kv_spec,
        ],
        out_specs=q_spec,
        out_shape=jax.ShapeDtypeStruct((b, s, MLA_HEADS * LANES), BF16),
        scratch_shapes=[pltpu.VMEM((tq, LANES), F32), pltpu.VMEM((tq, LANES), F32),
                        pltpu.VMEM((2, tq, tk), F32), pltpu.VMEM((2, tq, tk), BF16)],
        compiler_params=_cparams(("parallel", "parallel", "arbitrary")),
        name="mla_attn",
    )(qn, qr, kn, kr, v)


def _merge_body(x_ref, ya_ref, yb_ref, gate_a_ref, gate_b_ref, wa_ref, wb_ref, wo_ref, o_ref):
    @pl.when(pl.program_id(1) == 0)
    def _():
        o_ref[...] = x_ref[...]

    a = jnp.dot(ya_ref[...], wa_ref[...], preferred_element_type=F32)
    b = jnp.dot(yb_ref[...], wb_ref[...], preferred_element_type=F32)
    m = (gate_a_ref[...].astype(F32) * a + gate_b_ref[...].astype(F32) * b).astype(BF16)
    o_ref[...] += jnp.dot(m, wo_ref[...], preferred_element_type=F32)


def _merge(x, ya, yb, gates, wa, wb, wo, *, tm, tn):
    t, d = x.shape
    nj = d // tn
    tok = pl.BlockSpec((tm, d), lambda i, j: (i, 0))
    return pl.pallas_call(
        _merge_body,
        grid=(t // tm, nj),
        in_specs=[
            tok, tok, tok,
            pl.BlockSpec((tm, tn), lambda i, j: (i, j)),
            pl.BlockSpec((tm, tn), lambda i, j: (i, j + nj)),
            pl.BlockSpec((d, tn), lambda i, j: (0, j)),
            pl.BlockSpec((d, tn), lambda i, j: (0, j)),
            pl.BlockSpec((tn, d), lambda i, j: (j, 0)),
        ],
        out_specs=tok,
        out_shape=jax.ShapeDtypeStruct((t, d), F32),
        compiler_params=_cparams(("parallel", "arbitrary")),
        name="merge",
    )(x, ya, yb, gates, gates, wa, wb, wo)


def _rot_cols(w):
    half = w.shape[-1] // 2
    return jnp.concatenate([-w[..., half:], w[..., :half]], axis=-1)


def _swap_halves(g):
    half = g.shape[-1] // 2
    return jnp.concatenate([g[..., half:], g[..., :half]], axis=-1)


def _pick_tile(n, pref):
    return pref if n % pref == 0 else n


def kernel(x, ffn1_norm_g, ffn1_w1, ffn1_w3, ffn1_w2, mix_norm_g, w_in, da_q_norm_g, da_k_norm_g, da_lambda_q1, da_lambda_k1, da_lambda_q2, da_lambda_k2, da_subln_g, mla_q_a_norm_g, mla_w_qb, mla_kv_a_norm_g, mla_w_kvb, mla_q_norm_g, mla_k_norm_g, w_branch_a, w_branch_b, w_out, ffn2_norm_g, ffn2_w1, ffn2_w3, ffn2_w2):
    bsz, seq, d = x.shape
    t = bsz * seq
    depth = ffn1_norm_g.shape[0]
    tm = _pick_tile(t, 512)
    tq = _pick_tile(seq, 256)
    tk = _pick_tile(seq, 1024)
    row = lambda v: v.reshape(1, -1).astype(F32)

    inv = ROPE_THETA ** (-jnp.arange(0, MLA_ROPE_DIM, 2, dtype=F32) / MLA_ROPE_DIM)
    ang = jnp.arange(seq, dtype=F32)[:, None] * inv[None, :]
    cos2 = jnp.tile(jnp.cos(ang), (1, 4))
    sin2 = jnp.tile(jnp.sin(ang), (1, 4))
    slopes = (2.0 ** (-8.0 * jnp.arange(1, DA_HEADS + 1, dtype=F32) / DA_HEADS)) * LOG2E

    xt = x.reshape(t, d)
    for l in range(depth):
        lambda_init = 0.8 - 0.6 * math.exp(-0.3 * l)

        xt = _ffn(xt, row(ffn1_norm_g[l]), ffn1_w1[l].astype(BF16), ffn1_w3[l].astype(BF16),
                  (0.5 * ffn1_w2[l]).astype(BF16), tm=tm, tf=512)

        w = w_in[l]
        o = [0]
        for n in (2 * DA_HEADS * DA_HEAD_DIM, 2 * DA_HEADS * DA_HEAD_DIM, DA_HEADS * DA_V_DIM,
                  MLA_Q_RANK, MLA_KV_RANK, MLA_ROPE_DIM, d, d):
            o.append(o[-1] + n)
        w_kr = w[:, o[5]:o[6]]
        w_kr_rot = _rot_cols(w_kr)
        w_cat = jnp.concatenate(
            [w[:, o[0]:o[5]], w_kr, w_kr, w_kr_rot, w_kr_rot, w[:, o[6]:]], axis=1).astype(BF16)
        q_scale = DA_HEAD_DIM ** -0.5 * LOG2E
        q_da, k_da, v_da, lat, gates = _in_proj(
            xt, row(mix_norm_g[l]), w_cat, row(da_q_norm_g[l]) * q_scale, row(da_k_norm_g[l]), tm=tm)

        wq = mla_w_qb[l].reshape(MLA_Q_RANK, MLA_HEADS, MLA_QK_DIM)
        wq_nope = wq[:, :, :MLA_NOPE_DIM].reshape(MLA_Q_RANK, MLA_HEADS // 2, 2 * LANES)
        wq_rope = wq[:, :, MLA_NOPE_DIM:]
        wq_rot = _rot_cols(wq_rope).reshape(MLA_Q_RANK, MLA_HEADS // 2, LANES)
        wq_rope = wq_rope.reshape(MLA_Q_RANK, MLA_HEADS // 2, LANES)
        wq_packed = jnp.concatenate([wq_nope, wq_rope, wq_rot], axis=-1).reshape(
            MLA_Q_RANK, MLA_HEADS * 2 * LANES).astype(BF16)
        mla_scale = MLA_QK_DIM ** -0.5 * LOG2E
        gq, gk = mla_q_norm_g[l], mla_k_norm_g[l]
        gains = (
            row(gq[:MLA_NOPE_DIM]) * mla_scale,
            row(jnp.tile(gq[MLA_NOPE_DIM:], 2)) * mla_scale,
            row(jnp.tile(_swap_halves(gq[MLA_NOPE_DIM:]), 2)) * mla_scale,
            row(gk[:MLA_NOPE_DIM]),
            row(jnp.tile(gk[MLA_NOPE_DIM:], 2)),
            row(jnp.tile(_swap_halves(gk[MLA_NOPE_DIM:]), 2)),
        )
        qn, qr, kn, kr, v_mla = _mla_proj(
            lat, row(mla_q_a_norm_g[l]), row(mla_kv_a_norm_g[l]), wq_packed,
            mla_w_kvb[l].astype(BF16), gains, cos2, sin2, tm=_pick_tile(seq, 256), seq=seq)

        b3 = lambda a: a.reshape(bsz, seq, a.shape[-1])
        lams = (row(da_lambda_q1[l]), row(da_lambda_k1[l]), row(da_lambda_q2[l]), row(da_lambda_k2[l]))
        y_a = _da_attn(slopes, lams, row(da_subln_g[l]), b3(q_da), b3(k_da), b3(v_da),
                       tq=tq, tk=tk, lambda_init=lambda_init)
        y_b = _mla_attn(b3(qn), b3(qr), b3(kn), b3(kr), b3(v_mla), tq=tq, tk=tk)

        xt = _merge(xt, y_a.reshape(t, -1), y_b.reshape(t, -1), gates,
                    w_branch_a[l].astype(BF16), w_branch_b[l].astype(BF16), w_out[l].astype(BF16),
                    tm=tm, tn=512)

        xt = _ffn(xt, row(ffn2_norm_g[l]), ffn2_w1[l].astype(BF16), ffn2_w3[l].astype(BF16),
                  (0.5 * ffn2_w2[l]).astype(BF16), tm=tm, tf=512)
    return xt.reshape(bsz, seq, d)
```

```python
import functools
import math

import jax
import jax.numpy as jnp
from jax import lax
from jax.experimental import pallas as pl
from jax.experimental.pallas import tpu as pltpu

F32 = jnp.float32
BF16 = jnp.bfloat16

EPS = 1e-6
CHUNK = 64
DA_HEADS = 8
DA_HEAD_DIM = 128
DA_V_DIM = 2 * DA_HEAD_DIM
MLA_HEADS = 16
MLA_Q_RANK = 768
MLA_KV_RANK = 512
MLA_NOPE_DIM = 128
MLA_ROPE_DIM = 64
MLA_V_DIM = 128
MLA_QK_DIM = MLA_NOPE_DIM + MLA_ROPE_DIM
ROPE_THETA = 10000.0
LOG2E = math.log2(math.e)
MASK_VALUE = -1e30

LANES = 128
V7X_VMEM_BYTES = 64 * 1024 * 1024
VMEM_LIMIT = V7X_VMEM_BYTES - 8 * 1024 * 1024


def _cparams(sem):
    return pltpu.CompilerParams(dimension_semantics=sem, vmem_limit_bytes=VMEM_LIMIT)


def _rms(x, g):
    ms = jnp.mean(x * x, axis=-1, keepdims=True)
    return x * lax.rsqrt(ms + EPS) * g


def _nt_dot(a, b):
    return lax.dot_general(a, b, (((1,), (1,)), ((), ())), preferred_element_type=F32)


def _ffn_body(x_ref, g_ref, w1_ref, w3_ref, w2_ref, o_ref, h_ref):
    @pl.when(pl.program_id(1) == 0)
    def _():
        x = x_ref[...]
        h_ref[...] = _rms(x, g_ref[...]).astype(BF16)
        o_ref[...] = x

    h = h_ref[...]
    a = jnp.dot(h, w1_ref[...], preferred_element_type=F32)
    b = jnp.dot(h, w3_ref[...], preferred_element_type=F32)
    u = (a * jax.nn.sigmoid(a) * b).astype(BF16)
    d = o_ref.shape[1]
    for c in range(0, d, FFN_OUT_CHUNK):
        cols = slice(c, c + FFN_OUT_CHUNK)
        o_ref[:, cols] += jnp.dot(u, w2_ref[:, cols], preferred_element_type=F32)


FFN_OUT_CHUNK = 512


def _ffn(x, g, w1, w3, w2_half, *, tm, tf):
    t, d = x.shape
    f = w1.shape[1]
    return pl.pallas_call(
        _ffn_body,
        grid=(t // tm, f // tf),
        in_specs=[
            pl.BlockSpec((tm, d), lambda i, j: (i, 0), pipeline_mode=pl.Buffered(1)),
            pl.BlockSpec((1, d), lambda i, j: (0, 0)),
            pl.BlockSpec((d, tf), lambda i, j: (0, j)),
            pl.BlockSpec((d, tf), lambda i, j: (0, j)),
            pl.BlockSpec((tf, d), lambda i, j: (j, 0)),
        ],
        out_specs=pl.BlockSpec((tm, d), lambda i, j: (i, 0)),
        out_shape=jax.ShapeDtypeStruct((t, d), F32),
        scratch_shapes=[pltpu.VMEM((tm, d), BF16)],
        compiler_params=_cparams(("parallel", "arbitrary")),
        name="ffn",
    )(x, g, w1, w3, w2_half)


PROJ_TN = 512
_Q0, _K0, _V0, _C0, _G0, _NJ = 0, 4, 8, 12, 15, 23


def _head_norm(acc, g):
    parts = []
    for c in range(acc.shape[1] // LANES):
        blk = acc[:, c * LANES:(c + 1) * LANES]
        ms = jnp.mean(blk * blk, axis=-1, keepdims=True)
        parts.append(blk * lax.rsqrt(ms + EPS) * g)
    return jnp.concatenate(parts, axis=-1)


def _proj_body(x_ref, g_ref, w_ref, qg_ref, kg_ref,
               q_ref, k_ref, v_ref, c_ref, gate_ref, h_ref):
    j = pl.program_id(1)

    @pl.when(j == 0)
    def _():
        h_ref[...] = _rms(x_ref[...], g_ref[...]).astype(BF16)

    acc = jnp.dot(h_ref[...], w_ref[...], preferred_element_type=F32)

    @pl.when(j < _K0)
    def _():
        q_ref[...] = _head_norm(acc, qg_ref[...]).astype(BF16)

    @pl.when((j >= _K0) & (j < _V0))
    def _():
        k_ref[...] = _head_norm(acc, kg_ref[...]).astype(BF16)

    @pl.when((j >= _V0) & (j < _C0))
    def _():
        v_ref[...] = acc.astype(BF16)

    @pl.when((j >= _C0) & (j < _G0))
    def _():
        c_ref[...] = acc

    @pl.when(j >= _G0)
    def _():
        gate_ref[...] = jax.nn.sigmoid(acc).astype(BF16)


def _in_proj(x, g, w_cat, qg, kg, *, tm):
    t, d = x.shape
    tn = PROJ_TN

    def sect(j0, n):
        return pl.BlockSpec((tm, tn), lambda i, j: (i, jnp.clip(j - j0, 0, n - 1)))

    return pl.pallas_call(
        _proj_body,
        grid=(t // tm, _NJ),
        in_specs=[
            pl.BlockSpec((tm, d), lambda i, j: (i, 0)),
            pl.BlockSpec((1, d), lambda i, j: (0, 0)),
            pl.BlockSpec((d, tn), lambda i, j: (0, j)),
            pl.BlockSpec((1, LANES), lambda i, j: (0, 0)),
            pl.BlockSpec((1, LANES), lambda i, j: (0, 0)),
        ],
        out_specs=[
            sect(_Q0, _K0 - _Q0), sect(_K0, _V0 - _K0), sect(_V0, _C0 - _V0),
            sect(_C0, _G0 - _C0), sect(_G0, _NJ - _G0),
        ],
        out_shape=[
            jax.ShapeDtypeStruct((t, (_K0 - _Q0) * tn), BF16),
            jax.ShapeDtypeStruct((t, (_V0 - _K0) * tn), BF16),
            jax.ShapeDtypeStruct((t, (_C0 - _V0) * tn), BF16),
            jax.ShapeDtypeStruct((t, (_G0 - _C0) * tn), F32),
            jax.ShapeDtypeStruct((t, (_NJ - _G0) * tn), BF16),
        ],
        scratch_shapes=[pltpu.VMEM((tm, d), BF16)],
        compiler_params=_cparams(("parallel", "arbitrary")),
        name="in_proj",
    )(x, g, w_cat, qg, kg)


def _mla_proj_body(c_ref, qa_g_ref, kva_g_ref, wq_ref, wkv_ref,
                   gq_nope_ref, gq_rope_ref, gq_rot_ref,
                   gk_nope_ref, gk_rope_ref, gk_rot_ref, cos_ref, sin_ref,
                   qn_ref, qr_ref, kn_ref, kr_ref, v_ref):
    c = c_ref[...]
    o_kv = MLA_Q_RANK
    o_kr = MLA_Q_RANK + MLA_KV_RANK
    cqn = _rms(c[:, :o_kv], qa_g_ref[...]).astype(BF16)
    ckvn = _rms(c[:, o_kv:o_kr], kva_g_ref[...]).astype(BF16)
    k_rope = c[:, o_kr:o_kr + LANES]
    k_rot = c[:, o_kr + LANES:o_kr + 2 * LANES]
    cos = cos_ref[...]
    sin = sin_ref[...]
    first = lax.broadcasted_iota(jnp.int32, (1, LANES), 1) < MLA_ROPE_DIM
    k_roped = k_rope * gk_rope_ref[...] * cos + k_rot * gk_rot_ref[...] * sin
    k_rope_sq = 0.5 * jnp.sum(k_rope * k_rope, axis=-1, keepdims=True)
    inv_d = 1.0 / MLA_QK_DIM

    for p in range(MLA_HEADS // 2):
        cols = slice(p * 4 * LANES, (p + 1) * 4 * LANES)
        qraw = jnp.dot(cqn, wq_ref[:, cols], preferred_element_type=F32)
        na, nb = qraw[:, :LANES], qraw[:, LANES:2 * LANES]
        rp, rt = qraw[:, 2 * LANES:3 * LANES], qraw[:, 3 * LANES:]
        rp2 = rp * rp
        sa = jnp.sum(jnp.where(first, rp2, 0.0), axis=-1, keepdims=True)
        sb = jnp.sum(jnp.where(first, 0.0, rp2), axis=-1, keepdims=True)
        ra = lax.rsqrt((jnp.sum(na * na, axis=-1, keepdims=True) + sa) * inv_d + EPS)
        rb = lax.rsqrt((jnp.sum(nb * nb, axis=-1, keepdims=True) + sb) * inv_d + EPS)
        qn_ref[:, (2 * p) * LANES:(2 * p + 1) * LANES] = (na * ra * gq_nope_ref[...]).astype(BF16)
        qn_ref[:, (2 * p + 1) * LANES:(2 * p + 2) * LANES] = (nb * rb * gq_nope_ref[...]).astype(BF16)
        roped = rp * gq_rope_ref[...] * cos + rt * gq_rot_ref[...] * sin
        qr_ref[:, p * LANES:(p + 1) * LANES] = (roped * jnp.where(first, ra, rb)).astype(BF16)

        kvraw = jnp.dot(ckvn, wkv_ref[:, cols], preferred_element_type=F32)
        kna, va = kvraw[:, :LANES], kvraw[:, LANES:2 * LANES]
        knb, vb = kvraw[:, 2 * LANES:3 * LANES], kvraw[:, 3 * LANES:]
        rka = lax.rsqrt((jnp.sum(kna * kna, axis=-1, keepdims=True) + k_rope_sq) * inv_d + EPS)
        rkb = lax.rsqrt((jnp.sum(knb * knb, axis=-1, keepdims=True) + k_rope_sq) * inv_d + EPS)
        ca = slice((2 * p) * LANES, (2 * p + 1) * LANES)
        cb = slice((2 * p + 1) * LANES, (2 * p + 2) * LANES)
        kn_ref[:, ca] = (kna * rka * gk_nope_ref[...]).astype(BF16)
        kn_ref[:, cb] = (knb * rkb * gk_nope_ref[...]).astype(BF16)
        kr_ref[:, ca] = jnp.where(first, k_roped * rka, 0.0).astype(BF16)
        kr_ref[:, cb] = jnp.where(first, 0.0, k_roped * rkb).astype(BF16)
        v_ref[:, ca] = va.astype(BF16)
        v_ref[:, cb] = vb.astype(BF16)


def _mla_proj(c, qa_g, kva_g, wq, wkv, gains, cos2, sin2, *, tm, seq):
    t = c.shape[0]
    n_pos = seq // tm
    const = lambda shape: pl.BlockSpec(shape, lambda i: (0, 0))
    hw = MLA_HEADS * LANES
    tok = lambda w: pl.BlockSpec((tm, w), lambda i: (i, 0))
    return pl.pallas_call(
        _mla_proj_body,
        grid=(t // tm,),
        in_specs=[
            tok(c.shape[1]), const(qa_g.shape), const(kva_g.shape),
            const(wq.shape), const(wkv.shape),
        ] + [const((1, LANES))] * 6 + [
            pl.BlockSpec((tm, LANES), lambda i: (i % n_pos, 0)),
            pl.BlockSpec((tm, LANES), lambda i: (i % n_pos, 0)),
        ],
        out_specs=[tok(hw), tok(hw // 2), tok(hw), tok(hw), tok(hw)],
        out_shape=[
            jax.ShapeDtypeStruct((t, hw), BF16),
            jax.ShapeDtypeStruct((t, hw // 2), BF16),
            jax.ShapeDtypeStruct((t, hw), BF16),
            jax.ShapeDtypeStruct((t, hw), BF16),
            jax.ShapeDtypeStruct((t, hw), BF16),
        ],
        compiler_params=_cparams(("parallel",)),
        name="mla_proj",
    )(c, qa_g, kva_g, wq, wkv, *gains, cos2, sin2)


CHUNK_SHIFT = CHUNK.bit_length() - 1


def _chunk_mask(q0, k0, tq, tk):
    row = lax.broadcasted_iota(jnp.int32, (tq, 1), 0)
    col = lax.broadcasted_iota(jnp.int32, (1, tk), 1)
    return ((k0 + col) >> CHUNK_SHIFT) <= ((q0 + row) >> CHUNK_SHIFT)


MXU_WIDTH = 256


def _score_pass(q, k_tile, bias, keep, s_ref, tk):
    mx = None
    for n in range(tk // MXU_WIDTH):
        s = _nt_dot(q, k_tile(n))
        if bias is not None:
            s = s + bias(n)
        if keep is not None:
            s = jnp.where(keep(n), s, MASK_VALUE)
        s_ref[:, n * MXU_WIDTH:(n + 1) * MXU_WIDTH] = s
        t = jnp.maximum(s[:, :LANES], s[:, LANES:])
        mx = t if mx is None else jnp.maximum(mx, t)
    return jnp.max(mx, axis=-1, keepdims=True)


def _prob_pass(s_ref, p_ref, m, l, m_blk, tk):
    m_new = jnp.maximum(m, m_blk)
    alpha = jnp.exp2(m - m_new)
    ps = None
    for n in range(tk // MXU_WIDTH):
        cols = slice(n * MXU_WIDTH, (n + 1) * MXU_WIDTH)
        p = jnp.exp2(s_ref[:, cols] - m_new)
        t = p[:, :LANES] + p[:, LANES:]
        ps = t if ps is None else ps + t
        p_ref[:, cols] = p.astype(BF16)
    l_new = alpha * l + jnp.sum(ps, axis=-1, keepdims=True)
    return m_new, l_new, alpha


def _pipelined_key_blocks(n_full, score, prob_pv, finalize, tq):
    neg = jnp.full((tq, 1), MASK_VALUE, F32)
    zero = jnp.zeros((tq, 1), F32)
    init = (neg, zero, neg, zero)

    @pl.when(n_full == 0)
    def _():
        finalize(prob_pv(0, init, score(0, True)))

    @pl.when(n_full > 0)
    def _():
        def body(j, carry):
            state, mx = carry
            state = prob_pv(j, state, mx)
            return state, score(j + 1, False)

        state, mx = lax.fori_loop(0, n_full - 1, body, (init, score(0, False)))
        state = prob_pv(n_full - 1, state, mx)
        mx_last = score(n_full, True)
        finalize(prob_pv(n_full, state, mx_last))


def _da_attn_body(slopes_ref, lq1_ref, lk1_ref, lq2_ref, lk2_ref, sg_ref,
                  q_ref, k_ref, v_ref, o_ref, acc1_ref, acc2_ref, s_ref, p_ref,
                  *, tq, tk, lambda_init):
    h = pl.program_id(1)
    qi = pl.program_id(2)
    slope = slopes_ref[h]
    q = q_ref[0]
    q1, q2 = q[:, :DA_HEAD_DIM], q[:, DA_HEAD_DIM:]
    q0 = qi * tq
    n_full = qi // (tk // tq)
    row = lax.broadcasted_iota(jnp.int32, (tq, 1), 0)
    col = lax.broadcasted_iota(jnp.int32, (1, MXU_WIDTH), 1)
    acc1_ref[...] = jnp.zeros_like(acc1_ref)
    acc2_ref[...] = jnp.zeros_like(acc2_ref)

    def score(j, last):
        k0 = pl.multiple_of(j * tk, tk)

        def k_tile(lo):
            return lambda n: k_ref[0, pl.ds(pl.multiple_of(k0 + n * MXU_WIDTH, MXU_WIDTH), MXU_WIDTH),
                                   lo:lo + DA_HEAD_DIM]

        if last:
            def bias(n):
                dist = jnp.abs((q0 - k0 - n * MXU_WIDTH) + row - col)
                return slope * (row - dist).astype(F32)
            keep = lambda n: _chunk_mask(q0, k0 + n * MXU_WIDTH, tq, MXU_WIDTH)
        else:
            bias = lambda n: slope * ((k0 + n * MXU_WIDTH - q0) + col).astype(F32)
            keep = None
        return (_score_pass(q1, k_tile(0), bias, keep, s_ref.at[0], tk),
                _score_pass(q2, k_tile(DA_HEAD_DIM), bias, keep, s_ref.at[1], tk))

    def prob_pv(j, state, mx):
        m1, l1, m2, l2 = state
        m1, l1, a1 = _prob_pass(s_ref.at[0], p_ref.at[0], m1, l1, mx[0], tk)
        m2, l2, a2 = _prob_pass(s_ref.at[1], p_ref.at[1], m2, l2, mx[1], tk)
        vb = v_ref[0, pl.ds(pl.multiple_of(j * tk, tk), tk), :]
        acc1_ref[...] = a1 * acc1_ref[...] + jnp.dot(p_ref[0], vb, preferred_element_type=F32)
        acc2_ref[...] = a2 * acc2_ref[...] + jnp.dot(p_ref[1], vb, preferred_element_type=F32)
        return m1, l1, m2, l2

    def finalize(state):
        _, l1, _, l2 = state
        lam = (jnp.exp(jnp.sum(lq1_ref[...] * lk1_ref[...], axis=-1, keepdims=True))
               - jnp.exp(jnp.sum(lq2_ref[...] * lk2_ref[...], axis=-1, keepdims=True))
               + lambda_init)
        o = acc1_ref[...] / l1 - lam * (acc2_ref[...] / l2)
        o_ref[0] = (_rms(o, sg_ref[...]) * (1.0 - lambda_init)).astype(BF16)

    _pipelined_key_blocks(n_full, score, prob_pv, finalize, tq)


def _da_attn(slopes, lams, sg, q, k, v, *, tq, tk, lambda_init):
    b, s, _ = q.shape
    w = DA_V_DIM
    vec = pl.BlockSpec((1, DA_HEAD_DIM), lambda bi, h, i: (0, 0))
    return pl.pallas_call(
        functools.partial(_da_attn_body, tq=tq, tk=tk, lambda_init=lambda_init),
        grid=(b, DA_HEADS, s // tq),
        in_specs=[
            pl.BlockSpec(memory_space=pltpu.SMEM),
            vec, vec, vec, vec,
            pl.BlockSpec((1, w), lambda bi, h, i: (0, 0)),
            pl.BlockSpec((1, tq, w), lambda bi, h, i: (bi, i, h)),
            pl.BlockSpec((1, s, w), lambda bi, h, i: (bi, 0, h)),
            pl.BlockSpec((1, s, w), lambda bi, h, i: (bi, 0, h)),
        ],
        out_specs=pl.BlockSpec((1, tq, w), lambda bi, h, i: (bi, i, h)),
        out_shape=jax.ShapeDtypeStruct((b, s, DA_HEADS * w), BF16),
        scratch_shapes=[pltpu.VMEM((tq, w), F32), pltpu.VMEM((tq, w), F32),
                        pltpu.VMEM((2, tq, tk), F32), pltpu.VMEM((2, tq, tk), BF16)],
        compiler_params=_cparams(("parallel", "parallel", "arbitrary")),
        name="da_attn",
    )(slopes, *lams, sg, q, k, v)


def _mla_attn_body(qn_ref, qr_ref, kn_ref, kr_ref, v_ref, o_ref, acc_a_ref, acc_b_ref,
                   s_ref, p_ref, *, tq, tk):
    qi = pl.program_id(2)
    q0 = qi * tq
    n_full = qi // (tk // tq)
    qn = qn_ref[0]
    qr = qr_ref[0]
    qa = jnp.concatenate([qn[:, :LANES], qr], axis=-1)
    qb = jnp.concatenate([qn[:, LANES:], qr], axis=-1)
    acc_a_ref[...] = jnp.zeros_like(acc_a_ref)
    acc_b_ref[...] = jnp.zeros_like(acc_b_ref)

    def score(j, last):
        k0 = pl.multiple_of(j * tk, tk)

        def k_tile(lo):
            def tile(n):
                rows = pl.ds(pl.multiple_of(k0 + n * MXU_WIDTH, MXU_WIDTH), MXU_WIDTH)
                return jnp.concatenate([kn_ref[0, rows, lo:lo + LANES],
                                        kr_ref[0, rows, lo:lo + LANES]], axis=-1)
            return tile

        keep = (lambda n: _chunk_mask(q0, k0 + n * MXU_WIDTH, tq, MXU_WIDTH)) if last else None
        return (_score_pass(qa, k_tile(0), None, keep, s_ref.at[0], tk),
                _score_pass(qb, k_tile(LANES), None, keep, s_ref.at[1], tk))

    def prob_pv(j, state, mx):
        ma, la, mb, lb = state
        ma, la, aa = _prob_pass(s_ref.at[0], p_ref.at[0], ma, la, mx[0], tk)
        mb, lb, ab = _prob_pass(s_ref.at[1], p_ref.at[1], mb, lb, mx[1], tk)
        vb = v_ref[0, pl.ds(pl.multiple_of(j * tk, tk), tk), :]
        acc_a_ref[...] = aa * acc_a_ref[...] + jnp.dot(p_ref[0], vb[:, :LANES], preferred_element_type=F32)
        acc_b_ref[...] = ab * acc_b_ref[...] + jnp.dot(p_ref[1], vb[:, LANES:], preferred_element_type=F32)
        return ma, la, mb, lb

    def finalize(state):
        _, la, _, lb = state
        o_ref[0] = jnp.concatenate([acc_a_ref[...] / la, acc_b_ref[...] / lb], axis=-1).astype(BF16)

    _pipelined_key_blocks(n_full, score, prob_pv, finalize, tq)


def _mla_attn(qn, qr, kn, kr, v, *, tq, tk):
    b, s, _ = qn.shape
    w = 2 * LANES
    q_spec = pl.BlockSpec((1, tq, w), lambda bi, h, i: (bi, i, h))
    kv_spec = pl.BlockSpec((1, s, w), lambda bi, h, i: (bi, 0, h))
    return pl.pallas_call(
        functools.partial(_mla_attn_body, tq=tq, tk=tk),
        grid=(b, MLA_HEADS // 2, s // tq),
        in_specs=[
            q_spec,
            pl.BlockSpec((1, tq, LANES), lambda bi, h, i: (bi, i, h)),
            kv_spec, kv_spec, kv_spec,
        ],
        out_specs=q_spec,
        out_shape=jax.ShapeDtypeStruct((b, s, MLA_HEADS * LANES), BF16),
        scratch_shapes=[pltpu.VMEM((tq, LANES), F32), pltpu.VMEM((tq, LANES), F32),
                        pltpu.VMEM((2, tq, tk), F32), pltpu.VMEM((2, tq, tk), BF16)],
        compiler_params=_cparams(("parallel", "parallel", "arbitrary")),
        name="mla_attn",
    )(qn, qr, kn, kr, v)


def _merge_body(x_ref, ya_ref, yb_ref, gate_a_ref, gate_b_ref, wa_ref, wb_ref, wo_ref, o_ref):
    @pl.when(pl.program_id(1) == 0)
    def _():
        o_ref[...] = x_ref[...]

    a = jnp.dot(ya_ref[...], wa_ref[...], preferred_element_type=F32)
    b = jnp.dot(yb_ref[...], wb_ref[...], preferred_element_type=F32)
    m = (gate_a_ref[...].astype(F32) * a + gate_b_ref[...].astype(F32) * b).astype(BF16)
    o_ref[...] += jnp.dot(m, wo_ref[...], preferred_element_type=F32)


def _merge(x, ya, yb, gates, wa, wb, wo, *, tm, tn):
    t, d = x.shape
    nj = d // tn
    tok = pl.BlockSpec((tm, d), lambda i, j: (i, 0))
    return pl.pallas_call(
        _merge_body,
        grid=(t // tm, nj),
        in_specs=[
            pl.BlockSpec((tm, d), lambda i, j: (i, 0), pipeline_mode=pl.Buffered(1)),
            tok, tok,
            pl.BlockSpec((tm, tn), lambda i, j: (i, j)),
            pl.BlockSpec((tm, tn), lambda i, j: (i, j + nj)),
            pl.BlockSpec((d, tn), lambda i, j: (0, j)),
            pl.BlockSpec((d, tn), lambda i, j: (0, j)),
            pl.BlockSpec((tn, d), lambda i, j: (j, 0)),
        ],
        out_specs=tok,
        out_shape=jax.ShapeDtypeStruct((t, d), F32),
        compiler_params=_cparams(("parallel", "arbitrary")),
        name="merge",
    )(x, ya, yb, gates, gates, wa, wb, wo)


def _rot_cols(w):
    half = w.shape[-1] // 2
    return jnp.concatenate([-w[..., half:], w[..., :half]], axis=-1)


def _swap_halves(g):
    half = g.shape[-1] // 2
    return jnp.concatenate([g[..., half:], g[..., :half]], axis=-1)


def _pick_tile(n, pref):
    return pref if n % pref == 0 else n


def kernel(x, ffn1_norm_g, ffn1_w1, ffn1_w3, ffn1_w2, mix_norm_g, w_in, da_q_norm_g, da_k_norm_g, da_lambda_q1, da_lambda_k1, da_lambda_q2, da_lambda_k2, da_subln_g, mla_q_a_norm_g, mla_w_qb, mla_kv_a_norm_g, mla_w_kvb, mla_q_norm_g, mla_k_norm_g, w_branch_a, w_branch_b, w_out, ffn2_norm_g, ffn2_w1, ffn2_w3, ffn2_w2):
    bsz, seq, d = x.shape
    t = bsz * seq
    depth = ffn1_norm_g.shape[0]
    tm = _pick_tile(t, 512)
    tm_big = _pick_tile(t, 1024)
    tq = _pick_tile(seq, 256)
    tk = _pick_tile(seq, 1024)
    row = lambda v: v.reshape(1, -1).astype(F32)

    inv = ROPE_THETA ** (-jnp.arange(0, MLA_ROPE_DIM, 2, dtype=F32) / MLA_ROPE_DIM)
    ang = jnp.arange(seq, dtype=F32)[:, None] * inv[None, :]
    cos2 = jnp.tile(jnp.cos(ang), (1, 4))
    sin2 = jnp.tile(jnp.sin(ang), (1, 4))
    slopes = (2.0 ** (-8.0 * jnp.arange(1, DA_HEADS + 1, dtype=F32) / DA_HEADS)) * LOG2E

    xt = x.reshape(t, d)
    for l in range(depth):
        lambda_init = 0.8 - 0.6 * math.exp(-0.3 * l)

        xt = _ffn(xt, row(ffn1_norm_g[l]), ffn1_w1[l].astype(BF16), ffn1_w3[l].astype(BF16),
                  (0.5 * ffn1_w2[l]).astype(BF16), tm=tm_big, tf=512)

        w = w_in[l]
        o = [0]
        for n in (2 * DA_HEADS * DA_HEAD_DIM, 2 * DA_HEADS * DA_HEAD_DIM, DA_HEADS * DA_V_DIM,
                  MLA_Q_RANK, MLA_KV_RANK, MLA_ROPE_DIM, d, d):
            o.append(o[-1] + n)
        w_kr = w[:, o[5]:o[6]]
        w_kr_rot = _rot_cols(w_kr)
        w_cat = jnp.concatenate(
            [w[:, o[0]:o[5]], w_kr, w_kr, w_kr_rot, w_kr_rot, w[:, o[6]:]], axis=1).astype(BF16)
        q_scale = DA_HEAD_DIM ** -0.5 * LOG2E
        q_da, k_da, v_da, lat, gates = _in_proj(
            xt, row(mix_norm_g[l]), w_cat, row(da_q_norm_g[l]) * q_scale, row(da_k_norm_g[l]), tm=tm_big)

        wq = mla_w_qb[l].reshape(MLA_Q_RANK, MLA_HEADS, MLA_QK_DIM)
        wq_nope = wq[:, :, :MLA_NOPE_DIM].reshape(MLA_Q_RANK, MLA_HEADS // 2, 2 * LANES)
        wq_rope = wq[:, :, MLA_NOPE_DIM:]
        wq_rot = _rot_cols(wq_rope).reshape(MLA_Q_RANK, MLA_HEADS // 2, LANES)
        wq_rope = wq_rope.reshape(MLA_Q_RANK, MLA_HEADS // 2, LANES)
        wq_packed = jnp.concatenate([wq_nope, wq_rope, wq_rot], axis=-1).reshape(
            MLA_Q_RANK, MLA_HEADS * 2 * LANES).astype(BF16)
        mla_scale = MLA_QK_DIM ** -0.5 * LOG2E
        gq, gk = mla_q_norm_g[l], mla_k_norm_g[l]
        gains = (
            row(gq[:MLA_NOPE_DIM]) * mla_scale,
            row(jnp.tile(gq[MLA_NOPE_DIM:], 2)) * mla_scale,
            row(jnp.tile(_swap_halves(gq[MLA_NOPE_DIM:]), 2)) * mla_scale,
            row(gk[:MLA_NOPE_DIM]),
            row(jnp.tile(gk[MLA_NOPE_DIM:], 2)),
            row(jnp.tile(_swap_halves(gk[MLA_NOPE_DIM:]), 2)),
        )
        qn, qr, kn, kr, v_mla = _mla_proj(
            lat, row(mla_q_a_norm_g[l]), row(mla_kv_a_norm_g[l]), wq_packed,
            mla_w_kvb[l].astype(BF16), gains, cos2, sin2, tm=_pick_tile(seq, 256), seq=seq)

        b3 = lambda a: a.reshape(bsz, seq, a.shape[-1])
        lams = (row(da_lambda_q1[l]), row(da_lambda_k1[l]), row(da_lambda_q2[l]), row(da_lambda_k2[l]))
        y_a = _da_attn(slopes, lams, row(da_subln_g[l]), b3(q_da), b3(k_da), b3(v_da),
                       tq=tq, tk=tk, lambda_init=lambda_init)
        y_b = _mla_attn(b3(qn), b3(qr), b3(kn), b3(kr), b3(v_mla), tq=tq, tk=tk)

        xt = _merge(xt, y_a.reshape(t, -1), y_b.reshape(t, -1), gates,
                    w_branch_a[l].astype(BF16), w_branch_b[l].astype(BF16), w_out[l].astype(BF16),
                    tm=tm, tn=_pick_tile(d, 1024))

        xt = _ffn(xt, row(ffn2_norm_g[l]), ffn2_w1[l].astype(BF16), ffn2_w3[l].astype(BF16),
                  (0.5 * ffn2_w2[l]).astype(BF16), tm=tm_big, tf=512)
    return xt.reshape(bsz, seq, d)
```

```python
import functools
import math

import jax
import jax.numpy as jnp
from jax import lax
from jax.experimental import pallas as pl
from jax.experimental.pallas import tpu as pltpu

F32 = jnp.float32
BF16 = jnp.bfloat16

EPS = 1e-6
CHUNK = 64
CHUNK_SHIFT = CHUNK.bit_length() - 1
DA_HEADS = 8
DA_HEAD_DIM = 128
DA_V_DIM = 2 * DA_HEAD_DIM
MLA_HEADS = 16
MLA_Q_RANK = 768
MLA_KV_RANK = 512
MLA_NOPE_DIM = 128
MLA_ROPE_DIM = 64
MLA_V_DIM = 128
MLA_QK_DIM = MLA_NOPE_DIM + MLA_ROPE_DIM
ROPE_THETA = 10000.0
LOG2E = math.log2(math.e)
MASK_VALUE = -1e30

LANES = 128
SUBLANES = 8
MXU_WIDTH = 256
V7X_VMEM_BYTES = 64 * 1024 * 1024
VMEM_LIMIT = V7X_VMEM_BYTES - 8 * 1024 * 1024


def _cparams(sem):
    return pltpu.CompilerParams(dimension_semantics=sem, vmem_limit_bytes=VMEM_LIMIT)


def _rms(x, g):
    ms = jnp.mean(x * x, axis=-1, keepdims=True)
    return x * lax.rsqrt(ms + EPS) * g


def _nt_dot(a, b):
    return lax.dot_general(a, b, (((1,), (1,)), ((), ())), preferred_element_type=F32)


def _ffn_body(x_ref, g_ref, w1_ref, w3_ref, w2_ref, o_ref, h_ref):
    @pl.when(pl.program_id(1) == 0)
    def _():
        x = x_ref[...]
        h_ref[...] = _rms(x, g_ref[...]).astype(BF16)
        o_ref[...] = x

    h = h_ref[...]
    a = jnp.dot(h, w1_ref[...], preferred_element_type=F32)
    b = jnp.dot(h, w3_ref[...], preferred_element_type=F32)
    u = (a * jax.nn.sigmoid(a) * b).astype(BF16)
    o_ref[...] += jnp.dot(u, w2_ref[...], preferred_element_type=F32)


def _ffn(x, g, w1, w3, w2_half, *, tm, tf):
    t, d = x.shape
    f = w1.shape[1]
    return pl.pallas_call(
        _ffn_body,
        grid=(t // tm, f // tf),
        in_specs=[
            pl.BlockSpec((tm, d), lambda i, j: (i, 0)),
            pl.BlockSpec((1, d), lambda i, j: (0, 0)),
            pl.BlockSpec((d, tf), lambda i, j: (0, j)),
            pl.BlockSpec((d, tf), lambda i, j: (0, j)),
            pl.BlockSpec((tf, d), lambda i, j: (j, 0)),
        ],
        out_specs=pl.BlockSpec((tm, d), lambda i, j: (i, 0)),
        out_shape=jax.ShapeDtypeStruct((t, d), F32),
        scratch_shapes=[pltpu.VMEM((tm, d), BF16)],
        compiler_params=_cparams(("parallel", "arbitrary")),
        name="ffn",
    )(x, g, w1, w3, w2_half)


PROJ_TN = 512
_Q0, _K0, _V0, _C0, _G0, _NJ = 0, 4, 8, 12, 15, 23


def _head_norm(acc, g):
    parts = []
    for c in range(acc.shape[1] // LANES):
        blk = acc[:, c * LANES:(c + 1) * LANES]
        ms = jnp.mean(blk * blk, axis=-1, keepdims=True)
        parts.append(blk * lax.rsqrt(ms + EPS) * g)
    return jnp.concatenate(parts, axis=-1)


def _head_norm_t(acc_t, g_t):
    parts = []
    for c in range(acc_t.shape[0] // LANES):
        blk = acc_t[c * LANES:(c + 1) * LANES, :]
        ms = jnp.mean(blk * blk, axis=0, keepdims=True)
        parts.append(blk * lax.rsqrt(ms + EPS) * g_t)
    return jnp.concatenate(parts, axis=0)


def _proj_body(x_ref, g_ref, wt_ref, wn_ref, qg_ref, kg_ref,
               qt_ref, k_ref, vt_ref, c_ref, gate_ref, h_ref):
    j = pl.program_id(1)

    @pl.when(j == 0)
    def _():
        h_ref[...] = _rms(x_ref[...], g_ref[...]).astype(BF16)

    @pl.when(j < _K0)
    def _():
        qt_ref[...] = _head_norm_t(_nt_dot(wt_ref[...], h_ref[...]), qg_ref[...]).astype(BF16)

    @pl.when((j >= _K0) & (j < _V0))
    def _():
        acc = jnp.dot(h_ref[...], wn_ref[...], preferred_element_type=F32)
        k_ref[...] = _head_norm(acc, kg_ref[...]).astype(BF16)

    @pl.when((j >= _V0) & (j < _C0))
    def _():
        vt_ref[...] = _nt_dot(wt_ref[...], h_ref[...]).astype(BF16)

    @pl.when((j >= _C0) & (j < _G0))
    def _():
        c_ref[...] = jnp.dot(h_ref[...], wn_ref[...], preferred_element_type=F32)

    @pl.when(j >= _G0)
    def _():
        acc = jnp.dot(h_ref[...], wn_ref[...], preferred_element_type=F32)
        gate_ref[...] = jax.nn.sigmoid(acc).astype(BF16)


def _in_proj(x, g, w_t, w_n, qg_t, kg, *, tm):
    t, d = x.shape
    tn = PROJ_TN
    n_q, n_k, n_v, n_c, n_g = _K0 - _Q0, _V0 - _K0, _C0 - _V0, _G0 - _C0, _NJ - _G0

    def wt_idx(i, j):
        return (jnp.where(j < _K0, j, jnp.clip(j - n_k, n_q - 1, n_q + n_v - 1)), 0)

    def wn_idx(i, j):
        return (0, jnp.where(j < _V0, jnp.clip(j - _K0, 0, n_k - 1),
                             jnp.clip(j - _K0 - n_v, n_k - 1, n_k + n_c + n_g - 1)))

    def cols(j0, n):
        return pl.BlockSpec((tm, tn), lambda i, j: (i, jnp.clip(j - j0, 0, n - 1)))

    def rows(j0, n):
        return pl.BlockSpec((tn, tm), lambda i, j: (jnp.clip(j - j0, 0, n - 1), i))

    return pl.pallas_call(
        _proj_body,
        grid=(t // tm, _NJ),
        in_specs=[
            pl.BlockSpec((tm, d), lambda i, j: (i, 0)),
            pl.BlockSpec((1, d), lambda i, j: (0, 0)),
            pl.BlockSpec((tn, d), wt_idx),
            pl.BlockSpec((d, tn), wn_idx),
            pl.BlockSpec((LANES, tm), lambda i, j: (0, 0)),
            pl.BlockSpec((1, LANES), lambda i, j: (0, 0)),
        ],
        out_specs=[rows(_Q0, n_q), cols(_K0, n_k), rows(_V0, n_v), cols(_C0, n_c), cols(_G0, n_g)],
        out_shape=[
            jax.ShapeDtypeStruct((n_q * tn, t), BF16),
            jax.ShapeDtypeStruct((t, n_k * tn), BF16),
            jax.ShapeDtypeStruct((n_v * tn, t), BF16),
            jax.ShapeDtypeStruct((t, n_c * tn), F32),
            jax.ShapeDtypeStruct((t, n_g * tn), BF16),
        ],
        scratch_shapes=[pltpu.VMEM((tm, d), BF16)],
        compiler_params=_cparams(("parallel", "arbitrary")),
        name="in_proj",
    )(x, g, w_t, w_n, qg_t, kg)


def _mla_proj_body(c_ref, qa_g_ref, kva_g_ref, wqt_ref, wkn_ref, wvt_ref,
                   gqn_ref, gqr_ref, gqt_ref, gk_nope_ref, gk_rope_ref, gk_rot_ref,
                   cos_ref, sin_ref, cost_ref, sint_ref,
                   qnt_ref, qrt_ref, kn_ref, kr_ref, vt_ref):
    c = c_ref[...]
    o_kv = MLA_Q_RANK
    o_kr = MLA_Q_RANK + MLA_KV_RANK
    cqn = _rms(c[:, :o_kv], qa_g_ref[...]).astype(BF16)
    ckvn = _rms(c[:, o_kv:o_kr], kva_g_ref[...]).astype(BF16)
    k_rope = c[:, o_kr:o_kr + LANES]
    k_rot = c[:, o_kr + LANES:o_kr + 2 * LANES]
    first = lax.broadcasted_iota(jnp.int32, (1, LANES), 1) < MLA_ROPE_DIM
    k_roped = k_rope * gk_rope_ref[...] * cos_ref[...] + k_rot * gk_rot_ref[...] * sin_ref[...]
    k_rope_sq = 0.5 * jnp.sum(k_rope * k_rope, axis=-1, keepdims=True)
    inv_d = 1.0 / MLA_QK_DIM
    cos_t = cost_ref[...]
    sin_t = sint_ref[...]
    half = MLA_ROPE_DIM

    for p in range(MLA_HEADS // 2):
        qraw = _nt_dot(wqt_ref[p * 4 * LANES:(p + 1) * 4 * LANES, :], cqn)
        na, nb = qraw[:LANES], qraw[LANES:2 * LANES]
        rp, rt = qraw[2 * LANES:3 * LANES], qraw[3 * LANES:]
        rp2 = rp * rp
        sa = jnp.sum(na * na, axis=0, keepdims=True) + jnp.sum(rp2[:half], axis=0, keepdims=True)
        sb = jnp.sum(nb * nb, axis=0, keepdims=True) + jnp.sum(rp2[half:], axis=0, keepdims=True)
        ra = lax.rsqrt(sa * inv_d + EPS)
        rb = lax.rsqrt(sb * inv_d + EPS)
        ra_rows = slice((2 * p) * LANES, (2 * p + 1) * LANES)
        rb_rows = slice((2 * p + 1) * LANES, (2 * p + 2) * LANES)
        qnt_ref[ra_rows, :] = (na * ra * gqn_ref[...]).astype(BF16)
        qnt_ref[rb_rows, :] = (nb * rb * gqn_ref[...]).astype(BF16)
        roped = rp * gqr_ref[...] * cos_t + rt * gqt_ref[...] * sin_t
        qrt_ref[p * LANES:(p + 1) * LANES, :] = jnp.concatenate(
            [roped[:half] * ra, roped[half:] * rb], axis=0).astype(BF16)

        knraw = jnp.dot(ckvn, wkn_ref[:, p * 2 * LANES:(p + 1) * 2 * LANES], preferred_element_type=F32)
        kna, knb = knraw[:, :LANES], knraw[:, LANES:]
        rka = lax.rsqrt((jnp.sum(kna * kna, axis=-1, keepdims=True) + k_rope_sq) * inv_d + EPS)
        rkb = lax.rsqrt((jnp.sum(knb * knb, axis=-1, keepdims=True) + k_rope_sq) * inv_d + EPS)
        kn_ref[:, ra_rows] = (kna * rka * gk_nope_ref[...]).astype(BF16)
        kn_ref[:, rb_rows] = (knb * rkb * gk_nope_ref[...]).astype(BF16)
        kr_ref[:, ra_rows] = jnp.where(first, k_roped * rka, 0.0).astype(BF16)
        kr_ref[:, rb_rows] = jnp.where(first, 0.0, k_roped * rkb).astype(BF16)

        vt_ref[p * 2 * LANES:(p + 1) * 2 * LANES, :] = _nt_dot(
            wvt_ref[p * 2 * LANES:(p + 1) * 2 * LANES, :], ckvn).astype(BF16)


def _mla_proj(c, qa_g, kva_g, wq_t, wkn, wv_t, q_gains_t, k_gains, cos2, sin2, cos_t, sin_t, *, tm, seq):
    t = c.shape[0]
    n_pos = seq // tm
    const = lambda a: pl.BlockSpec(a.shape, lambda i: (0, 0))
    hw = MLA_HEADS * LANES
    tok = lambda w: pl.BlockSpec((tm, w), lambda i: (i, 0))
    tok_t = lambda w: pl.BlockSpec((w, tm), lambda i: (0, i))
    return pl.pallas_call(
        _mla_proj_body,
        grid=(t // tm,),
        in_specs=[tok(c.shape[1]), const(qa_g), const(kva_g), const(wq_t), const(wkn), const(wv_t)]
        + [const(a) for a in q_gains_t] + [const(a) for a in k_gains] + [
            pl.BlockSpec((tm, LANES), lambda i: (i % n_pos, 0)),
            pl.BlockSpec((tm, LANES), lambda i: (i % n_pos, 0)),
            pl.BlockSpec((LANES, tm), lambda i: (0, i % n_pos)),
            pl.BlockSpec((LANES, tm), lambda i: (0, i % n_pos)),
        ],
        out_specs=[tok_t(hw), tok_t(hw // 2), tok(hw), tok(hw), tok_t(hw)],
        out_shape=[
            jax.ShapeDtypeStruct((hw, t), BF16),
            jax.ShapeDtypeStruct((hw // 2, t), BF16),
            jax.ShapeDtypeStruct((t, hw), BF16),
            jax.ShapeDtypeStruct((t, hw), BF16),
            jax.ShapeDtypeStruct((hw, t), BF16),
        ],
        compiler_params=_cparams(("parallel",)),
        name="mla_proj",
    )(c, qa_g, kva_g, wq_t, wkn, wv_t, *q_gains_t, *k_gains, cos2, sin2, cos_t, sin_t)


def _score_pass(k_tile, q_t, add, keep, offs, s_ref, tk):
    tq = q_t.shape[1]
    m8 = None
    for n in range(tk // MXU_WIDTH):
        u = jnp.dot(k_tile(n), q_t, preferred_element_type=F32)
        if add is not None:
            u = u + add(n)
        if keep is not None:
            u = jnp.where(keep(n), u, MASK_VALUE)
        s_ref[n * MXU_WIDTH:(n + 1) * MXU_WIDTH, :] = u
        t = jnp.max(u.reshape(MXU_WIDTH // SUBLANES, SUBLANES, tq), axis=0)
        if offs is not None:
            t = t + offs(n)
        m8 = t if m8 is None else jnp.maximum(m8, t)
    return jnp.max(m8, axis=0, keepdims=True)


def _prob_pass(s_ref, p_ref, m, l, m_blk, offs, tk):
    tq = s_ref.shape[1]
    m_new = jnp.maximum(m, m_blk)
    alpha = jnp.exp2(m - m_new)
    ps8 = None
    for n in range(tk // MXU_WIDTH):
        rows = slice(n * MXU_WIDTH, (n + 1) * MXU_WIDTH)
        ref = m_new if offs is None else m_new - offs(n)
        p = jnp.exp2(s_ref[rows, :] - ref)
        t = jnp.sum(p.reshape(MXU_WIDTH // SUBLANES, SUBLANES, tq), axis=0)
        ps8 = t if ps8 is None else ps8 + t
        p_ref[rows, :] = p.astype(BF16)
    l_new = alpha * l + jnp.sum(ps8, axis=0, keepdims=True)
    return m_new, l_new, alpha


def _pipelined_key_blocks(n_full, score, prob_pv, finalize, tq):
    neg = jnp.full((1, tq), MASK_VALUE, F32)
    zero = jnp.zeros((1, tq), F32)
    init = (neg, zero, neg, zero)

    @pl.when(n_full == 0)
    def _():
        finalize(prob_pv(0, True, init, score(0, True)))

    @pl.when(n_full > 0)
    def _():
        def body(j, carry):
            state, mx = carry
            state = prob_pv(j, False, state, mx)
            return state, score(j + 1, False)

        state, mx = lax.fori_loop(0, n_full - 1, body, (init, score(0, False)))
        state = prob_pv(n_full - 1, False, state, mx)
        mx_last = score(n_full, True)
        finalize(prob_pv(n_full, True, state, mx_last))


def _key_chunk_mask(q0, k0, tq):
    key = lax.broadcasted_iota(jnp.int32, (MXU_WIDTH, tq), 0)
    qry = lax.broadcasted_iota(jnp.int32, (MXU_WIDTH, tq), 1)
    return ((k0 + key) >> CHUNK_SHIFT) <= ((q0 + qry) >> CHUNK_SHIFT)


def _da_attn_body(slopes_ref, lq1_ref, lk1_ref, lq2_ref, lk2_ref, sg_ref,
                  qt_ref, k_ref, vt_ref, o_ref, acc1_ref, acc2_ref, s_ref, p_ref,
                  *, tq, tk, lambda_init):
    h = pl.program_id(1)
    qi = pl.program_id(2)
    slope = slopes_ref[h]
    q1_t = qt_ref[:DA_HEAD_DIM, :]
    q2_t = qt_ref[DA_HEAD_DIM:, :]
    q0 = qi * tq
    n_full = qi // (tk // tq)
    key = lax.broadcasted_iota(jnp.int32, (MXU_WIDTH, tq), 0)
    qry = lax.broadcasted_iota(jnp.int32, (MXU_WIDTH, tq), 1)
    key_bias = slope * key.astype(F32)
    acc1_ref[...] = jnp.zeros_like(acc1_ref)
    acc2_ref[...] = jnp.zeros_like(acc2_ref)

    def tile_offset(j):
        return lambda n: slope * (j * tk + n * MXU_WIDTH - q0).astype(F32)

    def score(j, last):
        k0 = pl.multiple_of(j * tk, tk)

        def k_tile(lo):
            return lambda n: k_ref[0, pl.ds(pl.multiple_of(k0 + n * MXU_WIDTH, MXU_WIDTH), MXU_WIDTH),
                                   lo:lo + DA_HEAD_DIM]

        if last:
            def add(n):
                dist = jnp.abs((q0 - k0 - n * MXU_WIDTH) + qry - key)
                return slope * (qry - dist).astype(F32)
            keep = lambda n: _key_chunk_mask(q0, k0 + n * MXU_WIDTH, tq)
            offs = None
        else:
            add = lambda n: key_bias
            keep = None
            offs = tile_offset(j)
        return (_score_pass(k_tile(0), q1_t, add, keep, offs, s_ref.at[0], tk),
                _score_pass(k_tile(DA_HEAD_DIM), q2_t, add, keep, offs, s_ref.at[1], tk))

    def prob_pv(j, last, state, mx):
        m1, l1, m2, l2 = state
        offs = None if last else tile_offset(j)
        m1, l1, a1 = _prob_pass(s_ref.at[0], p_ref.at[0], m1, l1, mx[0], offs, tk)
        m2, l2, a2 = _prob_pass(s_ref.at[1], p_ref.at[1], m2, l2, mx[1], offs, tk)
        vb = vt_ref[:, pl.ds(pl.multiple_of(j * tk, tk), tk)]
        acc1_ref[...] = a1 * acc1_ref[...] + jnp.dot(vb, p_ref[0], preferred_element_type=F32)
        acc2_ref[...] = a2 * acc2_ref[...] + jnp.dot(vb, p_ref[1], preferred_element_type=F32)
        return m1, l1, m2, l2

    def finalize(state):
        _, l1, _, l2 = state
        lam = (jnp.exp(jnp.sum(lq1_ref[...] * lk1_ref[...], axis=-1, keepdims=True))
               - jnp.exp(jnp.sum(lq2_ref[...] * lk2_ref[...], axis=-1, keepdims=True))
               + lambda_init)
        o = acc1_ref[...] * (1.0 / l1) - lam * (acc2_ref[...] * (1.0 / l2))
        ms = jnp.mean(o * o, axis=0, keepdims=True)
        y = o * lax.rsqrt(ms + EPS) * sg_ref[...] * (1.0 - lambda_init)
        o_ref[0] = y.T.astype(BF16)

    _pipelined_key_blocks(n_full, score, prob_pv, finalize, tq)


def _da_attn(slopes, lams, sg_t, q_t, k, v_t, *, tq, tk, lambda_init):
    b, s, _ = k.shape
    w = DA_V_DIM
    nq = s // tq
    vec = pl.BlockSpec((1, DA_HEAD_DIM), lambda bi, h, i: (0, 0))
    return pl.pallas_call(
        functools.partial(_da_attn_body, tq=tq, tk=tk, lambda_init=lambda_init),
        grid=(b, DA_HEADS, nq),
        in_specs=[
            pl.BlockSpec(memory_space=pltpu.SMEM),
            vec, vec, vec, vec,
            pl.BlockSpec((w, tq), lambda bi, h, i: (0, 0)),
            pl.BlockSpec((w, tq), lambda bi, h, i: (h, bi * nq + i)),
            pl.BlockSpec((1, s, w), lambda bi, h, i: (bi, 0, h)),
            pl.BlockSpec((w, s), lambda bi, h, i: (h, bi)),
        ],
        out_specs=pl.BlockSpec((1, tq, w), lambda bi, h, i: (bi, i, h)),
        out_shape=jax.ShapeDtypeStruct((b, s, DA_HEADS * w), BF16),
        scratch_shapes=[pltpu.VMEM((w, tq), F32), pltpu.VMEM((w, tq), F32),
                        pltpu.VMEM((2, tk, tq), F32), pltpu.VMEM((2, tk, tq), BF16)],
        compiler_params=_cparams(("parallel", "parallel", "arbitrary")),
        name="da_attn",
    )(slopes, *lams, sg_t, q_t, k, v_t)


def _mla_attn_body(qnt_ref, qrt_ref, kn_ref, kr_ref, vt_ref, o_ref, acc_a_ref, acc_b_ref,
                   s_ref, p_ref, *, tq, tk):
    qi = pl.program_id(2)
    q0 = qi * tq
    n_full = qi // (tk // tq)
    qr_t = qrt_ref[...]
    qa_t = jnp.concatenate([qnt_ref[:LANES, :], qr_t], axis=0)
    qb_t = jnp.concatenate([qnt_ref[LANES:, :], qr_t], axis=0)
    acc_a_ref[...] = jnp.zeros_like(acc_a_ref)
    acc_b_ref[...] = jnp.zeros_like(acc_b_ref)

    def score(j, last):
        k0 = pl.multiple_of(j * tk, tk)

        def k_tile(lo):
            def tile(n):
                rows = pl.ds(pl.multiple_of(k0 + n * MXU_WIDTH, MXU_WIDTH), MXU_WIDTH)
                return jnp.concatenate([kn_ref[0, rows, lo:lo + LANES],
                                        kr_ref[0, rows, lo:lo + LANES]], axis=-1)
            return tile

        keep = (lambda n: _key_chunk_mask(q0, k0 + n * MXU_WIDTH, tq)) if last else None
        return (_score_pass(k_tile(0), qa_t, None, keep, None, s_ref.at[0], tk),
                _score_pass(k_tile(LANES), qb_t, None, keep, None, s_ref.at[1], tk))

    def prob_pv(j, last, state, mx):
        ma, la, mb, lb = state
        ma, la, aa = _prob_pass(s_ref.at[0], p_ref.at[0], ma, la, mx[0], None, tk)
        mb, lb, ab = _prob_pass(s_ref.at[1], p_ref.at[1], mb, lb, mx[1], None, tk)
        cols = pl.ds(pl.multiple_of(j * tk, tk), tk)
        acc_a_ref[...] = aa * acc_a_ref[...] + jnp.dot(vt_ref[:LANES, cols], p_ref[0],
                                                      preferred_element_type=F32)
        acc_b_ref[...] = ab * acc_b_ref[...] + jnp.dot(vt_ref[LANES:, cols], p_ref[1],
                                                      preferred_element_type=F32)
        return ma, la, mb, lb

    def finalize(state):
        _, la, _, lb = state
        o_t = jnp.concatenate([acc_a_ref[...] * (1.0 / la), acc_b_ref[...] * (1.0 / lb)], axis=0)
        o_ref[0] = o_t.T.astype(BF16)

    _pipelined_key_blocks(n_full, score, prob_pv, finalize, tq)


def _mla_attn(qn_t, qr_t, kn, kr, v_t, *, tq, tk):
    b, s, _ = kn.shape
    w = 2 * LANES
    nq = s // tq
    kv_spec = pl.BlockSpec((1, s, w), lambda bi, h, i: (bi, 0, h))
    return pl.pallas_call(
        functools.partial(_mla_attn_body, tq=tq, tk=tk),
        grid=(b, MLA_HEADS // 2, nq),
        in_specs=[
            pl.BlockSpec((w, tq), lambda bi, h, i: (h, bi * nq + i)),
            pl.BlockSpec((LANES, tq), lambda bi, h, i: (h, bi * nq + i)),
            kv_spec, kv_spec,
            pl.BlockSpec((w, s), lambda bi, h, i: (h, bi)),
        ],
        out_specs=pl.BlockSpec((1, tq, w), lambda bi, h, i: (bi, i, h)),
        out_shape=jax.ShapeDtypeStruct((b, s, MLA_HEADS * LANES), BF16),
        scratch_shapes=[pltpu.VMEM((LANES, tq), F32), pltpu.VMEM((LANES, tq), F32),
                        pltpu.VMEM((2, tk, tq), F32), pltpu.VMEM((2, tk, tq), BF16)],
        compiler_params=_cparams(("parallel", "parallel", "arbitrary")),
        name="mla_attn",
    )(qn_t, qr_t, kn, kr, v_t)


def _merge_body(x_ref, ya_ref, yb_ref, gate_a_ref, gate_b_ref, wa_ref, wb_ref, wo_ref, o_ref):
    @pl.when(pl.program_id(1) == 0)
    def _():
        o_ref[...] = x_ref[...]

    a = jnp.dot(ya_ref[...], wa_ref[...], preferred_element_type=F32)
    b = jnp.dot(yb_ref[...], wb_ref[...], preferred_element_type=F32)
    m = (gate_a_ref[...].astype(F32) * a + gate_b_ref[...].astype(F32) * b).astype(BF16)
    o_ref[...] += jnp.dot(m, wo_ref[...], preferred_element_type=F32)


def _merge(x, ya, yb, gates, wa, wb, wo, *, tm, tn):
    t, d = x.shape
    nj = d // tn
    tok = pl.BlockSpec((tm, d), lambda i, j: (i, 0))
    return pl.pallas_call(
        _merge_body,
        grid=(t // tm, nj),
        in_specs=[
            tok, tok, tok,
            pl.BlockSpec((tm, tn), lambda i, j: (i, j)),
            pl.BlockSpec((tm, tn), lambda i, j: (i, j + nj)),
            pl.BlockSpec((d, tn), lambda i, j: (0, j)),
            pl.BlockSpec((d, tn), lambda i, j: (0, j)),
            pl.BlockSpec((tn, d), lambda i, j: (j, 0)),
        ],
        out_specs=tok,
        out_shape=jax.ShapeDtypeStruct((t, d), F32),
        compiler_params=_cparams(("parallel", "arbitrary")),
        name="merge",
    )(x, ya, yb, gates, gates, wa, wb, wo)


def _rot_cols(w):
    half = w.shape[-1] // 2
    return jnp.concatenate([-w[..., half:], w[..., :half]], axis=-1)


def _swap_halves(g):
    half = g.shape[-1] // 2
    return jnp.concatenate([g[..., half:], g[..., :half]], axis=-1)


def _pick_tile(n, pref):
    return pref if n % pref == 0 else n


def kernel(x, ffn1_norm_g, ffn1_w1, ffn1_w3, ffn1_w2, mix_norm_g, w_in, da_q_norm_g, da_k_norm_g, da_lambda_q1, da_lambda_k1, da_lambda_q2, da_lambda_k2, da_subln_g, mla_q_a_norm_g, mla_w_qb, mla_kv_a_norm_g, mla_w_kvb, mla_q_norm_g, mla_k_norm_g, w_branch_a, w_branch_b, w_out, ffn2_norm_g, ffn2_w1, ffn2_w3, ffn2_w2):
    bsz, seq, d = x.shape
    t = bsz * seq
    depth = ffn1_norm_g.shape[0]
    tm = _pick_tile(t, 512)
    tm_proj = _pick_tile(t, 1024)
    tm_mla = _pick_tile(seq, 256)
    tq = _pick_tile(seq, 256)
    tk = _pick_tile(seq, 1024)
    row = lambda v: v.reshape(1, -1).astype(F32)
    col_tile = lambda v, n: jnp.broadcast_to(v.astype(F32)[:, None], (v.shape[0], n))

    inv = ROPE_THETA ** (-jnp.arange(0, MLA_ROPE_DIM, 2, dtype=F32) / MLA_ROPE_DIM)
    ang = jnp.arange(seq, dtype=F32)[:, None] * inv[None, :]
    cos2 = jnp.tile(jnp.cos(ang), (1, 4))
    sin2 = jnp.tile(jnp.sin(ang), (1, 4))
    cos_t, sin_t = cos2.T, sin2.T
    slopes = (2.0 ** (-8.0 * jnp.arange(1, DA_HEADS + 1, dtype=F32) / DA_HEADS)) * LOG2E

    xt = x.reshape(t, d)
    for l in range(depth):
        lambda_init = 0.8 - 0.6 * math.exp(-0.3 * l)

        xt = _ffn(xt, row(ffn1_norm_g[l]), ffn1_w1[l].astype(BF16), ffn1_w3[l].astype(BF16),
                  (0.5 * ffn1_w2[l]).astype(BF16), tm=tm, tf=512)

        w = w_in[l]
        o = [0]
        for n in (2 * DA_HEADS * DA_HEAD_DIM, 2 * DA_HEADS * DA_HEAD_DIM, DA_HEADS * DA_V_DIM,
                  MLA_Q_RANK, MLA_KV_RANK, MLA_ROPE_DIM, d, d):
            o.append(o[-1] + n)
        w_kr = w[:, o[5]:o[6]]
        w_kr_rot = _rot_cols(w_kr)
        w_t = jnp.concatenate([w[:, o[0]:o[1]], w[:, o[2]:o[3]]], axis=1).T.astype(BF16)
        w_n = jnp.concatenate(
            [w[:, o[1]:o[2]], w[:, o[3]:o[5]], w_kr, w_kr, w_kr_rot, w_kr_rot, w[:, o[6]:]],
            axis=1).astype(BF16)
        q_scale = DA_HEAD_DIM ** -0.5 * LOG2E
        q_da_t, k_da, v_da_t, lat, gates = _in_proj(
            xt, row(mix_norm_g[l]), w_t, w_n, col_tile(da_q_norm_g[l] * q_scale, tm_proj),
            row(da_k_norm_g[l]), tm=tm_proj)

        wq = mla_w_qb[l].reshape(MLA_Q_RANK, MLA_HEADS, MLA_QK_DIM)
        wq_nope = wq[:, :, :MLA_NOPE_DIM].reshape(MLA_Q_RANK, MLA_HEADS // 2, 2 * LANES)
        wq_rope = wq[:, :, MLA_NOPE_DIM:]
        wq_rot = _rot_cols(wq_rope).reshape(MLA_Q_RANK, MLA_HEADS // 2, LANES)
        wq_rope = wq_rope.reshape(MLA_Q_RANK, MLA_HEADS // 2, LANES)
        wq_t = jnp.concatenate([wq_nope, wq_rope, wq_rot], axis=-1).reshape(
            MLA_Q_RANK, MLA_HEADS * 2 * LANES).T.astype(BF16)
        wkv = mla_w_kvb[l].reshape(MLA_KV_RANK, MLA_HEADS, MLA_NOPE_DIM + MLA_V_DIM)
        wkn = wkv[:, :, :MLA_NOPE_DIM].reshape(MLA_KV_RANK, MLA_HEADS * MLA_NOPE_DIM).astype(BF16)
        wv_t = wkv[:, :, MLA_NOPE_DIM:].reshape(MLA_KV_RANK, MLA_HEADS * MLA_V_DIM).T.astype(BF16)
        mla_scale = MLA_QK_DIM ** -0.5 * LOG2E
        gq, gk = mla_q_norm_g[l], mla_k_norm_g[l]
        q_gains_t = (
            col_tile(gq[:MLA_NOPE_DIM] * mla_scale, tm_mla),
            col_tile(jnp.tile(gq[MLA_NOPE_DIM:], 2) * mla_scale, tm_mla),
            col_tile(jnp.tile(_swap_halves(gq[MLA_NOPE_DIM:]), 2) * mla_scale, tm_mla),
        )
        k_gains = (
            row(gk[:MLA_NOPE_DIM]),
            row(jnp.tile(gk[MLA_NOPE_DIM:], 2)),
            row(jnp.tile(_swap_halves(gk[MLA_NOPE_DIM:]), 2)),
        )
        qn_t, qr_t, kn, kr, v_mla_t = _mla_proj(
            lat, row(mla_q_a_norm_g[l]), row(mla_kv_a_norm_g[l]), wq_t, wkn, wv_t,
            q_gains_t, k_gains, cos2, sin2, cos_t, sin_t, tm=tm_mla, seq=seq)

        b3 = lambda a: a.reshape(bsz, seq, a.shape[-1])
        lams = (row(da_lambda_q1[l]), row(da_lambda_k1[l]), row(da_lambda_q2[l]), row(da_lambda_k2[l]))
        y_a = _da_attn(slopes, lams, col_tile(da_subln_g[l], tq), q_da_t, b3(k_da), v_da_t,
                       tq=tq, tk=tk, lambda_init=lambda_init)
        y_b = _mla_attn(qn_t, qr_t, b3(kn), b3(kr), v_mla_t, tq=tq, tk=tk)

        xt = _merge(xt, y_a.reshape(t, -1), y_b.reshape(t, -1), gates,
                    w_branch_a[l].astype(BF16), w_branch_b[l].astype(BF16), w_out[l].astype(BF16),
                    tm=tm, tn=512)

        xt = _ffn(xt, row(ffn2_norm_g[l]), ffn2_w1[l].astype(BF16), ffn2_w3[l].astype(BF16),
                  (0.5 * ffn2_w2[l]).astype(BF16), tm=tm, tf=512)
    return xt.reshape(bsz, seq, d)
```

```python
import functools
import math

import jax
import jax.numpy as jnp
from jax import lax
from jax.experimental import pallas as pl
from jax.experimental.pallas import tpu as pltpu

F32 = jnp.float32
BF16 = jnp.bfloat16

EPS = 1e-6
CHUNK = 64
CHUNK_SHIFT = CHUNK.bit_length() - 1
DA_HEADS = 8
DA_HEAD_DIM = 128
DA_V_DIM = 2 * DA_HEAD_DIM
MLA_HEADS = 16
MLA_Q_RANK = 768
MLA_KV_RANK = 512
MLA_NOPE_DIM = 128
MLA_ROPE_DIM = 64
MLA_V_DIM = 128
MLA_QK_DIM = MLA_NOPE_DIM + MLA_ROPE_DIM
ROPE_THETA = 10000.0
LOG2E = math.log2(math.e)
MASK_VALUE = -1e30

LANES = 128
SUBLANES = 8
MXU_WIDTH = 256
V7X_VMEM_BYTES = 64 * 1024 * 1024
VMEM_LIMIT = V7X_VMEM_BYTES - 8 * 1024 * 1024


def _cparams(sem):
    return pltpu.CompilerParams(dimension_semantics=sem, vmem_limit_bytes=VMEM_LIMIT)


def _rms(x, g):
    ms = jnp.mean(x * x, axis=-1, keepdims=True)
    return x * lax.rsqrt(ms + EPS) * g


def _nt_dot(a, b):
    return lax.dot_general(a, b, (((1,), (1,)), ((), ())), preferred_element_type=F32)


def _ffn_body(x_ref, g_ref, w1_ref, w3_ref, w2_ref, o_ref, h_ref):
    @pl.when(pl.program_id(1) == 0)
    def _():
        x = x_ref[...]
        h_ref[...] = _rms(x, g_ref[...]).astype(BF16)
        o_ref[...] = x

    h = h_ref[...]
    a = jnp.dot(h, w1_ref[...], preferred_element_type=F32)
    b = jnp.dot(h, w3_ref[...], preferred_element_type=F32)
    u = (a * jax.nn.sigmoid(a) * b).astype(BF16)
    o_ref[...] += jnp.dot(u, w2_ref[...], preferred_element_type=F32)


def _ffn(x, g, w1, w3, w2_half, *, tm, tf):
    t, d = x.shape
    f = w1.shape[1]
    return pl.pallas_call(
        _ffn_body,
        grid=(t // tm, f // tf),
        in_specs=[
            pl.BlockSpec((tm, d), lambda i, j: (i, 0)),
            pl.BlockSpec((1, d), lambda i, j: (0, 0)),
            pl.BlockSpec((d, tf), lambda i, j: (0, j)),
            pl.BlockSpec((d, tf), lambda i, j: (0, j)),
            pl.BlockSpec((tf, d), lambda i, j: (j, 0)),
        ],
        out_specs=pl.BlockSpec((tm, d), lambda i, j: (i, 0)),
        out_shape=jax.ShapeDtypeStruct((t, d), F32),
        scratch_shapes=[pltpu.VMEM((tm, d), BF16)],
        compiler_params=_cparams(("parallel", "arbitrary")),
        name="ffn",
    )(x, g, w1, w3, w2_half)


PROJ_TN = 512
_Q0, _K0, _V0, _C0, _G0, _NJ = 0, 4, 8, 12, 15, 23


def _head_norm(acc, g):
    parts = []
    for c in range(acc.shape[1] // LANES):
        blk = acc[:, c * LANES:(c + 1) * LANES]
        ms = jnp.mean(blk * blk, axis=-1, keepdims=True)
        parts.append(blk * lax.rsqrt(ms + EPS) * g)
    return jnp.concatenate(parts, axis=-1)


def _head_norm_t(acc_t, g_t):
    parts = []
    for c in range(acc_t.shape[0] // LANES):
        blk = acc_t[c * LANES:(c + 1) * LANES, :]
        ms = jnp.mean(blk * blk, axis=0, keepdims=True)
        parts.append(blk * lax.rsqrt(ms + EPS) * g_t)
    return jnp.concatenate(parts, axis=0)


def _proj_body(x_ref, g_ref, wt_ref, wn_ref, qg_ref, kg_ref,
               qt_ref, k_ref, vt_ref, c_ref, gate_ref, h_ref):
    j = pl.program_id(1)

    @pl.when(j == 0)
    def _():
        h_ref[...] = _rms(x_ref[...], g_ref[...]).astype(BF16)

    @pl.when(j < _K0)
    def _():
        qt_ref[...] = _head_norm_t(_nt_dot(wt_ref[...], h_ref[...]), qg_ref[...]).astype(BF16)

    @pl.when((j >= _K0) & (j < _V0))
    def _():
        acc = jnp.dot(h_ref[...], wn_ref[...], preferred_element_type=F32)
        k_ref[...] = _head_norm(acc, kg_ref[...]).astype(BF16)

    @pl.when((j >= _V0) & (j < _C0))
    def _():
        vt_ref[...] = _nt_dot(wt_ref[...], h_ref[...]).astype(BF16)

    @pl.when((j >= _C0) & (j < _G0))
    def _():
        c_ref[...] = jnp.dot(h_ref[...], wn_ref[...], preferred_element_type=F32)

    @pl.when(j >= _G0)
    def _():
        acc = jnp.dot(h_ref[...], wn_ref[...], preferred_element_type=F32)
        gate_ref[...] = jax.nn.sigmoid(acc).astype(BF16)


def _in_proj(x, g, w_t, w_n, qg_t, kg, *, tm):
    t, d = x.shape
    tn = PROJ_TN
    n_q, n_k, n_v, n_c, n_g = _K0 - _Q0, _V0 - _K0, _C0 - _V0, _G0 - _C0, _NJ - _G0

    def wt_idx(i, j):
        return (jnp.where(j < _K0, j, jnp.clip(j - n_k, n_q - 1, n_q + n_v - 1)), 0)

    def wn_idx(i, j):
        return (0, jnp.where(j < _V0, jnp.clip(j - _K0, 0, n_k - 1),
                             jnp.clip(j - _K0 - n_v, n_k - 1, n_k + n_c + n_g - 1)))

    def cols(j0, n):
        return pl.BlockSpec((tm, tn), lambda i, j: (i, jnp.clip(j - j0, 0, n - 1)))

    def rows(j0, n):
        return pl.BlockSpec((tn, tm), lambda i, j: (jnp.clip(j - j0, 0, n - 1), i))

    return pl.pallas_call(
        _proj_body,
        grid=(t // tm, _NJ),
        in_specs=[
            pl.BlockSpec((tm, d), lambda i, j: (i, 0)),
            pl.BlockSpec((1, d), lambda i, j: (0, 0)),
            pl.BlockSpec((tn, d), wt_idx),
            pl.BlockSpec((d, tn), wn_idx),
            pl.BlockSpec((LANES, tm), lambda i, j: (0, 0)),
            pl.BlockSpec((1, LANES), lambda i, j: (0, 0)),
        ],
        out_specs=[rows(_Q0, n_q), cols(_K0, n_k), rows(_V0, n_v), cols(_C0, n_c), cols(_G0, n_g)],
        out_shape=[
            jax.ShapeDtypeStruct((n_q * tn, t), BF16),
            jax.ShapeDtypeStruct((t, n_k * tn), BF16),
            jax.ShapeDtypeStruct((n_v * tn, t), BF16),
            jax.ShapeDtypeStruct((t, n_c * tn), F32),
            jax.ShapeDtypeStruct((t, n_g * tn), BF16),
        ],
        scratch_shapes=[pltpu.VMEM((tm, d), BF16)],
        compiler_params=_cparams(("parallel", "arbitrary")),
        name="in_proj",
    )(x, g, w_t, w_n, qg_t, kg)


def _mla_proj_body(c_ref, qa_g_ref, kva_g_ref, wqt_ref, wkn_ref, wvt_ref,
                   gqn_ref, gqr_ref, gqt_ref, gk_nope_ref, gk_rope_ref, gk_rot_ref,
                   cos_ref, sin_ref, cost_ref, sint_ref,
                   qnt_ref, qrt_ref, kn_ref, kr_ref, vt_ref):
    c = c_ref[...]
    o_kv = MLA_Q_RANK
    o_kr = MLA_Q_RANK + MLA_KV_RANK
    cqn = _rms(c[:, :o_kv], qa_g_ref[...]).astype(BF16)
    ckvn = _rms(c[:, o_kv:o_kr], kva_g_ref[...]).astype(BF16)
    k_rope = c[:, o_kr:o_kr + LANES]
    k_rot = c[:, o_kr + LANES:o_kr + 2 * LANES]
    first = lax.broadcasted_iota(jnp.int32, (1, LANES), 1) < MLA_ROPE_DIM
    k_roped = k_rope * gk_rope_ref[...] * cos_ref[...] + k_rot * gk_rot_ref[...] * sin_ref[...]
    k_rope_sq = 0.5 * jnp.sum(k_rope * k_rope, axis=-1, keepdims=True)
    inv_d = 1.0 / MLA_QK_DIM
    cos_t = cost_ref[...]
    sin_t = sint_ref[...]
    half = MLA_ROPE_DIM

    for p in range(MLA_HEADS // 2):
        qraw = _nt_dot(wqt_ref[p * 4 * LANES:(p + 1) * 4 * LANES, :], cqn)
        na, nb = qraw[:LANES], qraw[LANES:2 * LANES]
        rp, rt = qraw[2 * LANES:3 * LANES], qraw[3 * LANES:]
        rp2 = rp * rp
        sa = jnp.sum(na * na, axis=0, keepdims=True) + jnp.sum(rp2[:half], axis=0, keepdims=True)
        sb = jnp.sum(nb * nb, axis=0, keepdims=True) + jnp.sum(rp2[half:], axis=0, keepdims=True)
        ra = lax.rsqrt(sa * inv_d + EPS)
        rb = lax.rsqrt(sb * inv_d + EPS)
        ra_rows = slice((2 * p) * LANES, (2 * p + 1) * LANES)
        rb_rows = slice((2 * p + 1) * LANES, (2 * p + 2) * LANES)
        qnt_ref[ra_rows, :] = (na * ra * gqn_ref[...]).astype(BF16)
        qnt_ref[rb_rows, :] = (nb * rb * gqn_ref[...]).astype(BF16)
        roped = rp * gqr_ref[...] * cos_t + rt * gqt_ref[...] * sin_t
        qrt_ref[p * LANES:(p + 1) * LANES, :] = jnp.concatenate(
            [roped[:half] * ra, roped[half:] * rb], axis=0).astype(BF16)

        knraw = jnp.dot(ckvn, wkn_ref[:, p * 2 * LANES:(p + 1) * 2 * LANES], preferred_element_type=F32)
        kna, knb = knraw[:, :LANES], knraw[:, LANES:]
        rka = lax.rsqrt((jnp.sum(kna * kna, axis=-1, keepdims=True) + k_rope_sq) * inv_d + EPS)
        rkb = lax.rsqrt((jnp.sum(knb * knb, axis=-1, keepdims=True) + k_rope_sq) * inv_d + EPS)
        kn_ref[:, ra_rows] = (kna * rka * gk_nope_ref[...]).astype(BF16)
        kn_ref[:, rb_rows] = (knb * rkb * gk_nope_ref[...]).astype(BF16)
        kr_ref[:, ra_rows] = jnp.where(first, k_roped * rka, 0.0).astype(BF16)
        kr_ref[:, rb_rows] = jnp.where(first, 0.0, k_roped * rkb).astype(BF16)

        vt_ref[p * 2 * LANES:(p + 1) * 2 * LANES, :] = _nt_dot(
            wvt_ref[p * 2 * LANES:(p + 1) * 2 * LANES, :], ckvn).astype(BF16)


def _mla_proj(c, qa_g, kva_g, wq_t, wkn, wv_t, q_gains_t, k_gains, cos2, sin2, cos_t, sin_t, *, tm, seq):
    t = c.shape[0]
    n_pos = seq // tm
    const = lambda a: pl.BlockSpec(a.shape, lambda i: (0, 0))
    hw = MLA_HEADS * LANES
    tok = lambda w: pl.BlockSpec((tm, w), lambda i: (i, 0))
    tok_t = lambda w: pl.BlockSpec((w, tm), lambda i: (0, i))
    return pl.pallas_call(
        _mla_proj_body,
        grid=(t // tm,),
        in_specs=[tok(c.shape[1]), const(qa_g), const(kva_g), const(wq_t), const(wkn), const(wv_t)]
        + [const(a) for a in q_gains_t] + [const(a) for a in k_gains] + [
            pl.BlockSpec((tm, LANES), lambda i: (i % n_pos, 0)),
            pl.BlockSpec((tm, LANES), lambda i: (i % n_pos, 0)),
            pl.BlockSpec((LANES, tm), lambda i: (0, i % n_pos)),
            pl.BlockSpec((LANES, tm), lambda i: (0, i % n_pos)),
        ],
        out_specs=[tok_t(hw), tok_t(hw // 2), tok(hw), tok(hw), tok_t(hw)],
        out_shape=[
            jax.ShapeDtypeStruct((hw, t), BF16),
            jax.ShapeDtypeStruct((hw // 2, t), BF16),
            jax.ShapeDtypeStruct((t, hw), BF16),
            jax.ShapeDtypeStruct((t, hw), BF16),
            jax.ShapeDtypeStruct((hw, t), BF16),
        ],
        compiler_params=_cparams(("parallel",)),
        name="mla_proj",
    )(c, qa_g, kva_g, wq_t, wkn, wv_t, *q_gains_t, *k_gains, cos2, sin2, cos_t, sin_t)


def _score_pass(k_tile, q_t, add, keep, offs, s_ref, tk):
    tq = q_t.shape[1]
    m8 = None
    for n in range(tk // MXU_WIDTH):
        u = jnp.dot(k_tile(n), q_t, preferred_element_type=F32)
        if add is not None:
            u = u + add(n)
        if keep is not None:
            u = jnp.where(keep(n), u, MASK_VALUE)
        s_ref[n * MXU_WIDTH:(n + 1) * MXU_WIDTH, :] = u
        t = jnp.max(u.reshape(MXU_WIDTH // SUBLANES, SUBLANES, tq), axis=0)
        if offs is not None:
            t = t + offs(n)
        m8 = t if m8 is None else jnp.maximum(m8, t)
    return jnp.max(m8, axis=0, keepdims=True)


def _prob_pass(s_ref, p_ref, m, l, m_blk, offs, tk):
    tq = s_ref.shape[1]
    m_new = jnp.maximum(m, m_blk)
    alpha = jnp.exp2(m - m_new)
    ps8 = None
    for n in range(tk // MXU_WIDTH):
        rows = slice(n * MXU_WIDTH, (n + 1) * MXU_WIDTH)
        ref = m_new if offs is None else m_new - offs(n)
        p = jnp.exp2(s_ref[rows, :] - ref)
        t = jnp.sum(p.reshape(MXU_WIDTH // SUBLANES, SUBLANES, tq), axis=0)
        ps8 = t if ps8 is None else ps8 + t
        p_ref[rows, :] = p.astype(BF16)
    l_new = alpha * l + jnp.sum(ps8, axis=0, keepdims=True)
    return m_new, l_new, alpha


def _pipelined_key_blocks(n_full, score, prob_pv, finalize, tq):
    neg = jnp.full((1, tq), MASK_VALUE, F32)
    zero = jnp.zeros((1, tq), F32)
    init = (neg, zero, neg, zero)

    @pl.when(n_full == 0)
    def _():
        finalize(prob_pv(0, True, init, score(0, True)))

    @pl.when(n_full > 0)
    def _():
        def body(j, carry):
            state, mx = carry
            state = prob_pv(j, False, state, mx)
            return state, score(j + 1, False)

        state, mx = lax.fori_loop(0, n_full - 1, body, (init, score(0, False)))
        state = prob_pv(n_full - 1, False, state, mx)
        mx_last = score(n_full, True)
        finalize(prob_pv(n_full, True, state, mx_last))


def _key_chunk_mask(q0, k0, tq):
    key = lax.broadcasted_iota(jnp.int32, (MXU_WIDTH, tq), 0)
    qry = lax.broadcasted_iota(jnp.int32, (MXU_WIDTH, tq), 1)
    return ((k0 + key) >> CHUNK_SHIFT) <= ((q0 + qry) >> CHUNK_SHIFT)


def _da_attn_body(slopes_ref, lq1_ref, lk1_ref, lq2_ref, lk2_ref, sg_ref,
                  qt_ref, k_ref, vt_ref, o_ref, acc1_ref, acc2_ref, s_ref, p_ref,
                  *, tq, tk, lambda_init):
    h = pl.program_id(1)
    qi = pl.program_id(2)
    slope = slopes_ref[h]
    q1_t = qt_ref[:DA_HEAD_DIM, :]
    q2_t = qt_ref[DA_HEAD_DIM:, :]
    q0 = qi * tq
    n_full = qi // (tk // tq)
    key = lax.broadcasted_iota(jnp.int32, (MXU_WIDTH, tq), 0)
    qry = lax.broadcasted_iota(jnp.int32, (MXU_WIDTH, tq), 1)
    key_bias = slope * key.astype(F32)
    acc1_ref[...] = jnp.zeros_like(acc1_ref)
    acc2_ref[...] = jnp.zeros_like(acc2_ref)

    def tile_offset(j):
        return lambda n: slope * (j * tk + n * MXU_WIDTH - q0).astype(F32)

    def score(j, last):
        k0 = pl.multiple_of(j * tk, tk)

        def k_tile(lo):
            return lambda n: k_ref[0, pl.ds(pl.multiple_of(k0 + n * MXU_WIDTH, MXU_WIDTH), MXU_WIDTH),
                                   lo:lo + DA_HEAD_DIM]

        if last:
            def add(n):
                dist = jnp.abs((q0 - k0 - n * MXU_WIDTH) + qry - key)
                return slope * (qry - dist).astype(F32)
            keep = lambda n: _key_chunk_mask(q0, k0 + n * MXU_WIDTH, tq)
            offs = None
        else:
            add = lambda n: key_bias
            keep = None
            offs = tile_offset(j)
        return (_score_pass(k_tile(0), q1_t, add, keep, offs, s_ref.at[0], tk),
                _score_pass(k_tile(DA_HEAD_DIM), q2_t, add, keep, offs, s_ref.at[1], tk))

    def prob_pv(j, last, state, mx):
        m1, l1, m2, l2 = state
        offs = None if last else tile_offset(j)
        m1, l1, a1 = _prob_pass(s_ref.at[0], p_ref.at[0], m1, l1, mx[0], offs, tk)
        m2, l2, a2 = _prob_pass(s_ref.at[1], p_ref.at[1], m2, l2, mx[1], offs, tk)
        vb = vt_ref[:, pl.ds(pl.multiple_of(j * tk, tk), tk)]
        acc1_ref[...] = a1 * acc1_ref[...] + jnp.dot(vb, p_ref[0], preferred_element_type=F32)
        acc2_ref[...] = a2 * acc2_ref[...] + jnp.dot(vb, p_ref[1], preferred_element_type=F32)
        return m1, l1, m2, l2

    def finalize(state):
        _, l1, _, l2 = state
        lam = (jnp.exp(jnp.sum(lq1_ref[...] * lk1_ref[...], axis=-1, keepdims=True))
               - jnp.exp(jnp.sum(lq2_ref[...] * lk2_ref[...], axis=-1, keepdims=True))
               + lambda_init)
        o = acc1_ref[...] * (1.0 / l1) - lam * (acc2_ref[...] * (1.0 / l2))
        ms = jnp.mean(o * o, axis=0, keepdims=True)
        y = o * lax.rsqrt(ms + EPS) * sg_ref[...] * (1.0 - lambda_init)
        o_ref[0] = y.T.astype(BF16)

    _pipelined_key_blocks(n_full, score, prob_pv, finalize, tq)


def _da_attn(slopes, lams, sg_t, q_t, k, v_t, *, tq, tk, lambda_init):
    b, s, _ = k.shape
    w = DA_V_DIM
    nq = s // tq
    vec = pl.BlockSpec((1, DA_HEAD_DIM), lambda bi, h, i: (0, 0))
    return pl.pallas_call(
        functools.partial(_da_attn_body, tq=tq, tk=tk, lambda_init=lambda_init),
        grid=(b, DA_HEADS, nq),
        in_specs=[
            pl.BlockSpec(memory_space=pltpu.SMEM),
            vec, vec, vec, vec,
            pl.BlockSpec((w, tq), lambda bi, h, i: (0, 0)),
            pl.BlockSpec((w, tq), lambda bi, h, i: (h, bi * nq + i)),
            pl.BlockSpec((1, s, w), lambda bi, h, i: (bi, 0, h)),
            pl.BlockSpec((w, s), lambda bi, h, i: (h, bi)),
        ],
        out_specs=pl.BlockSpec((1, tq, w), lambda bi, h, i: (bi, i, h)),
        out_shape=jax.ShapeDtypeStruct((b, s, DA_HEADS * w), BF16),
        scratch_shapes=[pltpu.VMEM((w, tq), F32), pltpu.VMEM((w, tq), F32),
                        pltpu.VMEM((2, tk, tq), F32), pltpu.VMEM((2, tk, tq), BF16)],
        compiler_params=_cparams(("parallel", "parallel", "arbitrary")),
        name="da_attn",
    )(slopes, *lams, sg_t, q_t, k, v_t)


def _mla_attn_body(qnt_ref, qrt_ref, kn_ref, kr_ref, vt_ref, o_ref, acc_a_ref, acc_b_ref,
                   s_ref, p_ref, *, tq, tk):
    qi = pl.program_id(2)
    q0 = qi * tq
    n_full = qi // (tk // tq)
    qr_t = qrt_ref[...]
    qa_t = jnp.concatenate([qnt_ref[:LANES, :], qr_t], axis=0)
    qb_t = jnp.concatenate([qnt_ref[LANES:, :], qr_t], axis=0)
    acc_a_ref[...] = jnp.zeros_like(acc_a_ref)
    acc_b_ref[...] = jnp.zeros_like(acc_b_ref)

    def score(j, last):
        k0 = pl.multiple_of(j * tk, tk)

        def k_tile(lo):
            def tile(n):
                rows = pl.ds(pl.multiple_of(k0 + n * MXU_WIDTH, MXU_WIDTH), MXU_WIDTH)
                return jnp.concatenate([kn_ref[0, rows, lo:lo + LANES],
                                        kr_ref[0, rows, lo:lo + LANES]], axis=-1)
            return tile

        keep = (lambda n: _key_chunk_mask(q0, k0 + n * MXU_WIDTH, tq)) if last else None
        return (_score_pass(k_tile(0), qa_t, None, keep, None, s_ref.at[0], tk),
                _score_pass(k_tile(LANES), qb_t, None, keep, None, s_ref.at[1], tk))

    def prob_pv(j, last, state, mx):
        ma, la, mb, lb = state
        ma, la, aa = _prob_pass(s_ref.at[0], p_ref.at[0], ma, la, mx[0], None, tk)
        mb, lb, ab = _prob_pass(s_ref.at[1], p_ref.at[1], mb, lb, mx[1], None, tk)
        cols = pl.ds(pl.multiple_of(j * tk, tk), tk)
        acc_a_ref[...] = aa * acc_a_ref[...] + jnp.dot(vt_ref[:LANES, cols], p_ref[0],
                                                      preferred_element_type=F32)
        acc_b_ref[...] = ab * acc_b_ref[...] + jnp.dot(vt_ref[LANES:, cols], p_ref[1],
                                                      preferred_element_type=F32)
        return ma, la, mb, lb

    def finalize(state):
        _, la, _, lb = state
        o_t = jnp.concatenate([acc_a_ref[...] * (1.0 / la), acc_b_ref[...] * (1.0 / lb)], axis=0)
        o_ref[0] = o_t.T.astype(BF16)

    _pipelined_key_blocks(n_full, score, prob_pv, finalize, tq)


def _mla_attn(qn_t, qr_t, kn, kr, v_t, *, tq, tk):
    b, s, _ = kn.shape
    w = 2 * LANES
    nq = s // tq
    kv_spec = pl.BlockSpec((1, s, w), lambda bi, h, i: (bi, 0, h))
    return pl.pallas_call(
        functools.partial(_mla_attn_body, tq=tq, tk=tk),
        grid=(b, MLA_HEADS // 2, nq),
        in_specs=[
            pl.BlockSpec((w, tq), lambda bi, h, i: (h, bi * nq + i)),
            pl.BlockSpec((LANES, tq), lambda bi, h, i: (h, bi * nq + i)),
            kv_spec, kv_spec,
            pl.BlockSpec((w, s), lambda bi, h, i: (h, bi)),
        ],
        out_specs=pl.BlockSpec((1, tq, w), lambda bi, h, i: (bi, i, h)),
        out_shape=jax.ShapeDtypeStruct((b, s, MLA_HEADS * LANES), BF16),
        scratch_shapes=[pltpu.VMEM((LANES, tq), F32), pltpu.VMEM((LANES, tq), F32),
                        pltpu.VMEM((2, tk, tq), F32), pltpu.VMEM((2, tk, tq), BF16)],
        compiler_params=_cparams(("parallel", "parallel", "arbitrary")),
        name="mla_attn",
    )(qn_t, qr_t, kn, kr, v_t)


def _merge_body(x_ref, ya_ref, yb_ref, gate_a_ref, gate_b_ref, wa_ref, wb_ref, wo_ref, o_ref):
    @pl.when(pl.program_id(1) == 0)
    def _():
        o_ref[...] = x_ref[...]

    a = jnp.dot(ya_ref[...], wa_ref[...], preferred_element_type=F32)
    b = jnp.dot(yb_ref[...], wb_ref[...], preferred_element_type=F32)
    m = (gate_a_ref[...].astype(F32) * a + gate_b_ref[...].astype(F32) * b).astype(BF16)
    o_ref[...] += jnp.dot(m, wo_ref[...], preferred_element_type=F32)


def _merge(x, ya, yb, gates, wa, wb, wo, *, tm, tn):
    t, d = x.shape
    nj = d // tn
    tok = pl.BlockSpec((tm, d), lambda i, j: (i, 0))
    return pl.pallas_call(
        _merge_body,
        grid=(t // tm, nj),
        in_specs=[
            tok, tok, tok,
            pl.BlockSpec((tm, tn), lambda i, j: (i, j)),
            pl.BlockSpec((tm, tn), lambda i, j: (i, j + nj)),
            pl.BlockSpec((d, tn), lambda i, j: (0, j)),
            pl.BlockSpec((d, tn), lambda i, j: (0, j)),
            pl.BlockSpec((tn, d), lambda i, j: (j, 0)),
        ],
        out_specs=tok,
        out_shape=jax.ShapeDtypeStruct((t, d), F32),
        compiler_params=_cparams(("parallel", "arbitrary")),
        name="merge",
    )(x, ya, yb, gates, gates, wa, wb, wo)


def _rot_cols(w):
    half = w.shape[-1] // 2
    return jnp.concatenate([-w[..., half:], w[..., :half]], axis=-1)


def _swap_halves(g):
    half = g.shape[-1] // 2
    return jnp.concatenate([g[..., half:], g[..., :half]], axis=-1)


def _pick_tile(n, pref):
    return pref if n % pref == 0 else n


def kernel(x, ffn1_norm_g, ffn1_w1, ffn1_w3, ffn1_w2, mix_norm_g, w_in, da_q_norm_g, da_k_norm_g, da_lambda_q1, da_lambda_k1, da_lambda_q2, da_lambda_k2, da_subln_g, mla_q_a_norm_g, mla_w_qb, mla_kv_a_norm_g, mla_w_kvb, mla_q_norm_g, mla_k_norm_g, w_branch_a, w_branch_b, w_out, ffn2_norm_g, ffn2_w1, ffn2_w3, ffn2_w2):
    bsz, seq, d = x.shape
    t = bsz * seq
    depth = ffn1_norm_g.shape[0]
    tm = _pick_tile(t, 512)
    tm_proj = _pick_tile(t, 1024)
    tm_mla = _pick_tile(seq, 256)
    tq = _pick_tile(seq, 512)
    tk = _pick_tile(seq, 1024)
    row = lambda v: v.reshape(1, -1).astype(F32)
    col_tile = lambda v, n: jnp.broadcast_to(v.astype(F32)[:, None], (v.shape[0], n))

    inv = ROPE_THETA ** (-jnp.arange(0, MLA_ROPE_DIM, 2, dtype=F32) / MLA_ROPE_DIM)
    ang = jnp.arange(seq, dtype=F32)[:, None] * inv[None, :]
    cos2 = jnp.tile(jnp.cos(ang), (1, 4))
    sin2 = jnp.tile(jnp.sin(ang), (1, 4))
    cos_t, sin_t = cos2.T, sin2.T
    slopes = (2.0 ** (-8.0 * jnp.arange(1, DA_HEADS + 1, dtype=F32) / DA_HEADS)) * LOG2E

    xt = x.reshape(t, d)
    for l in range(depth):
        lambda_init = 0.8 - 0.6 * math.exp(-0.3 * l)

        xt = _ffn(xt, row(ffn1_norm_g[l]), ffn1_w1[l].astype(BF16), ffn1_w3[l].astype(BF16),
                  (0.5 * ffn1_w2[l]).astype(BF16), tm=tm, tf=512)

        w = w_in[l]
        o = [0]
        for n in (2 * DA_HEADS * DA_HEAD_DIM, 2 * DA_HEADS * DA_HEAD_DIM, DA_HEADS * DA_V_DIM,
                  MLA_Q_RANK, MLA_KV_RANK, MLA_ROPE_DIM, d, d):
            o.append(o[-1] + n)
        w_kr = w[:, o[5]:o[6]]
        w_kr_rot = _rot_cols(w_kr)
        w_t = jnp.concatenate([w[:, o[0]:o[1]], w[:, o[2]:o[3]]], axis=1).T.astype(BF16)
        w_n = jnp.concatenate(
            [w[:, o[1]:o[2]], w[:, o[3]:o[5]], w_kr, w_kr, w_kr_rot, w_kr_rot, w[:, o[6]:]],
            axis=1).astype(BF16)
        q_scale = DA_HEAD_DIM ** -0.5 * LOG2E
        q_da_t, k_da, v_da_t, lat, gates = _in_proj(
            xt, row(mix_norm_g[l]), w_t, w_n, col_tile(da_q_norm_g[l] * q_scale, tm_proj),
            row(da_k_norm_g[l]), tm=tm_proj)

        wq = mla_w_qb[l].reshape(MLA_Q_RANK, MLA_HEADS, MLA_QK_DIM)
        wq_nope = wq[:, :, :MLA_NOPE_DIM].reshape(MLA_Q_RANK, MLA_HEADS // 2, 2 * LANES)
        wq_rope = wq[:, :, MLA_NOPE_DIM:]
        wq_rot = _rot_cols(wq_rope).reshape(MLA_Q_RANK, MLA_HEADS // 2, LANES)
        wq_rope = wq_rope.reshape(MLA_Q_RANK, MLA_HEADS // 2, LANES)
        wq_t = jnp.concatenate([wq_nope, wq_rope, wq_rot], axis=-1).reshape(
            MLA_Q_RANK, MLA_HEADS * 2 * LANES).T.astype(BF16)
        wkv = mla_w_kvb[l].reshape(MLA_KV_RANK, MLA_HEADS, MLA_NOPE_DIM + MLA_V_DIM)
        wkn = wkv[:, :, :MLA_NOPE_DIM].reshape(MLA_KV_RANK, MLA_HEADS * MLA_NOPE_DIM).astype(BF16)
        wv_t = wkv[:, :, MLA_NOPE_DIM:].reshape(MLA_KV_RANK, MLA_HEADS * MLA_V_DIM).T.astype(BF16)
        mla_scale = MLA_QK_DIM ** -0.5 * LOG2E
        gq, gk = mla_q_norm_g[l], mla_k_norm_g[l]
        q_gains_t = (
            col_tile(gq[:MLA_NOPE_DIM] * mla_scale, tm_mla),
            col_tile(jnp.tile(gq[MLA_NOPE_DIM:], 2) * mla_scale, tm_mla),
            col_tile(jnp.tile(_swap_halves(gq[MLA_NOPE_DIM:]), 2) * mla_scale, tm_mla),
        )
        k_gains = (
            row(gk[:MLA_NOPE_DIM]),
            row(jnp.tile(gk[MLA_NOPE_DIM:], 2)),
            row(jnp.tile(_swap_halves(gk[MLA_NOPE_DIM:]), 2)),
        )
        qn_t, qr_t, kn, kr, v_mla_t = _mla_proj(
            lat, row(mla_q_a_norm_g[l]), row(mla_kv_a_norm_g[l]), wq_t, wkn, wv_t,
            q_gains_t, k_gains, cos2, sin2, cos_t, sin_t, tm=tm_mla, seq=seq)

        b3 = lambda a: a.reshape(bsz, seq, a.shape[-1])
        lams = (row(da_lambda_q1[l]), row(da_lambda_k1[l]), row(da_lambda_q2[l]), row(da_lambda_k2[l]))
        y_a = _da_attn(slopes, lams, col_tile(da_subln_g[l], tq), q_da_t, b3(k_da), v_da_t,
                       tq=tq, tk=tk, lambda_init=lambda_init)
        y_b = _mla_attn(qn_t, qr_t, b3(kn), b3(kr), v_mla_t, tq=tq, tk=tk)

        xt = _merge(xt, y_a.reshape(t, -1), y_b.reshape(t, -1), gates,
                    w_branch_a[l].astype(BF16), w_branch_b[l].astype(BF16), w_out[l].astype(BF16),
                    tm=tm, tn=512)

        xt = _ffn(xt, row(ffn2_norm_g[l]), ffn2_w1[l].astype(BF16), ffn2_w3[l].astype(BF16),
                  (0.5 * ffn2_w2[l]).astype(BF16), tm=tm, tf=512)
    return xt.reshape(bsz, seq, d)
```

```python
import functools
import math

import jax
import jax.numpy as jnp
from jax import lax
from jax.experimental import pallas as pl
from jax.experimental.pallas import tpu as pltpu

F32 = jnp.float32
BF16 = jnp.bfloat16

EPS = 1e-6
CHUNK = 64
CHUNK_SHIFT = CHUNK.bit_length() - 1
DA_HEADS = 8
DA_HEAD_DIM = 128
DA_V_DIM = 2 * DA_HEAD_DIM
MLA_HEADS = 16
MLA_Q_RANK = 768
MLA_KV_RANK = 512
MLA_NOPE_DIM = 128
MLA_ROPE_DIM = 64
MLA_V_DIM = 128
MLA_QK_DIM = MLA_NOPE_DIM + MLA_ROPE_DIM
ROPE_THETA = 10000.0
LOG2E = math.log2(math.e)
MASK_VALUE = -1e30

LANES = 128
SUBLANES = 8
MXU_WIDTH = 256
V7X_VMEM_BYTES = 64 * 1024 * 1024
VMEM_LIMIT = V7X_VMEM_BYTES - 8 * 1024 * 1024


def _cparams(sem):
    return pltpu.CompilerParams(dimension_semantics=sem, vmem_limit_bytes=VMEM_LIMIT)


def _rms(x, g):
    ms = jnp.mean(x * x, axis=-1, keepdims=True)
    return x * lax.rsqrt(ms + EPS) * g


def _nt_dot(a, b):
    return lax.dot_general(a, b, (((1,), (1,)), ((), ())), preferred_element_type=F32)


def _ffn_body(x_ref, g_ref, w1_ref, w3_ref, w2_ref, o_ref, h_ref):
    @pl.when(pl.program_id(1) == 0)
    def _():
        x = x_ref[...]
        h_ref[...] = _rms(x, g_ref[...]).astype(BF16)
        o_ref[...] = x

    h = h_ref[...]
    a = jnp.dot(h, w1_ref[...], preferred_element_type=F32)
    b = jnp.dot(h, w3_ref[...], preferred_element_type=F32)
    u = (a * jax.nn.sigmoid(a) * b).astype(BF16)
    o_ref[...] += jnp.dot(u, w2_ref[...], preferred_element_type=F32)


def _ffn(x, g, w1, w3, w2_half, *, tm, tf):
    t, d = x.shape
    f = w1.shape[1]
    return pl.pallas_call(
        _ffn_body,
        grid=(t // tm, f // tf),
        in_specs=[
            pl.BlockSpec((tm, d), lambda i, j: (i, 0)),
            pl.BlockSpec((1, d), lambda i, j: (0, 0)),
            pl.BlockSpec((d, tf), lambda i, j: (0, j)),
            pl.BlockSpec((d, tf), lambda i, j: (0, j)),
            pl.BlockSpec((tf, d), lambda i, j: (j, 0)),
        ],
        out_specs=pl.BlockSpec((tm, d), lambda i, j: (i, 0)),
        out_shape=jax.ShapeDtypeStruct((t, d), F32),
        scratch_shapes=[pltpu.VMEM((tm, d), BF16)],
        compiler_params=_cparams(("parallel", "arbitrary")),
        name="ffn",
    )(x, g, w1, w3, w2_half)


PROJ_TN = 512
_Q0, _K0, _V0, _C0, _G0, _NJ = 0, 4, 8, 12, 15, 23


def _head_norm(acc, g):
    parts = []
    for c in range(acc.shape[1] // LANES):
        blk = acc[:, c * LANES:(c + 1) * LANES]
        ms = jnp.mean(blk * blk, axis=-1, keepdims=True)
        parts.append(blk * lax.rsqrt(ms + EPS) * g)
    return jnp.concatenate(parts, axis=-1)


def _head_norm_t(acc_t, g_t):
    parts = []
    for c in range(acc_t.shape[0] // LANES):
        blk = acc_t[c * LANES:(c + 1) * LANES, :]
        ms = jnp.mean(blk * blk, axis=0, keepdims=True)
        parts.append(blk * lax.rsqrt(ms + EPS) * g_t)
    return jnp.concatenate(parts, axis=0)


def _proj_body(x_ref, g_ref, wt_ref, wn_ref, qg_ref, kg_ref,
               qt_ref, k_ref, vt_ref, c_ref, gate_ref, h_ref):
    j = pl.program_id(1)

    @pl.when(j == 0)
    def _():
        h_ref[...] = _rms(x_ref[...], g_ref[...]).astype(BF16)

    @pl.when(j < _K0)
    def _():
        qt_ref[...] = _head_norm_t(_nt_dot(wt_ref[...], h_ref[...]), qg_ref[...]).astype(BF16)

    @pl.when((j >= _K0) & (j < _V0))
    def _():
        acc = jnp.dot(h_ref[...], wn_ref[...], preferred_element_type=F32)
        k_ref[...] = _head_norm(acc, kg_ref[...]).astype(BF16)

    @pl.when((j >= _V0) & (j < _C0))
    def _():
        vt_ref[...] = _nt_dot(wt_ref[...], h_ref[...]).astype(BF16)

    @pl.when((j >= _C0) & (j < _G0))
    def _():
        c_ref[...] = jnp.dot(h_ref[...], wn_ref[...], preferred_element_type=F32)

    @pl.when(j >= _G0)
    def _():
        acc = jnp.dot(h_ref[...], wn_ref[...], preferred_element_type=F32)
        gate_ref[...] = jax.nn.sigmoid(acc).astype(BF16)


def _in_proj(x, g, w_t, w_n, qg_t, kg, *, tm):
    t, d = x.shape
    tn = PROJ_TN
    n_q, n_k, n_v, n_c, n_g = _K0 - _Q0, _V0 - _K0, _C0 - _V0, _G0 - _C0, _NJ - _G0

    def wt_idx(i, j):
        return (jnp.where(j < _K0, j, jnp.clip(j - n_k, n_q - 1, n_q + n_v - 1)), 0)

    def wn_idx(i, j):
        return (0, jnp.where(j < _V0, jnp.clip(j - _K0, 0, n_k - 1),
                             jnp.clip(j - _K0 - n_v, n_k - 1, n_k + n_c + n_g - 1)))

    def cols(j0, n):
        return pl.BlockSpec((tm, tn), lambda i, j: (i, jnp.clip(j - j0, 0, n - 1)))

    def rows(j0, n):
        return pl.BlockSpec((tn, tm), lambda i, j: (jnp.clip(j - j0, 0, n - 1), i))

    return pl.pallas_call(
        _proj_body,
        grid=(t // tm, _NJ),
        in_specs=[
            pl.BlockSpec((tm, d), lambda i, j: (i, 0)),
            pl.BlockSpec((1, d), lambda i, j: (0, 0)),
            pl.BlockSpec((tn, d), wt_idx),
            pl.BlockSpec((d, tn), wn_idx),
            pl.BlockSpec((LANES, tm), lambda i, j: (0, 0)),
            pl.BlockSpec((1, LANES), lambda i, j: (0, 0)),
        ],
        out_specs=[rows(_Q0, n_q), cols(_K0, n_k), rows(_V0, n_v), cols(_C0, n_c), cols(_G0, n_g)],
        out_shape=[
            jax.ShapeDtypeStruct((n_q * tn, t), BF16),
            jax.ShapeDtypeStruct((t, n_k * tn), BF16),
            jax.ShapeDtypeStruct((n_v * tn, t), BF16),
            jax.ShapeDtypeStruct((t, n_c * tn), F32),
            jax.ShapeDtypeStruct((t, n_g * tn), BF16),
        ],
        scratch_shapes=[pltpu.VMEM((tm, d), BF16)],
        compiler_params=_cparams(("parallel", "arbitrary")),
        name="in_proj",
    )(x, g, w_t, w_n, qg_t, kg)


def _mla_proj_body(c_ref, qa_g_ref, kva_g_ref, wqt_ref, wkn_ref, wvt_ref,
                   gqn_ref, gqr_ref, gqt_ref, gk_nope_ref, gk_rope_ref, gk_rot_ref,
                   cos_ref, sin_ref, cost_ref, sint_ref,
                   qnt_ref, qrt_ref, kn_ref, kr_ref, vt_ref):
    c = c_ref[...]
    o_kv = MLA_Q_RANK
    o_kr = MLA_Q_RANK + MLA_KV_RANK
    cqn = _rms(c[:, :o_kv], qa_g_ref[...]).astype(BF16)
    ckvn = _rms(c[:, o_kv:o_kr], kva_g_ref[...]).astype(BF16)
    k_rope = c[:, o_kr:o_kr + LANES]
    k_rot = c[:, o_kr + LANES:o_kr + 2 * LANES]
    first = lax.broadcasted_iota(jnp.int32, (1, LANES), 1) < MLA_ROPE_DIM
    k_roped = k_rope * gk_rope_ref[...] * cos_ref[...] + k_rot * gk_rot_ref[...] * sin_ref[...]
    k_rope_sq = 0.5 * jnp.sum(k_rope * k_rope, axis=-1, keepdims=True)
    inv_d = 1.0 / MLA_QK_DIM
    cos_t = cost_ref[...]
    sin_t = sint_ref[...]
    half = MLA_ROPE_DIM

    for p in range(MLA_HEADS // 2):
        qraw = _nt_dot(wqt_ref[p * 4 * LANES:(p + 1) * 4 * LANES, :], cqn)
        na, nb = qraw[:LANES], qraw[LANES:2 * LANES]
        rp, rt = qraw[2 * LANES:3 * LANES], qraw[3 * LANES:]
        rp2 = rp * rp
        sa = jnp.sum(na * na, axis=0, keepdims=True) + jnp.sum(rp2[:half], axis=0, keepdims=True)
        sb = jnp.sum(nb * nb, axis=0, keepdims=True) + jnp.sum(rp2[half:], axis=0, keepdims=True)
        ra = lax.rsqrt(sa * inv_d + EPS)
        rb = lax.rsqrt(sb * inv_d + EPS)
        ra_rows = slice((2 * p) * LANES, (2 * p + 1) * LANES)
        rb_rows = slice((2 * p + 1) * LANES, (2 * p + 2) * LANES)
        qnt_ref[ra_rows, :] = (na * ra * gqn_ref[...]).astype(BF16)
        qnt_ref[rb_rows, :] = (nb * rb * gqn_ref[...]).astype(BF16)
        roped = rp * gqr_ref[...] * cos_t + rt * gqt_ref[...] * sin_t
        qrt_ref[p * LANES:(p + 1) * LANES, :] = jnp.concatenate(
            [roped[:half] * ra, roped[half:] * rb], axis=0).astype(BF16)

        knraw = jnp.dot(ckvn, wkn_ref[:, p * 2 * LANES:(p + 1) * 2 * LANES], preferred_element_type=F32)
        kna, knb = knraw[:, :LANES], knraw[:, LANES:]
        rka = lax.rsqrt((jnp.sum(kna * kna, axis=-1, keepdims=True) + k_rope_sq) * inv_d + EPS)
        rkb = lax.rsqrt((jnp.sum(knb * knb, axis=-1, keepdims=True) + k_rope_sq) * inv_d + EPS)
        kn_ref[:, ra_rows] = (kna * rka * gk_nope_ref[...]).astype(BF16)
        kn_ref[:, rb_rows] = (knb * rkb * gk_nope_ref[...]).astype(BF16)
        kr_ref[:, ra_rows] = jnp.where(first, k_roped * rka, 0.0).astype(BF16)
        kr_ref[:, rb_rows] = jnp.where(first, 0.0, k_roped * rkb).astype(BF16)

        vt_ref[p * 2 * LANES:(p + 1) * 2 * LANES, :] = _nt_dot(
            wvt_ref[p * 2 * LANES:(p + 1) * 2 * LANES, :], ckvn).astype(BF16)


def _mla_proj(c, qa_g, kva_g, wq_t, wkn, wv_t, q_gains_t, k_gains, cos2, sin2, cos_t, sin_t, *, tm, seq):
    t = c.shape[0]
    n_pos = seq // tm
    const = lambda a: pl.BlockSpec(a.shape, lambda i: (0, 0))
    hw = MLA_HEADS * LANES
    tok = lambda w: pl.BlockSpec((tm, w), lambda i: (i, 0))
    tok_t = lambda w: pl.BlockSpec((w, tm), lambda i: (0, i))
    return pl.pallas_call(
        _mla_proj_body,
        grid=(t // tm,),
        in_specs=[tok(c.shape[1]), const(qa_g), const(kva_g), const(wq_t), const(wkn), const(wv_t)]
        + [const(a) for a in q_gains_t] + [const(a) for a in k_gains] + [
            pl.BlockSpec((tm, LANES), lambda i: (i % n_pos, 0)),
            pl.BlockSpec((tm, LANES), lambda i: (i % n_pos, 0)),
            pl.BlockSpec((LANES, tm), lambda i: (0, i % n_pos)),
            pl.BlockSpec((LANES, tm), lambda i: (0, i % n_pos)),
        ],
        out_specs=[tok_t(hw), tok_t(hw // 2), tok(hw), tok(hw), tok_t(hw)],
        out_shape=[
            jax.ShapeDtypeStruct((hw, t), BF16),
            jax.ShapeDtypeStruct((hw // 2, t), BF16),
            jax.ShapeDtypeStruct((t, hw), BF16),
            jax.ShapeDtypeStruct((t, hw), BF16),
            jax.ShapeDtypeStruct((hw, t), BF16),
        ],
        compiler_params=_cparams(("parallel",)),
        name="mla_proj",
    )(c, qa_g, kva_g, wq_t, wkn, wv_t, *q_gains_t, *k_gains, cos2, sin2, cos_t, sin_t)


def _score_pass(k_tile, q_t, tile_spec, tile_off, n_tiles, s_ref):
    tq = q_t.shape[1]
    m8 = None
    for n in range(n_tiles):
        add, keep = tile_spec(n)
        u = jnp.dot(k_tile(n), q_t, preferred_element_type=F32)
        if add is not None:
            u = u + add
        if keep is not None:
            u = jnp.where(keep, u, MASK_VALUE)
        s_ref[n * MXU_WIDTH:(n + 1) * MXU_WIDTH, :] = u
        t = jnp.max(u.reshape(MXU_WIDTH // SUBLANES, SUBLANES, tq), axis=0)
        off = tile_off(n)
        if off is not None:
            t = t + off
        m8 = t if m8 is None else jnp.maximum(m8, t)
    return jnp.max(m8, axis=0, keepdims=True)


def _prob_pass(s_ref, p_ref, m, l, m_blk, tile_off, n_tiles):
    tq = s_ref.shape[1]
    m_new = jnp.maximum(m, m_blk)
    alpha = jnp.exp2(m - m_new)
    ps8 = None
    for n in range(n_tiles):
        rows = slice(n * MXU_WIDTH, (n + 1) * MXU_WIDTH)
        off = tile_off(n)
        ref = m_new if off is None else m_new - off
        p = jnp.exp2(s_ref[rows, :] - ref)
        t = jnp.sum(p.reshape(MXU_WIDTH // SUBLANES, SUBLANES, tq), axis=0)
        ps8 = t if ps8 is None else ps8 + t
        p_ref[rows, :] = p.astype(BF16)
    l_new = alpha * l + jnp.sum(ps8, axis=0, keepdims=True)
    return m_new, l_new, alpha


def _pipelined_key_blocks(qi, n_var, score, prob_pv, finalize, tq):
    n_full = qi // n_var
    r_dyn = qi - n_full * n_var
    neg = jnp.full((1, tq), MASK_VALUE, F32)
    zero = jnp.zeros((1, tq), F32)
    init = (neg, zero, neg, zero)

    def tail(state, mx_prev):
        for r in range(n_var):
            @pl.when(r_dyn == r)
            def _(r=r):
                st = state if mx_prev is None else prob_pv(n_full - 1, None, state, mx_prev)
                finalize(prob_pv(n_full, r, st, score(n_full, r)))

    @pl.when(n_full == 0)
    def _():
        tail(init, None)

    @pl.when(n_full > 0)
    def _():
        def body(j, carry):
            state, mx = carry
            state = prob_pv(j, None, state, mx)
            return state, score(j + 1, None)

        state, mx = lax.fori_loop(0, n_full - 1, body, (init, score(0, None)))
        tail(state, mx)


def _block_tiles(r, tq, tk):
    per_q = tq // MXU_WIDTH
    if r is None:
        return tk // MXU_WIDTH, tk // MXU_WIDTH
    return r * per_q, (r + 1) * per_q


def _own_chunk_mask(key, qry, dq):
    return ((key - dq) >> CHUNK_SHIFT) <= (qry >> CHUNK_SHIFT)


def _da_attn_body(slopes_ref, lq1_ref, lk1_ref, lq2_ref, lk2_ref, sg_ref,
                  qt_ref, k_ref, vt_ref, o_ref, acc1_ref, acc2_ref, s_ref, p_ref,
                  *, tq, tk, lambda_init):
    h = pl.program_id(1)
    qi = pl.program_id(2)
    slope = slopes_ref[h]
    q1_t = qt_ref[:DA_HEAD_DIM, :]
    q2_t = qt_ref[DA_HEAD_DIM:, :]
    q0 = qi * tq
    key = lax.broadcasted_iota(jnp.int32, (MXU_WIDTH, tq), 0)
    qry = lax.broadcasted_iota(jnp.int32, (MXU_WIDTH, tq), 1)
    key_bias = slope * key.astype(F32)
    mirror_bias = slope * (2 * qry - key).astype(F32)
    acc1_ref[...] = jnp.zeros_like(acc1_ref)
    acc2_ref[...] = jnp.zeros_like(acc2_ref)

    def tile_plan(j, r):
        first_masked, n_tiles = _block_tiles(r, tq, tk)
        specs = {}

        def tile_spec(n):
            if n not in specs:
                if n < first_masked:
                    specs[n] = (key_bias, None)
                else:
                    dq = r * tq - n * MXU_WIDTH
                    bias = jnp.minimum(key_bias - slope * dq, mirror_bias + slope * dq)
                    specs[n] = (bias, _own_chunk_mask(key, qry, dq))
            return specs[n]

        def tile_off(n):
            if n >= first_masked:
                return None
            if r is None:
                return slope * (j * tk + n * MXU_WIDTH - q0).astype(F32)
            return slope * float(n * MXU_WIDTH - r * tq)

        return tile_spec, tile_off, n_tiles

    def score(j, r):
        k0 = pl.multiple_of(j * tk, tk)
        tile_spec, tile_off, n_tiles = tile_plan(j, r)

        def k_tile(lo):
            return lambda n: k_ref[0, pl.ds(pl.multiple_of(k0 + n * MXU_WIDTH, MXU_WIDTH), MXU_WIDTH),
                                   lo:lo + DA_HEAD_DIM]

        return (_score_pass(k_tile(0), q1_t, tile_spec, tile_off, n_tiles, s_ref.at[0]),
                _score_pass(k_tile(DA_HEAD_DIM), q2_t, tile_spec, tile_off, n_tiles, s_ref.at[1]))

    def prob_pv(j, r, state, mx):
        m1, l1, m2, l2 = state
        _, tile_off, n_tiles = tile_plan(j, r)
        m1, l1, a1 = _prob_pass(s_ref.at[0], p_ref.at[0], m1, l1, mx[0], tile_off, n_tiles)
        m2, l2, a2 = _prob_pass(s_ref.at[1], p_ref.at[1], m2, l2, mx[1], tile_off, n_tiles)
        keys = n_tiles * MXU_WIDTH
        vb = vt_ref[:, pl.ds(pl.multiple_of(j * tk, tk), keys)]
        acc1_ref[...] = a1 * acc1_ref[...] + jnp.dot(vb, p_ref[0, :keys, :], preferred_element_type=F32)
        acc2_ref[...] = a2 * acc2_ref[...] + jnp.dot(vb, p_ref[1, :keys, :], preferred_element_type=F32)
        return m1, l1, m2, l2

    def finalize(state):
        _, l1, _, l2 = state
        lam = (jnp.exp(jnp.sum(lq1_ref[...] * lk1_ref[...], axis=-1, keepdims=True))
               - jnp.exp(jnp.sum(lq2_ref[...] * lk2_ref[...], axis=-1, keepdims=True))
               + lambda_init)
        o = acc1_ref[...] * (1.0 / l1) - lam * (acc2_ref[...] * (1.0 / l2))
        ms = jnp.mean(o * o, axis=0, keepdims=True)
        y = o * lax.rsqrt(ms + EPS) * sg_ref[...] * (1.0 - lambda_init)
        o_ref[0] = y.T.astype(BF16)

    _pipelined_key_blocks(qi, tk // tq, score, prob_pv, finalize, tq)


def _da_attn(slopes, lams, sg_t, q_t, k, v_t, *, tq, tk, lambda_init):
    b, s, _ = k.shape
    w = DA_V_DIM
    nq = s // tq
    vec = pl.BlockSpec((1, DA_HEAD_DIM), lambda bi, h, i: (0, 0))
    return pl.pallas_call(
        functools.partial(_da_attn_body, tq=tq, tk=tk, lambda_init=lambda_init),
        grid=(b, DA_HEADS, nq),
        in_specs=[
            pl.BlockSpec(memory_space=pltpu.SMEM),
            vec, vec, vec, vec,
            pl.BlockSpec((w, tq), lambda bi, h, i: (0, 0)),
            pl.BlockSpec((w, tq), lambda bi, h, i: (h, bi * nq + i)),
            pl.BlockSpec((1, s, w), lambda bi, h, i: (bi, 0, h)),
            pl.BlockSpec((w, s), lambda bi, h, i: (h, bi)),
        ],
        out_specs=pl.BlockSpec((1, tq, w), lambda bi, h, i: (bi, i, h)),
        out_shape=jax.ShapeDtypeStruct((b, s, DA_HEADS * w), BF16),
        scratch_shapes=[pltpu.VMEM((w, tq), F32), pltpu.VMEM((w, tq), F32),
                        pltpu.VMEM((2, tk, tq), F32), pltpu.VMEM((2, tk, tq), BF16)],
        compiler_params=_cparams(("parallel", "parallel", "arbitrary")),
        name="da_attn",
    )(slopes, *lams, sg_t, q_t, k, v_t)


def _mla_attn_body(qnt_ref, qrt_ref, kn_ref, kr_ref, vt_ref, o_ref, acc_a_ref, acc_b_ref,
                   s_ref, p_ref, *, tq, tk):
    qi = pl.program_id(2)
    key = lax.broadcasted_iota(jnp.int32, (MXU_WIDTH, tq), 0)
    qry = lax.broadcasted_iota(jnp.int32, (MXU_WIDTH, tq), 1)
    qr_t = qrt_ref[...]
    qa_t = jnp.concatenate([qnt_ref[:LANES, :], qr_t], axis=0)
    qb_t = jnp.concatenate([qnt_ref[LANES:, :], qr_t], axis=0)
    acc_a_ref[...] = jnp.zeros_like(acc_a_ref)
    acc_b_ref[...] = jnp.zeros_like(acc_b_ref)

    no_off = lambda n: None

    def score(j, r):
        k0 = pl.multiple_of(j * tk, tk)
        first_masked, n_tiles = _block_tiles(r, tq, tk)
        specs = {}

        def tile_spec(n):
            if n not in specs:
                keep = None if n < first_masked else _own_chunk_mask(key, qry, r * tq - n * MXU_WIDTH)
                specs[n] = (None, keep)
            return specs[n]

        def k_tile(lo):
            def tile(n):
                rows = pl.ds(pl.multiple_of(k0 + n * MXU_WIDTH, MXU_WIDTH), MXU_WIDTH)
                return jnp.concatenate([kn_ref[0, rows, lo:lo + LANES],
                                        kr_ref[0, rows, lo:lo + LANES]], axis=-1)
            return tile

        return (_score_pass(k_tile(0), qa_t, tile_spec, no_off, n_tiles, s_ref.at[0]),
                _score_pass(k_tile(LANES), qb_t, tile_spec, no_off, n_tiles, s_ref.at[1]))

    def prob_pv(j, r, state, mx):
        ma, la, mb, lb = state
        _, n_tiles = _block_tiles(r, tq, tk)
        ma, la, aa = _prob_pass(s_ref.at[0], p_ref.at[0], ma, la, mx[0], no_off, n_tiles)
        mb, lb, ab = _prob_pass(s_ref.at[1], p_ref.at[1], mb, lb, mx[1], no_off, n_tiles)
        keys = n_tiles * MXU_WIDTH
        cols = pl.ds(pl.multiple_of(j * tk, tk), keys)
        acc_a_ref[...] = aa * acc_a_ref[...] + jnp.dot(vt_ref[:LANES, cols], p_ref[0, :keys, :],
                                                      preferred_element_type=F32)
        acc_b_ref[...] = ab * acc_b_ref[...] + jnp.dot(vt_ref[LANES:, cols], p_ref[1, :keys, :],
                                                      preferred_element_type=F32)
        return ma, la, mb, lb

    def finalize(state):
        _, la, _, lb = state
        o_t = jnp.concatenate([acc_a_ref[...] * (1.0 / la), acc_b_ref[...] * (1.0 / lb)], axis=0)
        o_ref[0] = o_t.T.astype(BF16)

    _pipelined_key_blocks(qi, tk // tq, score, prob_pv, finalize, tq)


def _mla_attn(qn_t, qr_t, kn, kr, v_t, *, tq, tk):
    b, s, _ = kn.shape
    w = 2 * LANES
    nq = s // tq
    kv_spec = pl.BlockSpec((1, s, w), lambda bi, h, i: (bi, 0, h))
    return pl.pallas_call(
        functools.partial(_mla_attn_body, tq=tq, tk=tk),
        grid=(b, MLA_HEADS // 2, nq),
        in_specs=[
            pl.BlockSpec((w, tq), lambda bi, h, i: (h, bi * nq + i)),
            pl.BlockSpec((LANES, tq), lambda bi, h, i: (h, bi * nq + i)),
            kv_spec, kv_spec,
            pl.BlockSpec((w, s), lambda bi, h, i: (h, bi)),
        ],
        out_specs=pl.BlockSpec((1, tq, w), lambda bi, h, i: (bi, i, h)),
        out_shape=jax.ShapeDtypeStruct((b, s, MLA_HEADS * LANES), BF16),
        scratch_shapes=[pltpu.VMEM((LANES, tq), F32), pltpu.VMEM((LANES, tq), F32),
                        pltpu.VMEM((2, tk, tq), F32), pltpu.VMEM((2, tk, tq), BF16)],
        compiler_params=_cparams(("parallel", "parallel", "arbitrary")),
        name="mla_attn",
    )(qn_t, qr_t, kn, kr, v_t)


def _merge_body(x_ref, ya_ref, yb_ref, gate_a_ref, gate_b_ref, wa_ref, wb_ref, wo_ref, o_ref):
    @pl.when(pl.program_id(1) == 0)
    def _():
        o_ref[...] = x_ref[...]

    a = jnp.dot(ya_ref[...], wa_ref[...], preferred_element_type=F32)
    b = jnp.dot(yb_ref[...], wb_ref[...], preferred_element_type=F32)
    m = (gate_a_ref[...].astype(F32) * a + gate_b_ref[...].astype(F32) * b).astype(BF16)
    o_ref[...] += jnp.dot(m, wo_ref[...], preferred_element_type=F32)


def _merge(x, ya, yb, gates, wa, wb, wo, *, tm, tn):
    t, d = x.shape
    nj = d // tn
    tok = pl.BlockSpec((tm, d), lambda i, j: (i, 0))
    return pl.pallas_call(
        _merge_body,
        grid=(t // tm, nj),
        in_specs=[
            tok, tok, tok,
            pl.BlockSpec((tm, tn), lambda i, j: (i, j)),
            pl.BlockSpec((tm, tn), lambda i, j: (i, j + nj)),
            pl.BlockSpec((d, tn), lambda i, j: (0, j)),
            pl.BlockSpec((d, tn), lambda i, j: (0, j)),
            pl.BlockSpec((tn, d), lambda i, j: (j, 0)),
        ],
        out_specs=tok,
        out_shape=jax.ShapeDtypeStruct((t, d), F32),
        compiler_params=_cparams(("parallel", "arbitrary")),
        name="merge",
    )(x, ya, yb, gates, gates, wa, wb, wo)


def _rot_cols(w):
    half = w.shape[-1] // 2
    return jnp.concatenate([-w[..., half:], w[..., :half]], axis=-1)


def _swap_halves(g):
    half = g.shape[-1] // 2
    return jnp.concatenate([g[..., half:], g[..., :half]], axis=-1)


def _pick_tile(n, pref):
    return pref if n % pref == 0 else n


def kernel(x, ffn1_norm_g, ffn1_w1, ffn1_w3, ffn1_w2, mix_norm_g, w_in, da_q_norm_g, da_k_norm_g, da_lambda_q1, da_lambda_k1, da_lambda_q2, da_lambda_k2, da_subln_g, mla_q_a_norm_g, mla_w_qb, mla_kv_a_norm_g, mla_w_kvb, mla_q_norm_g, mla_k_norm_g, w_branch_a, w_branch_b, w_out, ffn2_norm_g, ffn2_w1, ffn2_w3, ffn2_w2):
    bsz, seq, d = x.shape
    t = bsz * seq
    depth = ffn1_norm_g.shape[0]
    tm = _pick_tile(t, 512)
    tm_proj = _pick_tile(t, 1024)
    tm_mla = _pick_tile(seq, 256)
    tq = _pick_tile(seq, 512)
    tk = _pick_tile(seq, 1024)
    row = lambda v: v.reshape(1, -1).astype(F32)
    col_tile = lambda v, n: jnp.broadcast_to(v.astype(F32)[:, None], (v.shape[0], n))

    inv = ROPE_THETA ** (-jnp.arange(0, MLA_ROPE_DIM, 2, dtype=F32) / MLA_ROPE_DIM)
    ang = jnp.arange(seq, dtype=F32)[:, None] * inv[None, :]
    cos2 = jnp.tile(jnp.cos(ang), (1, 4))
    sin2 = jnp.tile(jnp.sin(ang), (1, 4))
    cos_t, sin_t = cos2.T, sin2.T
    slopes = (2.0 ** (-8.0 * jnp.arange(1, DA_HEADS + 1, dtype=F32) / DA_HEADS)) * LOG2E

    xt = x.reshape(t, d)
    for l in range(depth):
        lambda_init = 0.8 - 0.6 * math.exp(-0.3 * l)

        xt = _ffn(xt, row(ffn1_norm_g[l]), ffn1_w1[l].astype(BF16), ffn1_w3[l].astype(BF16),
                  (0.5 * ffn1_w2[l]).astype(BF16), tm=tm, tf=512)

        w = w_in[l]
        o = [0]
        for n in (2 * DA_HEADS * DA_HEAD_DIM, 2 * DA_HEADS * DA_HEAD_DIM, DA_HEADS * DA_V_DIM,
                  MLA_Q_RANK, MLA_KV_RANK, MLA_ROPE_DIM, d, d):
            o.append(o[-1] + n)
        w_kr = w[:, o[5]:o[6]]
        w_kr_rot = _rot_cols(w_kr)
        w_t = jnp.concatenate([w[:, o[0]:o[1]], w[:, o[2]:o[3]]], axis=1).T.astype(BF16)
        w_n = jnp.concatenate(
            [w[:, o[1]:o[2]], w[:, o[3]:o[5]], w_kr, w_kr, w_kr_rot, w_kr_rot, w[:, o[6]:]],
            axis=1).astype(BF16)
        q_scale = DA_HEAD_DIM ** -0.5 * LOG2E
        q_da_t, k_da, v_da_t, lat, gates = _in_proj(
            xt, row(mix_norm_g[l]), w_t, w_n, col_tile(da_q_norm_g[l] * q_scale, tm_proj),
            row(da_k_norm_g[l]), tm=tm_proj)

        wq = mla_w_qb[l].reshape(MLA_Q_RANK, MLA_HEADS, MLA_QK_DIM)
        wq_nope = wq[:, :, :MLA_NOPE_DIM].reshape(MLA_Q_RANK, MLA_HEADS // 2, 2 * LANES)
        wq_rope = wq[:, :, MLA_NOPE_DIM:]
        wq_rot = _rot_cols(wq_rope).reshape(MLA_Q_RANK, MLA_HEADS // 2, LANES)
        wq_rope = wq_rope.reshape(MLA_Q_RANK, MLA_HEADS // 2, LANES)
        wq_t = jnp.concatenate([wq_nope, wq_rope, wq_rot], axis=-1).reshape(
            MLA_Q_RANK, MLA_HEADS * 2 * LANES).T.astype(BF16)
        wkv = mla_w_kvb[l].reshape(MLA_KV_RANK, MLA_HEADS, MLA_NOPE_DIM + MLA_V_DIM)
        wkn = wkv[:, :, :MLA_NOPE_DIM].reshape(MLA_KV_RANK, MLA_HEADS * MLA_NOPE_DIM).astype(BF16)
        wv_t = wkv[:, :, MLA_NOPE_DIM:].reshape(MLA_KV_RANK, MLA_HEADS * MLA_V_DIM).T.astype(BF16)
        mla_scale = MLA_QK_DIM ** -0.5 * LOG2E
        gq, gk = mla_q_norm_g[l], mla_k_norm_g[l]
        q_gains_t = (
            col_tile(gq[:MLA_NOPE_DIM] * mla_scale, tm_mla),
            col_tile(jnp.tile(gq[MLA_NOPE_DIM:], 2) * mla_scale, tm_mla),
            col_tile(jnp.tile(_swap_halves(gq[MLA_NOPE_DIM:]), 2) * mla_scale, tm_mla),
        )
        k_gains = (
            row(gk[:MLA_NOPE_DIM]),
            row(jnp.tile(gk[MLA_NOPE_DIM:], 2)),
            row(jnp.tile(_swap_halves(gk[MLA_NOPE_DIM:]), 2)),
        )
        qn_t, qr_t, kn, kr, v_mla_t = _mla_proj(
            lat, row(mla_q_a_norm_g[l]), row(mla_kv_a_norm_g[l]), wq_t, wkn, wv_t,
            q_gains_t, k_gains, cos2, sin2, cos_t, sin_t, tm=tm_mla, seq=seq)

        b3 = lambda a: a.reshape(bsz, seq, a.shape[-1])
        lams = (row(da_lambda_q1[l]), row(da_lambda_k1[l]), row(da_lambda_q2[l]), row(da_lambda_k2[l]))
        y_a = _da_attn(slopes, lams, col_tile(da_subln_g[l], tq), q_da_t, b3(k_da), v_da_t,
                       tq=tq, tk=tk, lambda_init=lambda_init)
        y_b = _mla_attn(qn_t, qr_t, b3(kn), b3(kr), v_mla_t, tq=tq, tk=tk)

        xt = _merge(xt, y_a.reshape(t, -1), y_b.reshape(t, -1), gates,
                    w_branch_a[l].astype(BF16), w_branch_b[l].astype(BF16), w_out[l].astype(BF16),
                    tm=tm, tn=512)

        xt = _ffn(xt, row(ffn2_norm_g[l]), ffn2_w1[l].astype(BF16), ffn2_w3[l].astype(BF16),
                  (0.5 * ffn2_w2[l]).astype(BF16), tm=tm, tf=512)
    return xt.reshape(bsz, seq, d)
```

```python
import functools
import math

import jax
import jax.numpy as jnp
from jax import lax
from jax.experimental import pallas as pl
from jax.experimental.pallas import tpu as pltpu

F32 = jnp.float32
BF16 = jnp.bfloat16

EPS = 1e-6
CHUNK = 64
CHUNK_SHIFT = CHUNK.bit_length() - 1
DA_HEADS = 8
DA_HEAD_DIM = 128
DA_V_DIM = 2 * DA_HEAD_DIM
MLA_HEADS = 16
MLA_Q_RANK = 768
MLA_KV_RANK = 512
MLA_NOPE_DIM = 128
MLA_ROPE_DIM = 64
MLA_V_DIM = 128
MLA_QK_DIM = MLA_NOPE_DIM + MLA_ROPE_DIM
ROPE_THETA = 10000.0
LOG2E = math.log2(math.e)
MASK_VALUE = -1e30

LANES = 128
SUBLANES = 8
MXU_WIDTH = 256
V7X_VMEM_BYTES = 64 * 1024 * 1024
VMEM_LIMIT = V7X_VMEM_BYTES - 8 * 1024 * 1024


def _cparams(sem):
    return pltpu.CompilerParams(dimension_semantics=sem, vmem_limit_bytes=VMEM_LIMIT)


def _rms(x, g):
    ms = jnp.mean(x * x, axis=-1, keepdims=True)
    return x * lax.rsqrt(ms + EPS) * g


def _nt_dot(a, b):
    return lax.dot_general(a, b, (((1,), (1,)), ((), ())), preferred_element_type=F32)


def _ffn_body(x_ref, g_ref, w1_ref, w3_ref, w2_ref, o_ref, h_ref):
    @pl.when(pl.program_id(1) == 0)
    def _():
        x = x_ref[...]
        h_ref[...] = _rms(x, g_ref[...]).astype(BF16)
        o_ref[...] = x

    h = h_ref[...]
    a = jnp.dot(h, w1_ref[...], preferred_element_type=F32)
    b = jnp.dot(h, w3_ref[...], preferred_element_type=F32)
    u = (a * jax.nn.sigmoid(a) * b).astype(BF16)
    o_ref[...] += jnp.dot(u, w2_ref[...], preferred_element_type=F32)


def _ffn(x, g, w1, w3, w2_half, *, tm, tf):
    t, d = x.shape
    f = w1.shape[1]
    return pl.pallas_call(
        _ffn_body,
        grid=(t // tm, f // tf),
        in_specs=[
            pl.BlockSpec((tm, d), lambda i, j: (i, 0)),
            pl.BlockSpec((1, d), lambda i, j: (0, 0)),
            pl.BlockSpec((d, tf), lambda i, j: (0, j)),
            pl.BlockSpec((d, tf), lambda i, j: (0, j)),
            pl.BlockSpec((tf, d), lambda i, j: (j, 0)),
        ],
        out_specs=pl.BlockSpec((tm, d), lambda i, j: (i, 0)),
        out_shape=jax.ShapeDtypeStruct((t, d), F32),
        scratch_shapes=[pltpu.VMEM((tm, d), BF16)],
        compiler_params=_cparams(("parallel", "arbitrary")),
        name="ffn",
    )(x, g, w1, w3, w2_half)


PROJ_TN = 512
_Q0, _K0, _V0, _C0, _G0, _NJ = 0, 4, 8, 12, 15, 23


def _head_norm(acc, g):
    parts = []
    for c in range(acc.shape[1] // LANES):
        blk = acc[:, c * LANES:(c + 1) * LANES]
        ms = jnp.mean(blk * blk, axis=-1, keepdims=True)
        parts.append(blk * lax.rsqrt(ms + EPS) * g)
    return jnp.concatenate(parts, axis=-1)


def _head_norm_t(acc_t, g_t):
    parts = []
    for c in range(acc_t.shape[0] // LANES):
        blk = acc_t[c * LANES:(c + 1) * LANES, :]
        ms = jnp.mean(blk * blk, axis=0, keepdims=True)
        parts.append(blk * lax.rsqrt(ms + EPS) * g_t)
    return jnp.concatenate(parts, axis=0)


def _proj_body(x_ref, g_ref, wt_ref, wn_ref, qg_ref, kg_ref,
               qt_ref, k_ref, vt_ref, c_ref, gate_ref, h_ref):
    j = pl.program_id(1)

    @pl.when(j == 0)
    def _():
        h_ref[...] = _rms(x_ref[...], g_ref[...]).astype(BF16)

    @pl.when(j < _K0)
    def _():
        qt_ref[...] = _head_norm_t(_nt_dot(wt_ref[...], h_ref[...]), qg_ref[...]).astype(BF16)

    @pl.when((j >= _K0) & (j < _V0))
    def _():
        acc = jnp.dot(h_ref[...], wn_ref[...], preferred_element_type=F32)
        k_ref[...] = _head_norm(acc, kg_ref[...]).astype(BF16)

    @pl.when((j >= _V0) & (j < _C0))
    def _():
        vt_ref[...] = _nt_dot(wt_ref[...], h_ref[...]).astype(BF16)

    @pl.when((j >= _C0) & (j < _G0))
    def _():
        c_ref[...] = jnp.dot(h_ref[...], wn_ref[...], preferred_element_type=F32)

    @pl.when(j >= _G0)
    def _():
        acc = jnp.dot(h_ref[...], wn_ref[...], preferred_element_type=F32)
        gate_ref[...] = jax.nn.sigmoid(acc).astype(BF16)


def _in_proj(x, g, w_t, w_n, qg_t, kg, *, tm):
    t, d = x.shape
    tn = PROJ_TN
    n_q, n_k, n_v, n_c, n_g = _K0 - _Q0, _V0 - _K0, _C0 - _V0, _G0 - _C0, _NJ - _G0

    def wt_idx(i, j):
        return (jnp.where(j < _K0, j, jnp.clip(j - n_k, n_q - 1, n_q + n_v - 1)), 0)

    def wn_idx(i, j):
        return (0, jnp.where(j < _V0, jnp.clip(j - _K0, 0, n_k - 1),
                             jnp.clip(j - _K0 - n_v, n_k - 1, n_k + n_c + n_g - 1)))

    def cols(j0, n):
        return pl.BlockSpec((tm, tn), lambda i, j: (i, jnp.clip(j - j0, 0, n - 1)))

    def rows(j0, n):
        return pl.BlockSpec((tn, tm), lambda i, j: (jnp.clip(j - j0, 0, n - 1), i))

    return pl.pallas_call(
        _proj_body,
        grid=(t // tm, _NJ),
        in_specs=[
            pl.BlockSpec((tm, d), lambda i, j: (i, 0)),
            pl.BlockSpec((1, d), lambda i, j: (0, 0)),
            pl.BlockSpec((tn, d), wt_idx),
            pl.BlockSpec((d, tn), wn_idx),
            pl.BlockSpec((LANES, tm), lambda i, j: (0, 0)),
            pl.BlockSpec((1, LANES), lambda i, j: (0, 0)),
        ],
        out_specs=[rows(_Q0, n_q), cols(_K0, n_k), rows(_V0, n_v), cols(_C0, n_c), cols(_G0, n_g)],
        out_shape=[
            jax.ShapeDtypeStruct((n_q * tn, t), BF16),
            jax.ShapeDtypeStruct((t, n_k * tn), BF16),
            jax.ShapeDtypeStruct((n_v * tn, t), BF16),
            jax.ShapeDtypeStruct((t, n_c * tn), F32),
            jax.ShapeDtypeStruct((t, n_g * tn), BF16),
        ],
        scratch_shapes=[pltpu.VMEM((tm, d), BF16)],
        compiler_params=_cparams(("parallel", "arbitrary")),
        name="in_proj",
    )(x, g, w_t, w_n, qg_t, kg)


def _mla_proj_body(c_ref, qa_g_ref, kva_g_ref, wqt_ref, wkn_ref, wvt_ref,
                   gqn_ref, gqr_ref, gqt_ref, gk_nope_ref, gk_rope_ref, gk_rot_ref,
                   cos_ref, sin_ref, cost_ref, sint_ref,
                   qnt_ref, qrt_ref, kn_ref, kr_ref, vt_ref):
    c = c_ref[...]
    o_kv = MLA_Q_RANK
    o_kr = MLA_Q_RANK + MLA_KV_RANK
    cqn = _rms(c[:, :o_kv], qa_g_ref[...]).astype(BF16)
    ckvn = _rms(c[:, o_kv:o_kr], kva_g_ref[...]).astype(BF16)
    k_rope = c[:, o_kr:o_kr + LANES]
    k_rot = c[:, o_kr + LANES:o_kr + 2 * LANES]
    first = lax.broadcasted_iota(jnp.int32, (1, LANES), 1) < MLA_ROPE_DIM
    k_roped = k_rope * gk_rope_ref[...] * cos_ref[...] + k_rot * gk_rot_ref[...] * sin_ref[...]
    k_rope_sq = 0.5 * jnp.sum(k_rope * k_rope, axis=-1, keepdims=True)
    inv_d = 1.0 / MLA_QK_DIM
    cos_t = cost_ref[...]
    sin_t = sint_ref[...]
    half = MLA_ROPE_DIM

    for p in range(MLA_HEADS // 2):
        qraw = _nt_dot(wqt_ref[p * 4 * LANES:(p + 1) * 4 * LANES, :], cqn)
        na, nb = qraw[:LANES], qraw[LANES:2 * LANES]
        rp, rt = qraw[2 * LANES:3 * LANES], qraw[3 * LANES:]
        rp2 = rp * rp
        sa = jnp.sum(na * na, axis=0, keepdims=True) + jnp.sum(rp2[:half], axis=0, keepdims=True)
        sb = jnp.sum(nb * nb, axis=0, keepdims=True) + jnp.sum(rp2[half:], axis=0, keepdims=True)
        ra = lax.rsqrt(sa * inv_d + EPS)
        rb = lax.rsqrt(sb * inv_d + EPS)
        ra_rows = slice((2 * p) * LANES, (2 * p + 1) * LANES)
        rb_rows = slice((2 * p + 1) * LANES, (2 * p + 2) * LANES)
        qnt_ref[ra_rows, :] = (na * ra * gqn_ref[...]).astype(BF16)
        qnt_ref[rb_rows, :] = (nb * rb * gqn_ref[...]).astype(BF16)
        roped = rp * gqr_ref[...] * cos_t + rt * gqt_ref[...] * sin_t
        qrt_ref[p * LANES:(p + 1) * LANES, :] = jnp.concatenate(
            [roped[:half] * ra, roped[half:] * rb], axis=0).astype(BF16)

        knraw = jnp.dot(ckvn, wkn_ref[:, p * 2 * LANES:(p + 1) * 2 * LANES], preferred_element_type=F32)
        kna, knb = knraw[:, :LANES], knraw[:, LANES:]
        rka = lax.rsqrt((jnp.sum(kna * kna, axis=-1, keepdims=True) + k_rope_sq) * inv_d + EPS)
        rkb = lax.rsqrt((jnp.sum(knb * knb, axis=-1, keepdims=True) + k_rope_sq) * inv_d + EPS)
        kn_ref[:, ra_rows] = (kna * rka * gk_nope_ref[...]).astype(BF16)
        kn_ref[:, rb_rows] = (knb * rkb * gk_nope_ref[...]).astype(BF16)
        kr_ref[:, ra_rows] = jnp.where(first, k_roped * rka, 0.0).astype(BF16)
        kr_ref[:, rb_rows] = jnp.where(first, 0.0, k_roped * rkb).astype(BF16)

        vt_ref[p * 2 * LANES:(p + 1) * 2 * LANES, :] = _nt_dot(
            wvt_ref[p * 2 * LANES:(p + 1) * 2 * LANES, :], ckvn).astype(BF16)


def _mla_proj(c, qa_g, kva_g, wq_t, wkn, wv_t, q_gains_t, k_gains, cos2, sin2, cos_t, sin_t, *, tm, seq):
    t = c.shape[0]
    n_pos = seq // tm
    const = lambda a: pl.BlockSpec(a.shape, lambda i: (0, 0))
    hw = MLA_HEADS * LANES
    tok = lambda w: pl.BlockSpec((tm, w), lambda i: (i, 0))
    tok_t = lambda w: pl.BlockSpec((w, tm), lambda i: (0, i))
    return pl.pallas_call(
        _mla_proj_body,
        grid=(t // tm,),
        in_specs=[tok(c.shape[1]), const(qa_g), const(kva_g), const(wq_t), const(wkn), const(wv_t)]
        + [const(a) for a in q_gains_t] + [const(a) for a in k_gains] + [
            pl.BlockSpec((tm, LANES), lambda i: (i % n_pos, 0)),
            pl.BlockSpec((tm, LANES), lambda i: (i % n_pos, 0)),
            pl.BlockSpec((LANES, tm), lambda i: (0, i % n_pos)),
            pl.BlockSpec((LANES, tm), lambda i: (0, i % n_pos)),
        ],
        out_specs=[tok_t(hw), tok_t(hw // 2), tok(hw), tok(hw), tok_t(hw)],
        out_shape=[
            jax.ShapeDtypeStruct((hw, t), BF16),
            jax.ShapeDtypeStruct((hw // 2, t), BF16),
            jax.ShapeDtypeStruct((t, hw), BF16),
            jax.ShapeDtypeStruct((t, hw), BF16),
            jax.ShapeDtypeStruct((hw, t), BF16),
        ],
        compiler_params=_cparams(("parallel",)),
        name="mla_proj",
    )(c, qa_g, kva_g, wq_t, wkn, wv_t, *q_gains_t, *k_gains, cos2, sin2, cos_t, sin_t)


def _score_pass(k_tile, q_t, tile_spec, tile_off, n_tiles, s_ref):
    tq = q_t.shape[1]
    m8 = None
    for n in range(n_tiles):
        add, keep = tile_spec(n)
        u = jnp.dot(k_tile(n), q_t, preferred_element_type=F32)
        if add is not None:
            u = u + add
        if keep is not None:
            u = jnp.where(keep, u, MASK_VALUE)
        s_ref[n * MXU_WIDTH:(n + 1) * MXU_WIDTH, :] = u
        t = jnp.max(u.reshape(MXU_WIDTH // SUBLANES, SUBLANES, tq), axis=0)
        off = tile_off(n)
        if off is not None:
            t = t + off
        m8 = t if m8 is None else jnp.maximum(m8, t)
    return jnp.max(m8, axis=0, keepdims=True)


def _prob_pass(s_ref, p_ref, m, l, m_blk, tile_off, n_tiles):
    tq = s_ref.shape[1]
    m_new = jnp.maximum(m, m_blk)
    alpha = jnp.exp2(m - m_new)
    ps8 = None
    for n in range(n_tiles):
        rows = slice(n * MXU_WIDTH, (n + 1) * MXU_WIDTH)
        off = tile_off(n)
        ref = m_new if off is None else m_new - off
        p = jnp.exp2(s_ref[rows, :] - ref)
        t = jnp.sum(p.reshape(MXU_WIDTH // SUBLANES, SUBLANES, tq), axis=0)
        ps8 = t if ps8 is None else ps8 + t
        p_ref[rows, :] = p.astype(BF16)
    l_new = alpha * l + jnp.sum(ps8, axis=0, keepdims=True)
    return m_new, l_new, alpha


def _pipelined_key_blocks(n_full, subs, tq):
    neg = jnp.full((1, tq), MASK_VALUE, F32)
    zero = jnp.zeros((1, tq), F32)
    init = (neg, zero, neg, zero)

    @pl.when(n_full == 0)
    def _():
        maxima = [score(0, r) for r, (score, _, _) in enumerate(subs)]
        for r, (_, prob_pv, finalize) in enumerate(subs):
            finalize(prob_pv(0, r, init, maxima[r]))

    @pl.when(n_full > 0)
    def _():
        mx = subs[0][0](0, None)
        for r, (score, prob_pv, finalize) in enumerate(subs):
            def body(j, carry, score=score, prob_pv=prob_pv):
                state, mx = carry
                state = prob_pv(j, None, state, mx)
                return state, score(j + 1, None)

            state, mx = lax.fori_loop(0, n_full - 1, body, (init, mx))
            state = prob_pv(n_full - 1, None, state, mx)
            mx_last = score(n_full, r)
            if r + 1 < len(subs):
                mx = subs[r + 1][0](0, None)
            finalize(prob_pv(n_full, r, state, mx_last))


def _block_tiles(r, tq, tk):
    per_q = tq // MXU_WIDTH
    if r is None:
        return tk // MXU_WIDTH, tk // MXU_WIDTH
    return r * per_q, (r + 1) * per_q


def _own_chunk_mask(key, qry, dq):
    return ((key - dq) >> CHUNK_SHIFT) <= (qry >> CHUNK_SHIFT)


def _da_attn_body(slopes_ref, lq1_ref, lk1_ref, lq2_ref, lk2_ref, sg_ref,
                  qt_ref, k_ref, vt_ref, o_ref, acc_ref, s_ref, p_ref,
                  *, tq, tk, lambda_init):
    h = pl.program_id(1)
    n_full = pl.program_id(2)
    slope = slopes_ref[h]
    key = lax.broadcasted_iota(jnp.int32, (MXU_WIDTH, tq), 0)
    qry = lax.broadcasted_iota(jnp.int32, (MXU_WIDTH, tq), 1)
    key_bias = slope * key.astype(F32)
    mirror_bias = slope * (2 * qry - key).astype(F32)
    acc_ref[...] = jnp.zeros_like(acc_ref)
    subs = [_da_sub_block(sub, n_full, slope, key, qry, key_bias, mirror_bias,
                          lq1_ref, lk1_ref, lq2_ref, lk2_ref, sg_ref, qt_ref, k_ref, vt_ref, o_ref,
                          acc_ref, s_ref, p_ref, tq=tq, tk=tk, lambda_init=lambda_init)
            for sub in range(tk // tq)]
    _pipelined_key_blocks(n_full, subs, tq)


def _da_sub_block(sub, n_full, slope, key, qry, key_bias, mirror_bias,
                  lq1_ref, lk1_ref, lq2_ref, lk2_ref, sg_ref, qt_ref, k_ref, vt_ref, o_ref,
                  acc_ref, s_ref, p_ref, *, tq, tk, lambda_init):
    q_cols = slice(sub * tq, (sub + 1) * tq)
    q1_t = qt_ref[:DA_HEAD_DIM, q_cols]
    q2_t = qt_ref[DA_HEAD_DIM:, q_cols]
    q0 = n_full * tk + sub * tq
    acc1_ref, acc2_ref = acc_ref.at[sub, 0], acc_ref.at[sub, 1]
    s_ref, p_ref = s_ref.at[sub], p_ref.at[sub]

    def tile_plan(j, r):
        first_masked, n_tiles = _block_tiles(r, tq, tk)
        specs = {}

        def tile_spec(n):
            if n not in specs:
                if n < first_masked:
                    specs[n] = (key_bias, None)
                else:
                    dq = r * tq - n * MXU_WIDTH
                    bias = jnp.minimum(key_bias - slope * dq, mirror_bias + slope * dq)
                    specs[n] = (bias, _own_chunk_mask(key, qry, dq))
            return specs[n]

        def tile_off(n):
            if n >= first_masked:
                return None
            if r is None:
                return slope * (j * tk + n * MXU_WIDTH - q0).astype(F32)
            return slope * float(n * MXU_WIDTH - r * tq)

        return tile_spec, tile_off, n_tiles

    def score(j, r):
        k0 = pl.multiple_of(j * tk, tk)
        tile_spec, tile_off, n_tiles = tile_plan(j, r)

        def k_tile(lo):
            return lambda n: k_ref[0, pl.ds(pl.multiple_of(k0 + n * MXU_WIDTH, MXU_WIDTH), MXU_WIDTH),
                                   lo:lo + DA_HEAD_DIM]

        return (_score_pass(k_tile(0), q1_t, tile_spec, tile_off, n_tiles, s_ref.at[0]),
                _score_pass(k_tile(DA_HEAD_DIM), q2_t, tile_spec, tile_off, n_tiles, s_ref.at[1]))

    def prob_pv(j, r, state, mx):
        m1, l1, m2, l2 = state
        _, tile_off, n_tiles = tile_plan(j, r)
        m1, l1, a1 = _prob_pass(s_ref.at[0], p_ref.at[0], m1, l1, mx[0], tile_off, n_tiles)
        m2, l2, a2 = _prob_pass(s_ref.at[1], p_ref.at[1], m2, l2, mx[1], tile_off, n_tiles)
        keys = n_tiles * MXU_WIDTH
        vb = vt_ref[:, pl.ds(pl.multiple_of(j * tk, tk), keys)]
        acc1_ref[...] = a1 * acc1_ref[...] + jnp.dot(vb, p_ref[0, :keys, :], preferred_element_type=F32)
        acc2_ref[...] = a2 * acc2_ref[...] + jnp.dot(vb, p_ref[1, :keys, :], preferred_element_type=F32)
        return m1, l1, m2, l2

    def finalize(state):
        _, l1, _, l2 = state
        lam = (jnp.exp(jnp.sum(lq1_ref[...] * lk1_ref[...], axis=-1, keepdims=True))
               - jnp.exp(jnp.sum(lq2_ref[...] * lk2_ref[...], axis=-1, keepdims=True))
               + lambda_init)
        o = acc1_ref[...] * (1.0 / l1) - lam * (acc2_ref[...] * (1.0 / l2))
        ms = jnp.mean(o * o, axis=0, keepdims=True)
        y = o * lax.rsqrt(ms + EPS) * sg_ref[...] * (1.0 - lambda_init)
        o_ref[0, q_cols, :] = y.T.astype(BF16)

    return score, prob_pv, finalize


def _da_attn(slopes, lams, sg_t, q_t, k, v_t, *, tq, tk, lambda_init):
    b, s, _ = k.shape
    w = DA_V_DIM
    nq = s // tk
    n_sub = tk // tq
    vec = pl.BlockSpec((1, DA_HEAD_DIM), lambda bi, h, i: (0, 0))
    return pl.pallas_call(
        functools.partial(_da_attn_body, tq=tq, tk=tk, lambda_init=lambda_init),
        grid=(b, DA_HEADS, nq),
        in_specs=[
            pl.BlockSpec(memory_space=pltpu.SMEM),
            vec, vec, vec, vec,
            pl.BlockSpec((w, tq), lambda bi, h, i: (0, 0)),
            pl.BlockSpec((w, tk), lambda bi, h, i: (h, bi * nq + i)),
            pl.BlockSpec((1, s, w), lambda bi, h, i: (bi, 0, h)),
            pl.BlockSpec((w, s), lambda bi, h, i: (h, bi)),
        ],
        out_specs=pl.BlockSpec((1, tk, w), lambda bi, h, i: (bi, i, h)),
        out_shape=jax.ShapeDtypeStruct((b, s, DA_HEADS * w), BF16),
        scratch_shapes=[pltpu.VMEM((n_sub, 2, w, tq), F32),
                        pltpu.VMEM((n_sub, 2, tk, tq), F32), pltpu.VMEM((n_sub, 2, tk, tq), BF16)],
        compiler_params=_cparams(("parallel", "parallel", "arbitrary")),
        name="da_attn",
    )(slopes, *lams, sg_t, q_t, k, v_t)


def _mla_attn_body(qnt_ref, qrt_ref, kn_ref, kr_ref, vt_ref, o_ref, acc_ref, s_ref, p_ref, *, tq, tk):
    n_full = pl.program_id(2)
    key = lax.broadcasted_iota(jnp.int32, (MXU_WIDTH, tq), 0)
    qry = lax.broadcasted_iota(jnp.int32, (MXU_WIDTH, tq), 1)
    acc_ref[...] = jnp.zeros_like(acc_ref)
    subs = [_mla_sub_block(sub, key, qry, qnt_ref, qrt_ref, kn_ref, kr_ref, vt_ref, o_ref,
                           acc_ref, s_ref, p_ref, tq=tq, tk=tk) for sub in range(tk // tq)]
    _pipelined_key_blocks(n_full, subs, tq)


def _mla_sub_block(sub, key, qry, qnt_ref, qrt_ref, kn_ref, kr_ref, vt_ref, o_ref,
                   acc_ref, s_ref, p_ref, *, tq, tk):
    q_cols = slice(sub * tq, (sub + 1) * tq)
    qr_t = qrt_ref[:, q_cols]
    qa_t = jnp.concatenate([qnt_ref[:LANES, q_cols], qr_t], axis=0)
    qb_t = jnp.concatenate([qnt_ref[LANES:, q_cols], qr_t], axis=0)
    acc_a_ref, acc_b_ref = acc_ref.at[sub, 0], acc_ref.at[sub, 1]
    s_ref, p_ref = s_ref.at[sub], p_ref.at[sub]
    no_off = lambda n: None

    def score(j, r):
        k0 = pl.multiple_of(j * tk, tk)
        first_masked, n_tiles = _block_tiles(r, tq, tk)
        specs = {}

        def tile_spec(n):
            if n not in specs:
                keep = None if n < first_masked else _own_chunk_mask(key, qry, r * tq - n * MXU_WIDTH)
                specs[n] = (None, keep)
            return specs[n]

        def k_tile(lo):
            def tile(n):
                rows = pl.ds(pl.multiple_of(k0 + n * MXU_WIDTH, MXU_WIDTH), MXU_WIDTH)
                return jnp.concatenate([kn_ref[0, rows, lo:lo + LANES],
                                        kr_ref[0, rows, lo:lo + LANES]], axis=-1)
            return tile

        return (_score_pass(k_tile(0), qa_t, tile_spec, no_off, n_tiles, s_ref.at[0]),
                _score_pass(k_tile(LANES), qb_t, tile_spec, no_off, n_tiles, s_ref.at[1]))

    def prob_pv(j, r, state, mx):
        ma, la, mb, lb = state
        _, n_tiles = _block_tiles(r, tq, tk)
        ma, la, aa = _prob_pass(s_ref.at[0], p_ref.at[0], ma, la, mx[0], no_off, n_tiles)
        mb, lb, ab = _prob_pass(s_ref.at[1], p_ref.at[1], mb, lb, mx[1], no_off, n_tiles)
        keys = n_tiles * MXU_WIDTH
        cols = pl.ds(pl.multiple_of(j * tk, tk), keys)
        acc_a_ref[...] = aa * acc_a_ref[...] + jnp.dot(vt_ref[:LANES, cols], p_ref[0, :keys, :],
                                                      preferred_element_type=F32)
        acc_b_ref[...] = ab * acc_b_ref[...] + jnp.dot(vt_ref[LANES:, cols], p_ref[1, :keys, :],
                                                      preferred_element_type=F32)
        return ma, la, mb, lb

    def finalize(state):
        _, la, _, lb = state
        o_t = jnp.concatenate([acc_a_ref[...] * (1.0 / la), acc_b_ref[...] * (1.0 / lb)], axis=0)
        o_ref[0, q_cols, :] = o_t.T.astype(BF16)

    return score, prob_pv, finalize


def _mla_attn(qn_t, qr_t, kn, kr, v_t, *, tq, tk):
    b, s, _ = kn.shape
    w = 2 * LANES
    nq = s // tk
    n_sub = tk // tq
    kv_spec = pl.BlockSpec((1, s, w), lambda bi, h, i: (bi, 0, h))
    return pl.pallas_call(
        functools.partial(_mla_attn_body, tq=tq, tk=tk),
        grid=(b, MLA_HEADS // 2, nq),
        in_specs=[
            pl.BlockSpec((w, tk), lambda bi, h, i: (h, bi * nq + i)),
            pl.BlockSpec((LANES, tk), lambda bi, h, i: (h, bi * nq + i)),
            kv_spec, kv_spec,
            pl.BlockSpec((w, s), lambda bi, h, i: (h, bi)),
        ],
        out_specs=pl.BlockSpec((1, tk, w), lambda bi, h, i: (bi, i, h)),
        out_shape=jax.ShapeDtypeStruct((b, s, MLA_HEADS * LANES), BF16),
        scratch_shapes=[pltpu.VMEM((n_sub, 2, LANES, tq), F32),
                        pltpu.VMEM((n_sub, 2, tk, tq), F32), pltpu.VMEM((n_sub, 2, tk, tq), BF16)],
        compiler_params=_cparams(("parallel", "parallel", "arbitrary")),
        name="mla_attn",
    )(qn_t, qr_t, kn, kr, v_t)


def _merge_body(x_ref, ya_ref, yb_ref, gate_a_ref, gate_b_ref, wa_ref, wb_ref, wo_ref, o_ref):
    @pl.when(pl.program_id(1) == 0)
    def _():
        o_ref[...] = x_ref[...]

    a = jnp.dot(ya_ref[...], wa_ref[...], preferred_element_type=F32)
    b = jnp.dot(yb_ref[...], wb_ref[...], preferred_element_type=F32)
    m = (gate_a_ref[...].astype(F32) * a + gate_b_ref[...].astype(F32) * b).astype(BF16)
    o_ref[...] += jnp.dot(m, wo_ref[...], preferred_element_type=F32)


def _merge(x, ya, yb, gates, wa, wb, wo, *, tm, tn):
    t, d = x.shape
    nj = d // tn
    tok = pl.BlockSpec((tm, d), lambda i, j: (i, 0))
    return pl.pallas_call(
        _merge_body,
        grid=(t // tm, nj),
        in_specs=[
            tok, tok, tok,
            pl.BlockSpec((tm, tn), lambda i, j: (i, j)),
            pl.BlockSpec((tm, tn), lambda i, j: (i, j + nj)),
            pl.BlockSpec((d, tn), lambda i, j: (0, j)),
            pl.BlockSpec((d, tn), lambda i, j: (0, j)),
            pl.BlockSpec((tn, d), lambda i, j: (j, 0)),
        ],
        out_specs=tok,
        out_shape=jax.ShapeDtypeStruct((t, d), F32),
        compiler_params=_cparams(("parallel", "arbitrary")),
        name="merge",
    )(x, ya, yb, gates, gates, wa, wb, wo)


def _rot_cols(w):
    half = w.shape[-1] // 2
    return jnp.concatenate([-w[..., half:], w[..., :half]], axis=-1)


def _swap_halves(g):
    half = g.shape[-1] // 2
    return jnp.concatenate([g[..., half:], g[..., :half]], axis=-1)


def _pick_tile(n, pref):
    return pref if n % pref == 0 else n


def kernel(x, ffn1_norm_g, ffn1_w1, ffn1_w3, ffn1_w2, mix_norm_g, w_in, da_q_norm_g, da_k_norm_g, da_lambda_q1, da_lambda_k1, da_lambda_q2, da_lambda_k2, da_subln_g, mla_q_a_norm_g, mla_w_qb, mla_kv_a_norm_g, mla_w_kvb, mla_q_norm_g, mla_k_norm_g, w_branch_a, w_branch_b, w_out, ffn2_norm_g, ffn2_w1, ffn2_w3, ffn2_w2):
    bsz, seq, d = x.shape
    t = bsz * seq
    depth = ffn1_norm_g.shape[0]
    tm = _pick_tile(t, 512)
    tm_proj = _pick_tile(t, 1024)
    tm_mla = _pick_tile(seq, 256)
    tq = _pick_tile(seq, 512)
    tk = _pick_tile(seq, 1024)
    row = lambda v: v.reshape(1, -1).astype(F32)
    col_tile = lambda v, n: jnp.broadcast_to(v.astype(F32)[:, None], (v.shape[0], n))

    inv = ROPE_THETA ** (-jnp.arange(0, MLA_ROPE_DIM, 2, dtype=F32) / MLA_ROPE_DIM)
    ang = jnp.arange(seq, dtype=F32)[:, None] * inv[None, :]
    cos2 = jnp.tile(jnp.cos(ang), (1, 4))
    sin2 = jnp.tile(jnp.sin(ang), (1, 4))
    cos_t, sin_t = cos2.T, sin2.T
    slopes = (2.0 ** (-8.0 * jnp.arange(1, DA_HEADS + 1, dtype=F32) / DA_HEADS)) * LOG2E

    xt = x.reshape(t, d)
    for l in range(depth):
        lambda_init = 0.8 - 0.6 * math.exp(-0.3 * l)

        xt = _ffn(xt, row(ffn1_norm_g[l]), ffn1_w1[l].astype(BF16), ffn1_w3[l].astype(BF16),
                  (0.5 * ffn1_w2[l]).astype(BF16), tm=tm, tf=512)

        w = w_in[l]
        o = [0]
        for n in (2 * DA_HEADS * DA_HEAD_DIM, 2 * DA_HEADS * DA_HEAD_DIM, DA_HEADS * DA_V_DIM,
                  MLA_Q_RANK, MLA_KV_RANK, MLA_ROPE_DIM, d, d):
            o.append(o[-1] + n)
        w_kr = w[:, o[5]:o[6]]
        w_kr_rot = _rot_cols(w_kr)
        w_t = jnp.concatenate([w[:, o[0]:o[1]], w[:, o[2]:o[3]]], axis=1).T.astype(BF16)
        w_n = jnp.concatenate(
            [w[:, o[1]:o[2]], w[:, o[3]:o[5]], w_kr, w_kr, w_kr_rot, w_kr_rot, w[:, o[6]:]],
            axis=1).astype(BF16)
        q_scale = DA_HEAD_DIM ** -0.5 * LOG2E
        q_da_t, k_da, v_da_t, lat, gates = _in_proj(
            xt, row(mix_norm_g[l]), w_t, w_n, col_tile(da_q_norm_g[l] * q_scale, tm_proj),
            row(da_k_norm_g[l]), tm=tm_proj)

        wq = mla_w_qb[l].reshape(MLA_Q_RANK, MLA_HEADS, MLA_QK_DIM)
        wq_nope = wq[:, :, :MLA_NOPE_DIM].reshape(MLA_Q_RANK, MLA_HEADS // 2, 2 * LANES)
        wq_rope = wq[:, :, MLA_NOPE_DIM:]
        wq_rot = _rot_cols(wq_rope).reshape(MLA_Q_RANK, MLA_HEADS // 2, LANES)
        wq_rope = wq_rope.reshape(MLA_Q_RANK, MLA_HEADS // 2, LANES)
        wq_t = jnp.concatenate([wq_nope, wq_rope, wq_rot], axis=-1).reshape(
            MLA_Q_RANK, MLA_HEADS * 2 * LANES).T.astype(BF16)
        wkv = mla_w_kvb[l].reshape(MLA_KV_RANK, MLA_HEADS, MLA_NOPE_DIM + MLA_V_DIM)
        wkn = wkv[:, :, :MLA_NOPE_DIM].reshape(MLA_KV_RANK, MLA_HEADS * MLA_NOPE_DIM).astype(BF16)
        wv_t = wkv[:, :, MLA_NOPE_DIM:].reshape(MLA_KV_RANK, MLA_HEADS * MLA_V_DIM).T.astype(BF16)
        mla_scale = MLA_QK_DIM ** -0.5 * LOG2E
        gq, gk = mla_q_norm_g[l], mla_k_norm_g[l]
        q_gains_t = (
            col_tile(gq[:MLA_NOPE_DIM] * mla_scale, tm_mla),
            col_tile(jnp.tile(gq[MLA_NOPE_DIM:], 2) * mla_scale, tm_mla),
            col_tile(jnp.tile(_swap_halves(gq[MLA_NOPE_DIM:]), 2) * mla_scale, tm_mla),
        )
        k_gains = (
            row(gk[:MLA_NOPE_DIM]),
            row(jnp.tile(gk[MLA_NOPE_DIM:], 2)),
            row(jnp.tile(_swap_halves(gk[MLA_NOPE_DIM:]), 2)),
        )
        qn_t, qr_t, kn, kr, v_mla_t = _mla_proj(
            lat, row(mla_q_a_norm_g[l]), row(mla_kv_a_norm_g[l]), wq_t, wkn, wv_t,
            q_gains_t, k_gains, cos2, sin2, cos_t, sin_t, tm=tm_mla, seq=seq)

        b3 = lambda a: a.reshape(bsz, seq, a.shape[-1])
        lams = (row(da_lambda_q1[l]), row(da_lambda_k1[l]), row(da_lambda_q2[l]), row(da_lambda_k2[l]))
        y_a = _da_attn(slopes, lams, col_tile(da_subln_g[l], tq), q_da_t, b3(k_da), v_da_t,
                       tq=tq, tk=tk, lambda_init=lambda_init)
        y_b = _mla_attn(qn_t, qr_t, b3(kn), b3(kr), v_mla_t, tq=tq, tk=tk)

        xt = _merge(xt, y_a.reshape(t, -1), y_b.reshape(t, -1), gates,
                    w_branch_a[l].astype(BF16), w_branch_b[l].astype(BF16), w_out[l].astype(BF16),
                    tm=tm, tn=512)

        xt = _ffn(xt, row(ffn2_norm_g[l]), ffn2_w1[l].astype(BF16), ffn2_w3[l].astype(BF16),
                  (0.5 * ffn2_w2[l]).astype(BF16), tm=tm, tf=512)
    return xt.reshape(bsz, seq, d)
```

```python
import functools
import math

import jax
import jax.numpy as jnp
from jax import lax
from jax.experimental import pallas as pl
from jax.experimental.pallas import tpu as pltpu

F32 = jnp.float32
BF16 = jnp.bfloat16

EPS = 1e-6
CHUNK = 64
CHUNK_SHIFT = CHUNK.bit_length() - 1
DA_HEADS = 8
DA_HEAD_DIM = 128
DA_V_DIM = 2 * DA_HEAD_DIM
MLA_HEADS = 16
MLA_Q_RANK = 768
MLA_KV_RANK = 512
MLA_NOPE_DIM = 128
MLA_ROPE_DIM = 64
MLA_V_DIM = 128
MLA_QK_DIM = MLA_NOPE_DIM + MLA_ROPE_DIM
ROPE_THETA = 10000.0
LOG2E = math.log2(math.e)
MASK_VALUE = -1e30

LANES = 128
SUBLANES = 8
MXU_WIDTH = 256
V7X_VMEM_BYTES = 64 * 1024 * 1024
VMEM_LIMIT = V7X_VMEM_BYTES - 8 * 1024 * 1024


def _cparams(sem):
    return pltpu.CompilerParams(dimension_semantics=sem, vmem_limit_bytes=VMEM_LIMIT)


def _rms(x, g):
    ms = jnp.mean(x * x, axis=-1, keepdims=True)
    return x * lax.rsqrt(ms + EPS) * g


def _nt_dot(a, b):
    return lax.dot_general(a, b, (((1,), (1,)), ((), ())), preferred_element_type=F32)


CAST_BLOCK_BYTES = 6 * 1024 * 1024


def _cast_body(w_ref, o_ref, *, scale):
    w = w_ref[...]
    o_ref[...] = (w if scale == 1.0 else w * scale).astype(BF16)


def _to_bf16(w, scale=1.0):
    r, c = w.shape
    tr = r
    while tr % 2 == 0 and tr * c * 4 > CAST_BLOCK_BYTES:
        tr //= 2
    spec = pl.BlockSpec((tr, c), lambda i: (i, 0))
    return pl.pallas_call(
        functools.partial(_cast_body, scale=scale),
        grid=(r // tr,), in_specs=[spec], out_specs=spec,
        out_shape=jax.ShapeDtypeStruct((r, c), BF16),
        compiler_params=_cparams(("parallel",)),
        name="to_bf16",
    )(w)


def _ffn_body(x_ref, g_ref, w1_ref, w3_ref, w2_ref, o_ref, h_ref):
    @pl.when(pl.program_id(1) == 0)
    def _():
        x = x_ref[...]
        h_ref[...] = _rms(x, g_ref[...]).astype(BF16)
        o_ref[...] = x

    h = h_ref[...]
    a = jnp.dot(h, w1_ref[...], preferred_element_type=F32)
    b = jnp.dot(h, w3_ref[...], preferred_element_type=F32)
    u = (a * jax.nn.sigmoid(a) * b).astype(BF16)
    o_ref[...] += jnp.dot(u, w2_ref[...], preferred_element_type=F32)


def _ffn(x, g, w1, w3, w2_half, *, tm, tf):
    t, d = x.shape
    f = w1.shape[1]
    return pl.pallas_call(
        _ffn_body,
        grid=(t // tm, f // tf),
        in_specs=[
            pl.BlockSpec((tm, d), lambda i, j: (i, 0)),
            pl.BlockSpec((1, d), lambda i, j: (0, 0)),
            pl.BlockSpec((d, tf), lambda i, j: (0, j)),
            pl.BlockSpec((d, tf), lambda i, j: (0, j)),
            pl.BlockSpec((tf, d), lambda i, j: (j, 0)),
        ],
        out_specs=pl.BlockSpec((tm, d), lambda i, j: (i, 0)),
        out_shape=jax.ShapeDtypeStruct((t, d), F32),
        scratch_shapes=[pltpu.VMEM((tm, d), BF16)],
        compiler_params=_cparams(("parallel", "arbitrary")),
        name="ffn",
    )(x, g, w1, w3, w2_half)


PROJ_TN = 512
_Q0, _K0, _V0, _C0, _G0, _NJ = 0, 4, 8, 12, 15, 23


def _head_norm(acc, g):
    parts = []
    for c in range(acc.shape[1] // LANES):
        blk = acc[:, c * LANES:(c + 1) * LANES]
        ms = jnp.mean(blk * blk, axis=-1, keepdims=True)
        parts.append(blk * lax.rsqrt(ms + EPS) * g)
    return jnp.concatenate(parts, axis=-1)


def _head_norm_t(acc_t, g_t):
    parts = []
    for c in range(acc_t.shape[0] // LANES):
        blk = acc_t[c * LANES:(c + 1) * LANES, :]
        ms = jnp.mean(blk * blk, axis=0, keepdims=True)
        parts.append(blk * lax.rsqrt(ms + EPS) * g_t)
    return jnp.concatenate(parts, axis=0)


def _proj_body(x_ref, g_ref, wt_ref, wn_ref, qg_ref, kg_ref,
               qt_ref, k_ref, vt_ref, c_ref, gate_ref, h_ref):
    j = pl.program_id(1)

    @pl.when(j == 0)
    def _():
        h_ref[...] = _rms(x_ref[...], g_ref[...]).astype(BF16)

    @pl.when(j < _K0)
    def _():
        qt_ref[...] = _head_norm_t(_nt_dot(wt_ref[...], h_ref[...]), qg_ref[...]).astype(BF16)

    @pl.when((j >= _K0) & (j < _V0))
    def _():
        acc = jnp.dot(h_ref[...], wn_ref[...], preferred_element_type=F32)
        k_ref[...] = _head_norm(acc, kg_ref[...]).astype(BF16)

    @pl.when((j >= _V0) & (j < _C0))
    def _():
        vt_ref[...] = _nt_dot(wt_ref[...], h_ref[...]).astype(BF16)

    @pl.when((j >= _C0) & (j < _G0))
    def _():
        c_ref[...] = jnp.dot(h_ref[...], wn_ref[...], preferred_element_type=F32)

    @pl.when(j >= _G0)
    def _():
        acc = jnp.dot(h_ref[...], wn_ref[...], preferred_element_type=F32)
        gate_ref[...] = jax.nn.sigmoid(acc).astype(BF16)


def _in_proj(x, g, w_t, w_n, qg_t, kg, *, tm):
    t, d = x.shape
    tn = PROJ_TN
    n_q, n_k, n_v, n_c, n_g = _K0 - _Q0, _V0 - _K0, _C0 - _V0, _G0 - _C0, _NJ - _G0

    def wt_idx(i, j):
        return (jnp.where(j < _K0, j, jnp.clip(j - n_k, n_q - 1, n_q + n_v - 1)), 0)

    def wn_idx(i, j):
        return (0, jnp.where(j < _V0, jnp.clip(j - _K0, 0, n_k - 1),
                             jnp.clip(j - _K0 - n_v, n_k - 1, n_k + n_c + n_g - 1)))

    def cols(j0, n):
        return pl.BlockSpec((tm, tn), lambda i, j: (i, jnp.clip(j - j0, 0, n - 1)))

    def rows(j0, n):
        return pl.BlockSpec((tn, tm), lambda i, j: (jnp.clip(j - j0, 0, n - 1), i))

    return pl.pallas_call(
        _proj_body,
        grid=(t // tm, _NJ),
        in_specs=[
            pl.BlockSpec((tm, d), lambda i, j: (i, 0)),
            pl.BlockSpec((1, d), lambda i, j: (0, 0)),
            pl.BlockSpec((tn, d), wt_idx),
            pl.BlockSpec((d, tn), wn_idx),
            pl.BlockSpec((LANES, tm), lambda i, j: (0, 0)),
            pl.BlockSpec((1, LANES), lambda i, j: (0, 0)),
        ],
        out_specs=[rows(_Q0, n_q), cols(_K0, n_k), rows(_V0, n_v), cols(_C0, n_c), cols(_G0, n_g)],
        out_shape=[
            jax.ShapeDtypeStruct((n_q * tn, t), BF16),
            jax.ShapeDtypeStruct((t, n_k * tn), BF16),
            jax.ShapeDtypeStruct((n_v * tn, t), BF16),
            jax.ShapeDtypeStruct((t, n_c * tn), F32),
            jax.ShapeDtypeStruct((t, n_g * tn), BF16),
        ],
        scratch_shapes=[pltpu.VMEM((tm, d), BF16)],
        compiler_params=_cparams(("parallel", "arbitrary")),
        name="in_proj",
    )(x, g, w_t, w_n, qg_t, kg)


def _mla_proj_body(c_ref, qa_g_ref, kva_g_ref, wqt_ref, wkn_ref, wvt_ref,
                   gqn_ref, gqr_ref, gqt_ref, gk_nope_ref, gk_rope_ref, gk_rot_ref,
                   cos_ref, sin_ref, cost_ref, sint_ref,
                   qnt_ref, qrt_ref, kn_ref, kr_ref, vt_ref):
    c = c_ref[...]
    o_kv = MLA_Q_RANK
    o_kr = MLA_Q_RANK + MLA_KV_RANK
    cqn = _rms(c[:, :o_kv], qa_g_ref[...]).astype(BF16)
    ckvn = _rms(c[:, o_kv:o_kr], kva_g_ref[...]).astype(BF16)
    k_rope = c[:, o_kr:o_kr + LANES]
    k_rot = c[:, o_kr + LANES:o_kr + 2 * LANES]
    first = lax.broadcasted_iota(jnp.int32, (1, LANES), 1) < MLA_ROPE_DIM
    k_roped = k_rope * gk_rope_ref[...] * cos_ref[...] + k_rot * gk_rot_ref[...] * sin_ref[...]
    k_rope_sq = 0.5 * jnp.sum(k_rope * k_rope, axis=-1, keepdims=True)
    inv_d = 1.0 / MLA_QK_DIM
    cos_t = cost_ref[...]
    sin_t = sint_ref[...]
    half = MLA_ROPE_DIM

    for p in range(MLA_HEADS // 2):
        qraw = _nt_dot(wqt_ref[p * 4 * LANES:(p + 1) * 4 * LANES, :], cqn)
        na, nb = qraw[:LANES], qraw[LANES:2 * LANES]
        rp, rt = qraw[2 * LANES:3 * LANES], qraw[3 * LANES:]
        rp2 = rp * rp
        sa = jnp.sum(na * na, axis=0, keepdims=True) + jnp.sum(rp2[:half], axis=0, keepdims=True)
        sb = jnp.sum(nb * nb, axis=0, keepdims=True) + jnp.sum(rp2[half:], axis=0, keepdims=True)
        ra = lax.rsqrt(sa * inv_d + EPS)
        rb = lax.rsqrt(sb * inv_d + EPS)
        ra_rows = slice((2 * p) * LANES, (2 * p + 1) * LANES)
        rb_rows = slice((2 * p + 1) * LANES, (2 * p + 2) * LANES)
        qnt_ref[ra_rows, :] = (na * ra * gqn_ref[...]).astype(BF16)
        qnt_ref[rb_rows, :] = (nb * rb * gqn_ref[...]).astype(BF16)
        roped = rp * gqr_ref[...] * cos_t + rt * gqt_ref[...] * sin_t
        qrt_ref[p * LANES:(p + 1) * LANES, :] = jnp.concatenate(
            [roped[:half] * ra, roped[half:] * rb], axis=0).astype(BF16)

        knraw = jnp.dot(ckvn, wkn_ref[:, p * 2 * LANES:(p + 1) * 2 * LANES], preferred_element_type=F32)
        kna, knb = knraw[:, :LANES], knraw[:, LANES:]
        rka = lax.rsqrt((jnp.sum(kna * kna, axis=-1, keepdims=True) + k_rope_sq) * inv_d + EPS)
        rkb = lax.rsqrt((jnp.sum(knb * knb, axis=-1, keepdims=True) + k_rope_sq) * inv_d + EPS)
        kn_ref[:, ra_rows] = (kna * rka * gk_nope_ref[...]).astype(BF16)
        kn_ref[:, rb_rows] = (knb * rkb * gk_nope_ref[...]).astype(BF16)
        kr_ref[:, ra_rows] = jnp.where(first, k_roped * rka, 0.0).astype(BF16)
        kr_ref[:, rb_rows] = jnp.where(first, 0.0, k_roped * rkb).astype(BF16)

        vt_ref[p * 2 * LANES:(p + 1) * 2 * LANES, :] = _nt_dot(
            wvt_ref[p * 2 * LANES:(p + 1) * 2 * LANES, :], ckvn).astype(BF16)


def _mla_proj(c, qa_g, kva_g, wq_t, wkn, wv_t, q_gains_t, k_gains, cos2, sin2, cos_t, sin_t, *, tm, seq):
    t = c.shape[0]
    n_pos = seq // tm
    const = lambda a: pl.BlockSpec(a.shape, lambda i: (0, 0))
    hw = MLA_HEADS * LANES
    tok = lambda w: pl.BlockSpec((tm, w), lambda i: (i, 0))
    tok_t = lambda w: pl.BlockSpec((w, tm), lambda i: (0, i))
    return pl.pallas_call(
        _mla_proj_body,
        grid=(t // tm,),
        in_specs=[tok(c.shape[1]), const(qa_g), const(kva_g), const(wq_t), const(wkn), const(wv_t)]
        + [const(a) for a in q_gains_t] + [const(a) for a in k_gains] + [
            pl.BlockSpec((tm, LANES), lambda i: (i % n_pos, 0)),
            pl.BlockSpec((tm, LANES), lambda i: (i % n_pos, 0)),
            pl.BlockSpec((LANES, tm), lambda i: (0, i % n_pos)),
            pl.BlockSpec((LANES, tm), lambda i: (0, i % n_pos)),
        ],
        out_specs=[tok_t(hw), tok_t(hw // 2), tok(hw), tok(hw), tok_t(hw)],
        out_shape=[
            jax.ShapeDtypeStruct((hw, t), BF16),
            jax.ShapeDtypeStruct((hw // 2, t), BF16),
            jax.ShapeDtypeStruct((t, hw), BF16),
            jax.ShapeDtypeStruct((t, hw), BF16),
            jax.ShapeDtypeStruct((hw, t), BF16),
        ],
        compiler_params=_cparams(("parallel",)),
        name="mla_proj",
    )(c, qa_g, kva_g, wq_t, wkn, wv_t, *q_gains_t, *k_gains, cos2, sin2, cos_t, sin_t)


def _score_pass(k_tile, q_t, tile_spec, tile_off, n_tiles, s_ref):
    tq = q_t.shape[1]
    m8 = None
    for n in range(n_tiles):
        add, keep = tile_spec(n)
        u = jnp.dot(k_tile(n), q_t, preferred_element_type=F32)
        if add is not None:
            u = u + add
        if keep is not None:
            u = jnp.where(keep, u, MASK_VALUE)
        s_ref[n * MXU_WIDTH:(n + 1) * MXU_WIDTH, :] = u
        t = jnp.max(u.reshape(MXU_WIDTH // SUBLANES, SUBLANES, tq), axis=0)
        off = tile_off(n)
        if off is not None:
            t = t + off
        m8 = t if m8 is None else jnp.maximum(m8, t)
    return jnp.max(m8, axis=0, keepdims=True)


def _prob_pass(s_ref, p_ref, m, l, m_blk, tile_off, n_tiles):
    tq = s_ref.shape[1]
    m_new = jnp.maximum(m, m_blk)
    alpha = jnp.exp2(m - m_new)
    ps8 = None
    for n in range(n_tiles):
        rows = slice(n * MXU_WIDTH, (n + 1) * MXU_WIDTH)
        off = tile_off(n)
        ref = m_new if off is None else m_new - off
        p = jnp.exp2(s_ref[rows, :] - ref)
        t = jnp.sum(p.reshape(MXU_WIDTH // SUBLANES, SUBLANES, tq), axis=0)
        ps8 = t if ps8 is None else ps8 + t
        p_ref[rows, :] = p.astype(BF16)
    l_new = alpha * l + jnp.sum(ps8, axis=0, keepdims=True)
    return m_new, l_new, alpha


def _pipelined_key_blocks(n_full, subs, tq):
    neg = jnp.full((1, tq), MASK_VALUE, F32)
    zero = jnp.zeros((1, tq), F32)
    init = (neg, zero, neg, zero)

    @pl.when(n_full == 0)
    def _():
        maxima = [score(0, r) for r, (score, _, _) in enumerate(subs)]
        for r, (_, prob_pv, finalize) in enumerate(subs):
            finalize(prob_pv(0, r, init, maxima[r]))

    @pl.when(n_full > 0)
    def _():
        mx = subs[0][0](0, None)
        for r, (score, prob_pv, finalize) in enumerate(subs):
            def body(j, carry, score=score, prob_pv=prob_pv):
                state, mx = carry
                state = prob_pv(j, None, state, mx)
                return state, score(j + 1, None)

            state, mx = lax.fori_loop(0, n_full - 1, body, (init, mx))
            state = prob_pv(n_full - 1, None, state, mx)
            mx_last = score(n_full, r)
            if r + 1 < len(subs):
                mx = subs[r + 1][0](0, None)
            finalize(prob_pv(n_full, r, state, mx_last))


def _block_tiles(r, tq, tk):
    per_q = tq // MXU_WIDTH
    if r is None:
        return tk // MXU_WIDTH, tk // MXU_WIDTH
    return r * per_q, (r + 1) * per_q


def _own_chunk_mask(key, qry, dq):
    return ((key - dq) >> CHUNK_SHIFT) <= (qry >> CHUNK_SHIFT)


def _da_attn_body(slopes_ref, lq1_ref, lk1_ref, lq2_ref, lk2_ref, sg_ref,
                  qt_ref, k_ref, vt_ref, o_ref, acc_ref, s_ref, p_ref, own_ref,
                  *, tq, tk, lambda_init):
    h = pl.program_id(1)
    n_full = pl.program_id(2)
    slope = slopes_ref[h]
    key = lax.broadcasted_iota(jnp.int32, (MXU_WIDTH, tq), 0)
    key_bias = slope * key.astype(F32)
    acc_ref[...] = jnp.zeros_like(acc_ref)

    @pl.when(n_full == 0)
    def _():
        qry = lax.broadcasted_iota(jnp.int32, (MXU_WIDTH, tq), 1)
        mirror_bias = slope * (2 * qry - key).astype(F32)
        for m in range(tq // MXU_WIDTH):
            dq = -m * MXU_WIDTH
            bias = jnp.minimum(key_bias - slope * dq, mirror_bias + slope * dq)
            own_ref[m] = jnp.where(_own_chunk_mask(key, qry, dq), bias, MASK_VALUE)

    subs = [_da_sub_block(sub, n_full, slope, key_bias, own_ref,
                          lq1_ref, lk1_ref, lq2_ref, lk2_ref, sg_ref, qt_ref, k_ref, vt_ref, o_ref,
                          acc_ref, s_ref, p_ref, tq=tq, tk=tk, lambda_init=lambda_init)
            for sub in range(tk // tq)]
    _pipelined_key_blocks(n_full, subs, tq)


def _da_sub_block(sub, n_full, slope, key_bias, own_ref,
                  lq1_ref, lk1_ref, lq2_ref, lk2_ref, sg_ref, qt_ref, k_ref, vt_ref, o_ref,
                  acc_ref, s_ref, p_ref, *, tq, tk, lambda_init):
    q_cols = slice(sub * tq, (sub + 1) * tq)
    q1_t = qt_ref[:DA_HEAD_DIM, q_cols]
    q2_t = qt_ref[DA_HEAD_DIM:, q_cols]
    q0 = n_full * tk + sub * tq
    acc1_ref, acc2_ref = acc_ref.at[sub, 0], acc_ref.at[sub, 1]
    s_ref, p_ref = s_ref.at[sub], p_ref.at[sub]

    def tile_plan(j, r):
        first_masked, n_tiles = _block_tiles(r, tq, tk)
        specs = {}

        def tile_spec(n):
            if n not in specs:
                specs[n] = (key_bias if n < first_masked else own_ref[n - first_masked], None)
            return specs[n]

        def tile_off(n):
            if n >= first_masked:
                return None
            if r is None:
                return slope * (j * tk + n * MXU_WIDTH - q0).astype(F32)
            return slope * float(n * MXU_WIDTH - r * tq)

        return tile_spec, tile_off, n_tiles

    def score(j, r):
        k0 = pl.multiple_of(j * tk, tk)
        tile_spec, tile_off, n_tiles = tile_plan(j, r)

        def k_tile(lo):
            return lambda n: k_ref[0, pl.ds(pl.multiple_of(k0 + n * MXU_WIDTH, MXU_WIDTH), MXU_WIDTH),
                                   lo:lo + DA_HEAD_DIM]

        return (_score_pass(k_tile(0), q1_t, tile_spec, tile_off, n_tiles, s_ref.at[0]),
                _score_pass(k_tile(DA_HEAD_DIM), q2_t, tile_spec, tile_off, n_tiles, s_ref.at[1]))

    def prob_pv(j, r, state, mx):
        m1, l1, m2, l2 = state
        _, tile_off, n_tiles = tile_plan(j, r)
        m1, l1, a1 = _prob_pass(s_ref.at[0], p_ref.at[0], m1, l1, mx[0], tile_off, n_tiles)
        m2, l2, a2 = _prob_pass(s_ref.at[1], p_ref.at[1], m2, l2, mx[1], tile_off, n_tiles)
        keys = n_tiles * MXU_WIDTH
        vb = vt_ref[:, pl.ds(pl.multiple_of(j * tk, tk), keys)]
        acc1_ref[...] = a1 * acc1_ref[...] + jnp.dot(vb, p_ref[0, :keys, :], preferred_element_type=F32)
        acc2_ref[...] = a2 * acc2_ref[...] + jnp.dot(vb, p_ref[1, :keys, :], preferred_element_type=F32)
        return m1, l1, m2, l2

    def finalize(state):
        _, l1, _, l2 = state
        lam = (jnp.exp(jnp.sum(lq1_ref[...] * lk1_ref[...], axis=-1, keepdims=True))
               - jnp.exp(jnp.sum(lq2_ref[...] * lk2_ref[...], axis=-1, keepdims=True))
               + lambda_init)
        o = acc1_ref[...] * (1.0 / l1) - lam * (acc2_ref[...] * (1.0 / l2))
        ms = jnp.mean(o * o, axis=0, keepdims=True)
        y = o * lax.rsqrt(ms + EPS) * sg_ref[...] * (1.0 - lambda_init)
        o_ref[0, q_cols, :] = y.T.astype(BF16)

    return score, prob_pv, finalize


def _da_attn(slopes, lams, sg_t, q_t, k, v_t, *, tq, tk, lambda_init):
    b, s, _ = k.shape
    w = DA_V_DIM
    nq = s // tk
    n_sub = tk // tq
    vec = pl.BlockSpec((1, DA_HEAD_DIM), lambda bi, h, i: (0, 0))
    return pl.pallas_call(
        functools.partial(_da_attn_body, tq=tq, tk=tk, lambda_init=lambda_init),
        grid=(b, DA_HEADS, nq),
        in_specs=[
            pl.BlockSpec(memory_space=pltpu.SMEM),
            vec, vec, vec, vec,
            pl.BlockSpec((w, tq), lambda bi, h, i: (0, 0)),
            pl.BlockSpec((w, tk), lambda bi, h, i: (h, bi * nq + i)),
            pl.BlockSpec((1, s, w), lambda bi, h, i: (bi, 0, h)),
            pl.BlockSpec((w, s), lambda bi, h, i: (h, bi)),
        ],
        out_specs=pl.BlockSpec((1, tk, w), lambda bi, h, i: (bi, i, h)),
        out_shape=jax.ShapeDtypeStruct((b, s, DA_HEADS * w), BF16),
        scratch_shapes=[pltpu.VMEM((n_sub, 2, w, tq), F32),
                        pltpu.VMEM((n_sub, 2, tk, tq), F32), pltpu.VMEM((n_sub, 2, tk, tq), BF16),
                        pltpu.VMEM((tq // MXU_WIDTH, MXU_WIDTH, tq), F32)],
        compiler_params=_cparams(("parallel", "parallel", "arbitrary")),
        name="da_attn",
    )(slopes, *lams, sg_t, q_t, k, v_t)


def _mla_attn_body(qnt_ref, qrt_ref, kn_ref, kr_ref, vt_ref, o_ref, acc_ref, s_ref, p_ref, own_ref,
                   *, tq, tk):
    n_full = pl.program_id(2)
    acc_ref[...] = jnp.zeros_like(acc_ref)

    @pl.when(n_full == 0)
    def _():
        key = lax.broadcasted_iota(jnp.int32, (MXU_WIDTH, tq), 0)
        qry = lax.broadcasted_iota(jnp.int32, (MXU_WIDTH, tq), 1)
        for m in range(tq // MXU_WIDTH):
            own_ref[m] = jnp.where(_own_chunk_mask(key, qry, -m * MXU_WIDTH), 0.0, MASK_VALUE)

    subs = [_mla_sub_block(sub, own_ref, qnt_ref, qrt_ref, kn_ref, kr_ref, vt_ref, o_ref,
                           acc_ref, s_ref, p_ref, tq=tq, tk=tk) for sub in range(tk // tq)]
    _pipelined_key_blocks(n_full, subs, tq)


def _mla_sub_block(sub, own_ref, qnt_ref, qrt_ref, kn_ref, kr_ref, vt_ref, o_ref,
                   acc_ref, s_ref, p_ref, *, tq, tk):
    q_cols = slice(sub * tq, (sub + 1) * tq)
    qr_t = qrt_ref[:, q_cols]
    qa_t = jnp.concatenate([qnt_ref[:LANES, q_cols], qr_t], axis=0)
    qb_t = jnp.concatenate([qnt_ref[LANES:, q_cols], qr_t], axis=0)
    acc_a_ref, acc_b_ref = acc_ref.at[sub, 0], acc_ref.at[sub, 1]
    s_ref, p_ref = s_ref.at[sub], p_ref.at[sub]
    no_off = lambda n: None

    def score(j, r):
        k0 = pl.multiple_of(j * tk, tk)
        first_masked, n_tiles = _block_tiles(r, tq, tk)
        specs = {}

        def tile_spec(n):
            if n not in specs:
                specs[n] = (None if n < first_masked else own_ref[n - first_masked], None)
            return specs[n]

        def k_tile(lo):
            def tile(n):
                rows = pl.ds(pl.multiple_of(k0 + n * MXU_WIDTH, MXU_WIDTH), MXU_WIDTH)
                return jnp.concatenate([kn_ref[0, rows, lo:lo + LANES],
                                        kr_ref[0, rows, lo:lo + LANES]], axis=-1)
            return tile

        return (_score_pass(k_tile(0), qa_t, tile_spec, no_off, n_tiles, s_ref.at[0]),
                _score_pass(k_tile(LANES), qb_t, tile_spec, no_off, n_tiles, s_ref.at[1]))

    def prob_pv(j, r, state, mx):
        ma, la, mb, lb = state
        _, n_tiles = _block_tiles(r, tq, tk)
        ma, la, aa = _prob_pass(s_ref.at[0], p_ref.at[0], ma, la, mx[0], no_off, n_tiles)
        mb, lb, ab = _prob_pass(s_ref.at[1], p_ref.at[1], mb, lb, mx[1], no_off, n_tiles)
        keys = n_tiles * MXU_WIDTH
        cols = pl.ds(pl.multiple_of(j * tk, tk), keys)
        acc_a_ref[...] = aa * acc_a_ref[...] + jnp.dot(vt_ref[:LANES, cols], p_ref[0, :keys, :],
                                                      preferred_element_type=F32)
        acc_b_ref[...] = ab * acc_b_ref[...] + jnp.dot(vt_ref[LANES:, cols], p_ref[1, :keys, :],
                                                      preferred_element_type=F32)
        return ma, la, mb, lb

    def finalize(state):
        _, la, _, lb = state
        o_t = jnp.concatenate([acc_a_ref[...] * (1.0 / la), acc_b_ref[...] * (1.0 / lb)], axis=0)
        o_ref[0, q_cols, :] = o_t.T.astype(BF16)

    return score, prob_pv, finalize


def _mla_attn(qn_t, qr_t, kn, kr, v_t, *, tq, tk):
    b, s, _ = kn.shape
    w = 2 * LANES
    nq = s // tk
    n_sub = tk // tq
    kv_spec = pl.BlockSpec((1, s, w), lambda bi, h, i: (bi, 0, h))
    return pl.pallas_call(
        functools.partial(_mla_attn_body, tq=tq, tk=tk),
        grid=(b, MLA_HEADS // 2, nq),
        in_specs=[
            pl.BlockSpec((w, tk), lambda bi, h, i: (h, bi * nq + i)),
            pl.BlockSpec((LANES, tk), lambda bi, h, i: (h, bi * nq + i)),
            kv_spec, kv_spec,
            pl.BlockSpec((w, s), lambda bi, h, i: (h, bi)),
        ],
        out_specs=pl.BlockSpec((1, tk, w), lambda bi, h, i: (bi, i, h)),
        out_shape=jax.ShapeDtypeStruct((b, s, MLA_HEADS * LANES), BF16),
        scratch_shapes=[pltpu.VMEM((n_sub, 2, LANES, tq), F32),
                        pltpu.VMEM((n_sub, 2, tk, tq), F32), pltpu.VMEM((n_sub, 2, tk, tq), BF16),
                        pltpu.VMEM((tq // MXU_WIDTH, MXU_WIDTH, tq), F32)],
        compiler_params=_cparams(("parallel", "parallel", "arbitrary")),
        name="mla_attn",
    )(qn_t, qr_t, kn, kr, v_t)


def _merge_body(x_ref, ya_ref, yb_ref, gate_a_ref, gate_b_ref, wa_ref, wb_ref, wo_ref, o_ref):
    @pl.when(pl.program_id(1) == 0)
    def _():
        o_ref[...] = x_ref[...]

    a = jnp.dot(ya_ref[...], wa_ref[...], preferred_element_type=F32)
    b = jnp.dot(yb_ref[...], wb_ref[...], preferred_element_type=F32)
    m = (gate_a_ref[...].astype(F32) * a + gate_b_ref[...].astype(F32) * b).astype(BF16)
    o_ref[...] += jnp.dot(m, wo_ref[...], preferred_element_type=F32)


def _merge(x, ya, yb, gates, wa, wb, wo, *, tm, tn):
    t, d = x.shape
    nj = d // tn
    tok = pl.BlockSpec((tm, d), lambda i, j: (i, 0))
    return pl.pallas_call(
        _merge_body,
        grid=(t // tm, nj),
        in_specs=[
            tok, tok, tok,
            pl.BlockSpec((tm, tn), lambda i, j: (i, j)),
            pl.BlockSpec((tm, tn), lambda i, j: (i, j + nj)),
            pl.BlockSpec((d, tn), lambda i, j: (0, j)),
            pl.BlockSpec((d, tn), lambda i, j: (0, j)),
            pl.BlockSpec((tn, d), lambda i, j: (j, 0)),
        ],
        out_specs=tok,
        out_shape=jax.ShapeDtypeStruct((t, d), F32),
        compiler_params=_cparams(("parallel", "arbitrary")),
        name="merge",
    )(x, ya, yb, gates, gates, wa, wb, wo)


def _rot_cols(w):
    half = w.shape[-1] // 2
    return jnp.concatenate([-w[..., half:], w[..., :half]], axis=-1)


def _swap_halves(g):
    half = g.shape[-1] // 2
    return jnp.concatenate([g[..., half:], g[..., :half]], axis=-1)


def _pick_tile(n, pref):
    return pref if n % pref == 0 else n


def kernel(x, ffn1_norm_g, ffn1_w1, ffn1_w3, ffn1_w2, mix_norm_g, w_in, da_q_norm_g, da_k_norm_g, da_lambda_q1, da_lambda_k1, da_lambda_q2, da_lambda_k2, da_subln_g, mla_q_a_norm_g, mla_w_qb, mla_kv_a_norm_g, mla_w_kvb, mla_q_norm_g, mla_k_norm_g, w_branch_a, w_branch_b, w_out, ffn2_norm_g, ffn2_w1, ffn2_w3, ffn2_w2):
    bsz, seq, d = x.shape
    t = bsz * seq
    depth = ffn1_norm_g.shape[0]
    tm = _pick_tile(t, 512)
    tm_proj = _pick_tile(t, 1024)
    tm_mla = _pick_tile(seq, 256)
    tq = _pick_tile(seq, 512)
    tk = _pick_tile(seq, 1024)
    row = lambda v: v.reshape(1, -1).astype(F32)
    col_tile = lambda v, n: jnp.broadcast_to(v.astype(F32)[:, None], (v.shape[0], n))

    inv = ROPE_THETA ** (-jnp.arange(0, MLA_ROPE_DIM, 2, dtype=F32) / MLA_ROPE_DIM)
    ang = jnp.arange(seq, dtype=F32)[:, None] * inv[None, :]
    cos2 = jnp.tile(jnp.cos(ang), (1, 4))
    sin2 = jnp.tile(jnp.sin(ang), (1, 4))
    cos_t, sin_t = cos2.T, sin2.T
    slopes = (2.0 ** (-8.0 * jnp.arange(1, DA_HEADS + 1, dtype=F32) / DA_HEADS)) * LOG2E

    xt = x.reshape(t, d)
    for l in range(depth):
        lambda_init = 0.8 - 0.6 * math.exp(-0.3 * l)

        xt = _ffn(xt, row(ffn1_norm_g[l]), _to_bf16(ffn1_w1[l]), _to_bf16(ffn1_w3[l]),
                  _to_bf16(ffn1_w2[l], 0.5), tm=tm, tf=512)

        w = w_in[l]
        o = [0]
        for n in (2 * DA_HEADS * DA_HEAD_DIM, 2 * DA_HEADS * DA_HEAD_DIM, DA_HEADS * DA_V_DIM,
                  MLA_Q_RANK, MLA_KV_RANK, MLA_ROPE_DIM, d, d):
            o.append(o[-1] + n)
        w_kr = w[:, o[5]:o[6]]
        w_kr_rot = _rot_cols(w_kr)
        w_t = jnp.concatenate([w[:, o[0]:o[1]], w[:, o[2]:o[3]]], axis=1).T.astype(BF16)
        w_n = jnp.concatenate(
            [w[:, o[1]:o[2]], w[:, o[3]:o[5]], w_kr, w_kr, w_kr_rot, w_kr_rot, w[:, o[6]:]],
            axis=1).astype(BF16)
        q_scale = DA_HEAD_DIM ** -0.5 * LOG2E
        q_da_t, k_da, v_da_t, lat, gates = _in_proj(
            xt, row(mix_norm_g[l]), w_t, w_n, col_tile(da_q_norm_g[l] * q_scale, tm_proj),
            row(da_k_norm_g[l]), tm=tm_proj)

        wq = mla_w_qb[l].reshape(MLA_Q_RANK, MLA_HEADS, MLA_QK_DIM)
        wq_nope = wq[:, :, :MLA_NOPE_DIM].reshape(MLA_Q_RANK, MLA_HEADS // 2, 2 * LANES)
        wq_rope = wq[:, :, MLA_NOPE_DIM:]
        wq_rot = _rot_cols(wq_rope).reshape(MLA_Q_RANK, MLA_HEADS // 2, LANES)
        wq_rope = wq_rope.reshape(MLA_Q_RANK, MLA_HEADS // 2, LANES)
        wq_t = jnp.concatenate([wq_nope, wq_rope, wq_rot], axis=-1).reshape(
            MLA_Q_RANK, MLA_HEADS * 2 * LANES).T.astype(BF16)
        wkv = mla_w_kvb[l].reshape(MLA_KV_RANK, MLA_HEADS, MLA_NOPE_DIM + MLA_V_DIM)
        wkn = wkv[:, :, :MLA_NOPE_DIM].reshape(MLA_KV_RANK, MLA_HEADS * MLA_NOPE_DIM).astype(BF16)
        wv_t = wkv[:, :, MLA_NOPE_DIM:].reshape(MLA_KV_RANK, MLA_HEADS * MLA_V_DIM).T.astype(BF16)
        mla_scale = MLA_QK_DIM ** -0.5 * LOG2E
        gq, gk = mla_q_norm_g[l], mla_k_norm_g[l]
        q_gains_t = (
            col_tile(gq[:MLA_NOPE_DIM] * mla_scale, tm_mla),
            col_tile(jnp.tile(gq[MLA_NOPE_DIM:], 2) * mla_scale, tm_mla),
            col_tile(jnp.tile(_swap_halves(gq[MLA_NOPE_DIM:]), 2) * mla_scale, tm_mla),
        )
        k_gains = (
            row(gk[:MLA_NOPE_DIM]),
            row(jnp.tile(gk[MLA_NOPE_DIM:], 2)),
            row(jnp.tile(_swap_halves(gk[MLA_NOPE_DIM:]), 2)),
        )
        qn_t, qr_t, kn, kr, v_mla_t = _mla_proj(
            lat, row(mla_q_a_norm_g[l]), row(mla_kv_a_norm_g[l]), wq_t, wkn, wv_t,
            q_gains_t, k_gains, cos2, sin2, cos_t, sin_t, tm=tm_mla, seq=seq)

        b3 = lambda a: a.reshape(bsz, seq, a.shape[-1])
        lams = (row(da_lambda_q1[l]), row(da_lambda_k1[l]), row(da_lambda_q2[l]), row(da_lambda_k2[l]))
        y_a = _da_attn(slopes, lams, col_tile(da_subln_g[l], tq), q_da_t, b3(k_da), v_da_t,
                       tq=tq, tk=tk, lambda_init=lambda_init)
        y_b = _mla_attn(qn_t, qr_t, b3(kn), b3(kr), v_mla_t, tq=tq, tk=tk)

        xt = _merge(xt, y_a.reshape(t, -1), y_b.reshape(t, -1), gates,
                    _to_bf16(w_branch_a[l]), _to_bf16(w_branch_b[l]), _to_bf16(w_out[l]),
                    tm=tm, tn=512)

        xt = _ffn(xt, row(ffn2_norm_g[l]), _to_bf16(ffn2_w1[l]), _to_bf16(ffn2_w3[l]),
                  _to_bf16(ffn2_w2[l], 0.5), tm=tm, tf=512)
    return xt.reshape(bsz, seq, d)
```

```python
import functools
import math

import jax
import jax.numpy as jnp
from jax import lax
from jax.experimental import pallas as pl
from jax.experimental.pallas import tpu as pltpu

F32 = jnp.float32
BF16 = jnp.bfloat16

EPS = 1e-6
CHUNK = 64
CHUNK_SHIFT = CHUNK.bit_length() - 1
DA_HEADS = 8
DA_HEAD_DIM = 128
DA_V_DIM = 2 * DA_HEAD_DIM
MLA_HEADS = 16
MLA_Q_RANK = 768
MLA_KV_RANK = 512
MLA_NOPE_DIM = 128
MLA_ROPE_DIM = 64
MLA_V_DIM = 128
MLA_QK_DIM = MLA_NOPE_DIM + MLA_ROPE_DIM
ROPE_THETA = 10000.0
LOG2E = math.log2(math.e)
MASK_VALUE = -1e30

LANES = 128
SUBLANES = 8
MXU_WIDTH = 256
V7X_VMEM_BYTES = 64 * 1024 * 1024
VMEM_LIMIT = V7X_VMEM_BYTES - 8 * 1024 * 1024


def _cparams(sem):
    return pltpu.CompilerParams(dimension_semantics=sem, vmem_limit_bytes=VMEM_LIMIT)


def _rms(x, g):
    ms = jnp.mean(x * x, axis=-1, keepdims=True)
    return x * lax.rsqrt(ms + EPS) * g


def _tn_dot(w, h):
    return lax.dot_general(w, h, (((0,), (1,)), ((), ())), preferred_element_type=F32)


def _ffn_body(x_ref, g_ref, w1_ref, w3_ref, w2_ref, o_ref, h_ref):
    @pl.when(pl.program_id(1) == 0)
    def _():
        x = x_ref[...]
        h_ref[...] = _rms(x, g_ref[...]).astype(BF16)
        o_ref[...] = x

    h = h_ref[...]
    a = jnp.dot(h, w1_ref[...], preferred_element_type=F32)
    b = jnp.dot(h, w3_ref[...], preferred_element_type=F32)
    u = (a * jax.nn.sigmoid(a) * b).astype(BF16)
    o_ref[...] += jnp.dot(u, w2_ref[...], preferred_element_type=F32)


def _ffn(x, g, w1, w3, w2_half, *, tm, tf):
    t, d = x.shape
    f = w1.shape[1]
    return pl.pallas_call(
        _ffn_body,
        grid=(t // tm, f // tf),
        in_specs=[
            pl.BlockSpec((tm, d), lambda i, j: (i, 0)),
            pl.BlockSpec((1, d), lambda i, j: (0, 0)),
            pl.BlockSpec((d, tf), lambda i, j: (0, j)),
            pl.BlockSpec((d, tf), lambda i, j: (0, j)),
            pl.BlockSpec((tf, d), lambda i, j: (j, 0)),
        ],
        out_specs=pl.BlockSpec((tm, d), lambda i, j: (i, 0)),
        out_shape=jax.ShapeDtypeStruct((t, d), F32),
        scratch_shapes=[pltpu.VMEM((tm, d), BF16)],
        compiler_params=_cparams(("parallel", "arbitrary")),
        name="ffn",
    )(x, g, w1, w3, w2_half)


PROJ_TN = 512
_Q0, _K0, _V0, _C0, _G0, _NJ = 0, 4, 8, 12, 15, 23


def _head_norm(acc, g):
    parts = []
    for c in range(acc.shape[1] // LANES):
        blk = acc[:, c * LANES:(c + 1) * LANES]
        ms = jnp.mean(blk * blk, axis=-1, keepdims=True)
        parts.append(blk * lax.rsqrt(ms + EPS) * g)
    return jnp.concatenate(parts, axis=-1)


def _head_norm_t(acc_t, g_t):
    parts = []
    for c in range(acc_t.shape[0] // LANES):
        blk = acc_t[c * LANES:(c + 1) * LANES, :]
        ms = jnp.mean(blk * blk, axis=0, keepdims=True)
        parts.append(blk * lax.rsqrt(ms + EPS) * g_t)
    return jnp.concatenate(parts, axis=0)


def _proj_body(x_ref, g_ref, wqkv_ref, wlat_ref, wgate_ref, qg_ref, kg_ref,
               qt_ref, k_ref, vt_ref, c_ref, gate_ref, h_ref):
    j = pl.program_id(1)

    @pl.when(j == 0)
    def _():
        h_ref[...] = _rms(x_ref[...], g_ref[...]).astype(BF16)

    @pl.when(j < _K0)
    def _():
        qt_ref[...] = _head_norm_t(_tn_dot(wqkv_ref[...], h_ref[...]), qg_ref[...]).astype(BF16)

    @pl.when((j >= _K0) & (j < _V0))
    def _():
        acc = jnp.dot(h_ref[...], wqkv_ref[...], preferred_element_type=F32)
        k_ref[...] = _head_norm(acc, kg_ref[...]).astype(BF16)

    @pl.when((j >= _V0) & (j < _C0))
    def _():
        vt_ref[...] = _tn_dot(wqkv_ref[...], h_ref[...]).astype(BF16)

    @pl.when((j >= _C0) & (j < _G0))
    def _():
        c_ref[...] = jnp.dot(h_ref[...], wlat_ref[...], preferred_element_type=F32)

    @pl.when(j >= _G0)
    def _():
        acc = jnp.dot(h_ref[...], wgate_ref[...], preferred_element_type=F32)
        gate_ref[...] = jax.nn.sigmoid(acc).astype(BF16)


def _in_proj(x, g, w_qkv, w_lat, w_gate, qg_t, kg, *, tm):
    t, d = x.shape
    tn = PROJ_TN
    n_q, n_k, n_v, n_c, n_g = _K0 - _Q0, _V0 - _K0, _C0 - _V0, _G0 - _C0, _NJ - _G0

    def w_cols(j0, n):
        return pl.BlockSpec((d, tn), lambda i, j: (0, jnp.clip(j - j0, 0, n - 1)))

    def cols(j0, n):
        return pl.BlockSpec((tm, tn), lambda i, j: (i, jnp.clip(j - j0, 0, n - 1)))

    def rows(j0, n):
        return pl.BlockSpec((tn, tm), lambda i, j: (jnp.clip(j - j0, 0, n - 1), i))

    return pl.pallas_call(
        _proj_body,
        grid=(t // tm, _NJ),
        in_specs=[
            pl.BlockSpec((tm, d), lambda i, j: (i, 0)),
            pl.BlockSpec((1, d), lambda i, j: (0, 0)),
            w_cols(_Q0, n_q + n_k + n_v), w_cols(_C0, n_c), w_cols(_G0, n_g),
            pl.BlockSpec((LANES, tm), lambda i, j: (0, 0)),
            pl.BlockSpec((1, LANES), lambda i, j: (0, 0)),
        ],
        out_specs=[rows(_Q0, n_q), cols(_K0, n_k), rows(_V0, n_v), cols(_C0, n_c), cols(_G0, n_g)],
        out_shape=[
            jax.ShapeDtypeStruct((n_q * tn, t), BF16),
            jax.ShapeDtypeStruct((t, n_k * tn), BF16),
            jax.ShapeDtypeStruct((n_v * tn, t), BF16),
            jax.ShapeDtypeStruct((t, n_c * tn), F32),
            jax.ShapeDtypeStruct((t, n_g * tn), BF16),
        ],
        scratch_shapes=[pltpu.VMEM((tm, d), BF16)],
        compiler_params=_cparams(("parallel", "arbitrary")),
        name="in_proj",
    )(x, g, w_qkv, w_lat, w_gate, qg_t, kg)


def _mla_proj_body(c_ref, qa_g_ref, kva_g_ref, wq_ref, wkn_ref, wv_ref,
                   gqn_ref, gqr_ref, gqt_ref, gk_nope_ref, gk_rope_ref, gk_rot_ref,
                   cos_ref, sin_ref, cost_ref, sint_ref,
                   qnt_ref, qrt_ref, kn_ref, kr_ref, vt_ref):
    c = c_ref[...]
    o_kv = MLA_Q_RANK
    o_kr = MLA_Q_RANK + MLA_KV_RANK
    cqn = _rms(c[:, :o_kv], qa_g_ref[...]).astype(BF16)
    ckvn = _rms(c[:, o_kv:o_kr], kva_g_ref[...]).astype(BF16)
    k_rope = c[:, o_kr:o_kr + LANES]
    k_rot = c[:, o_kr + LANES:o_kr + 2 * LANES]
    first = lax.broadcasted_iota(jnp.int32, (1, LANES), 1) < MLA_ROPE_DIM
    k_roped = k_rope * gk_rope_ref[...] * cos_ref[...] + k_rot * gk_rot_ref[...] * sin_ref[...]
    k_rope_sq = 0.5 * jnp.sum(k_rope * k_rope, axis=-1, keepdims=True)
    inv_d = 1.0 / MLA_QK_DIM
    cos_t = cost_ref[...]
    sin_t = sint_ref[...]
    half = MLA_ROPE_DIM

    for p in range(MLA_HEADS // 2):
        qraw = _tn_dot(wq_ref[:, p * 4 * LANES:(p + 1) * 4 * LANES], cqn)
        na, nb = qraw[:LANES], qraw[LANES:2 * LANES]
        rp, rt = qraw[2 * LANES:3 * LANES], qraw[3 * LANES:]
        rp2 = rp * rp
        sa = jnp.sum(na * na, axis=0, keepdims=True) + jnp.sum(rp2[:half], axis=0, keepdims=True)
        sb = jnp.sum(nb * nb, axis=0, keepdims=True) + jnp.sum(rp2[half:], axis=0, keepdims=True)
        ra = lax.rsqrt(sa * inv_d + EPS)
        rb = lax.rsqrt(sb * inv_d + EPS)
        ra_rows = slice((2 * p) * LANES, (2 * p + 1) * LANES)
        rb_rows = slice((2 * p + 1) * LANES, (2 * p + 2) * LANES)
        qnt_ref[ra_rows, :] = (na * ra * gqn_ref[...]).astype(BF16)
        qnt_ref[rb_rows, :] = (nb * rb * gqn_ref[...]).astype(BF16)
        roped = rp * gqr_ref[...] * cos_t + rt * gqt_ref[...] * sin_t
        qrt_ref[p * LANES:(p + 1) * LANES, :] = jnp.concatenate(
            [roped[:half] * ra, roped[half:] * rb], axis=0).astype(BF16)

        knraw = jnp.dot(ckvn, wkn_ref[:, p * 2 * LANES:(p + 1) * 2 * LANES], preferred_element_type=F32)
        kna, knb = knraw[:, :LANES], knraw[:, LANES:]
        rka = lax.rsqrt((jnp.sum(kna * kna, axis=-1, keepdims=True) + k_rope_sq) * inv_d + EPS)
        rkb = lax.rsqrt((jnp.sum(knb * knb, axis=-1, keepdims=True) + k_rope_sq) * inv_d + EPS)
        kn_ref[:, ra_rows] = (kna * rka * gk_nope_ref[...]).astype(BF16)
        kn_ref[:, rb_rows] = (knb * rkb * gk_nope_ref[...]).astype(BF16)
        kr_ref[:, ra_rows] = jnp.where(first, k_roped * rka, 0.0).astype(BF16)
        kr_ref[:, rb_rows] = jnp.where(first, 0.0, k_roped * rkb).astype(BF16)

        vt_ref[p * 2 * LANES:(p + 1) * 2 * LANES, :] = _tn_dot(
            wv_ref[:, p * 2 * LANES:(p + 1) * 2 * LANES], ckvn).astype(BF16)


def _mla_proj(c, qa_g, kva_g, wq, wkn, wv, q_gains_t, k_gains, cos2, sin2, cos_t, sin_t, *, tm, seq):
    t = c.shape[0]
    n_pos = seq // tm
    const = lambda a: pl.BlockSpec(a.shape, lambda i: (0, 0))
    hw = MLA_HEADS * LANES
    tok = lambda w: pl.BlockSpec((tm, w), lambda i: (i, 0))
    tok_t = lambda w: pl.BlockSpec((w, tm), lambda i: (0, i))
    return pl.pallas_call(
        _mla_proj_body,
        grid=(t // tm,),
        in_specs=[tok(c.shape[1]), const(qa_g), const(kva_g), const(wq), const(wkn), const(wv)]
        + [const(a) for a in q_gains_t] + [const(a) for a in k_gains] + [
            pl.BlockSpec((tm, LANES), lambda i: (i % n_pos, 0)),
            pl.BlockSpec((tm, LANES), lambda i: (i % n_pos, 0)),
            pl.BlockSpec((LANES, tm), lambda i: (0, i % n_pos)),
            pl.BlockSpec((LANES, tm), lambda i: (0, i % n_pos)),
        ],
        out_specs=[tok_t(hw), tok_t(hw // 2), tok(hw), tok(hw), tok_t(hw)],
        out_shape=[
            jax.ShapeDtypeStruct((hw, t), BF16),
            jax.ShapeDtypeStruct((hw // 2, t), BF16),
            jax.ShapeDtypeStruct((t, hw), BF16),
            jax.ShapeDtypeStruct((t, hw), BF16),
            jax.ShapeDtypeStruct((hw, t), BF16),
        ],
        compiler_params=_cparams(("parallel",)),
        name="mla_proj",
    )(c, qa_g, kva_g, wq, wkn, wv, *q_gains_t, *k_gains, cos2, sin2, cos_t, sin_t)


def _score_pass(k_tile, q_t, tile_spec, tile_off, n_tiles, s_ref):
    tq = q_t.shape[1]
    m8 = None
    for n in range(n_tiles):
        add, keep = tile_spec(n)
        u = jnp.dot(k_tile(n), q_t, preferred_element_type=F32)
        if add is not None:
            u = u + add
        if keep is not None:
            u = jnp.where(keep, u, MASK_VALUE)
        s_ref[n * MXU_WIDTH:(n + 1) * MXU_WIDTH, :] = u
        t = jnp.max(u.reshape(MXU_WIDTH // SUBLANES, SUBLANES, tq), axis=0)
        off = tile_off(n)
        if off is not None:
            t = t + off
        m8 = t if m8 is None else jnp.maximum(m8, t)
    return jnp.max(m8, axis=0, keepdims=True)


def _prob_pass(s_ref, p_ref, m, l, m_blk, tile_off, n_tiles):
    tq = s_ref.shape[1]
    m_new = jnp.maximum(m, m_blk)
    alpha = jnp.exp2(m - m_new)
    ps8 = None
    for n in range(n_tiles):
        rows = slice(n * MXU_WIDTH, (n + 1) * MXU_WIDTH)
        off = tile_off(n)
        ref = m_new if off is None else m_new - off
        p = jnp.exp2(s_ref[rows, :] - ref)
        t = jnp.sum(p.reshape(MXU_WIDTH // SUBLANES, SUBLANES, tq), axis=0)
        ps8 = t if ps8 is None else ps8 + t
        p_ref[rows, :] = p.astype(BF16)
    l_new = alpha * l + jnp.sum(ps8, axis=0, keepdims=True)
    return m_new, l_new, alpha


def _pipelined_key_blocks(n_full, subs, tq):
    neg = jnp.full((1, tq), MASK_VALUE, F32)
    zero = jnp.zeros((1, tq), F32)
    init = (neg, zero, neg, zero)

    @pl.when(n_full == 0)
    def _():
        maxima = [score(0, r) for r, (score, _, _) in enumerate(subs)]
        for r, (_, prob_pv, finalize) in enumerate(subs):
            finalize(prob_pv(0, r, init, maxima[r]))

    @pl.when(n_full > 0)
    def _():
        mx = subs[0][0](0, None)
        for r, (score, prob_pv, finalize) in enumerate(subs):
            def body(j, carry, score=score, prob_pv=prob_pv):
                state, mx = carry
                state = prob_pv(j, None, state, mx)
                return state, score(j + 1, None)

            state, mx = lax.fori_loop(0, n_full - 1, body, (init, mx))
            state = prob_pv(n_full - 1, None, state, mx)
            mx_last = score(n_full, r)
            if r + 1 < len(subs):
                mx = subs[r + 1][0](0, None)
            finalize(prob_pv(n_full, r, state, mx_last))


def _block_tiles(r, tq, tk):
    per_q = tq // MXU_WIDTH
    if r is None:
        return tk // MXU_WIDTH, tk // MXU_WIDTH
    return r * per_q, (r + 1) * per_q


def _own_chunk_mask(key, qry, dq):
    return ((key - dq) >> CHUNK_SHIFT) <= (qry >> CHUNK_SHIFT)


def _da_attn_body(slopes_ref, lq1_ref, lk1_ref, lq2_ref, lk2_ref, sg_ref,
                  qt_ref, k_ref, vt_ref, o_ref, acc_ref, s_ref, p_ref, own_ref,
                  *, tq, tk, lambda_init):
    h = pl.program_id(1)
    n_full = pl.program_id(2)
    slope = slopes_ref[h]
    key = lax.broadcasted_iota(jnp.int32, (MXU_WIDTH, tq), 0)
    key_bias = slope * key.astype(F32)
    acc_ref[...] = jnp.zeros_like(acc_ref)

    @pl.when(n_full == 0)
    def _():
        qry = lax.broadcasted_iota(jnp.int32, (MXU_WIDTH, tq), 1)
        mirror_bias = slope * (2 * qry - key).astype(F32)
        for m in range(tq // MXU_WIDTH):
            dq = -m * MXU_WIDTH
            bias = jnp.minimum(key_bias - slope * dq, mirror_bias + slope * dq)
            own_ref[m] = jnp.where(_own_chunk_mask(key, qry, dq), bias, MASK_VALUE)

    subs = [_da_sub_block(sub, n_full, slope, key_bias, own_ref,
                          lq1_ref, lk1_ref, lq2_ref, lk2_ref, sg_ref, qt_ref, k_ref, vt_ref, o_ref,
                          acc_ref, s_ref, p_ref, tq=tq, tk=tk, lambda_init=lambda_init)
            for sub in range(tk // tq)]
    _pipelined_key_blocks(n_full, subs, tq)


def _da_sub_block(sub, n_full, slope, key_bias, own_ref,
                  lq1_ref, lk1_ref, lq2_ref, lk2_ref, sg_ref, qt_ref, k_ref, vt_ref, o_ref,
                  acc_ref, s_ref, p_ref, *, tq, tk, lambda_init):
    q_cols = slice(sub * tq, (sub + 1) * tq)
    q1_t = qt_ref[:DA_HEAD_DIM, q_cols]
    q2_t = qt_ref[DA_HEAD_DIM:, q_cols]
    q0 = n_full * tk + sub * tq
    acc1_ref, acc2_ref = acc_ref.at[sub, 0], acc_ref.at[sub, 1]
    s_ref, p_ref = s_ref.at[sub], p_ref.at[sub]

    def tile_plan(j, r):
        first_masked, n_tiles = _block_tiles(r, tq, tk)
        specs = {}

        def tile_spec(n):
            if n not in specs:
                specs[n] = (key_bias if n < first_masked else own_ref[n - first_masked], None)
            return specs[n]

        def tile_off(n):
            if n >= first_masked:
                return None
            if r is None:
                return slope * (j * tk + n * MXU_WIDTH - q0).astype(F32)
            return slope * float(n * MXU_WIDTH - r * tq)

        return tile_spec, tile_off, n_tiles

    def score(j, r):
        k0 = pl.multiple_of(j * tk, tk)
        tile_spec, tile_off, n_tiles = tile_plan(j, r)

        def k_tile(lo):
            return lambda n: k_ref[0, pl.ds(pl.multiple_of(k0 + n * MXU_WIDTH, MXU_WIDTH), MXU_WIDTH),
                                   lo:lo + DA_HEAD_DIM]

        return (_score_pass(k_tile(0), q1_t, tile_spec, tile_off, n_tiles, s_ref.at[0]),
                _score_pass(k_tile(DA_HEAD_DIM), q2_t, tile_spec, tile_off, n_tiles, s_ref.at[1]))

    def prob_pv(j, r, state, mx):
        m1, l1, m2, l2 = state
        _, tile_off, n_tiles = tile_plan(j, r)
        m1, l1, a1 = _prob_pass(s_ref.at[0], p_ref.at[0], m1, l1, mx[0], tile_off, n_tiles)
        m2, l2, a2 = _prob_pass(s_ref.at[1], p_ref.at[1], m2, l2, mx[1], tile_off, n_tiles)
        keys = n_tiles * MXU_WIDTH
        vb = vt_ref[:, pl.ds(pl.multiple_of(j * tk, tk), keys)]
        acc1_ref[...] = a1 * acc1_ref[...] + jnp.dot(vb, p_ref[0, :keys, :], preferred_element_type=F32)
        acc2_ref[...] = a2 * acc2_ref[...] + jnp.dot(vb, p_ref[1, :keys, :], preferred_element_type=F32)
        return m1, l1, m2, l2

    def finalize(state):
        _, l1, _, l2 = state
        lam = (jnp.exp(jnp.sum(lq1_ref[...] * lk1_ref[...], axis=-1, keepdims=True))
               - jnp.exp(jnp.sum(lq2_ref[...] * lk2_ref[...], axis=-1, keepdims=True))
               + lambda_init)
        o = acc1_ref[...] * (1.0 / l1) - lam * (acc2_ref[...] * (1.0 / l2))
        ms = jnp.mean(o * o, axis=0, keepdims=True)
        y = o * lax.rsqrt(ms + EPS) * sg_ref[...] * (1.0 - lambda_init)
        o_ref[0, q_cols, :] = y.T.astype(BF16)

    return score, prob_pv, finalize


def _da_attn(slopes, lams, sg_t, q_t, k, v_t, *, tq, tk, lambda_init):
    b, s, _ = k.shape
    w = DA_V_DIM
    nq = s // tk
    n_sub = tk // tq
    vec = pl.BlockSpec((1, DA_HEAD_DIM), lambda bi, h, i: (0, 0))
    return pl.pallas_call(
        functools.partial(_da_attn_body, tq=tq, tk=tk, lambda_init=lambda_init),
        grid=(b, DA_HEADS, nq),
        in_specs=[
            pl.BlockSpec(memory_space=pltpu.SMEM),
            vec, vec, vec, vec,
            pl.BlockSpec((w, tq), lambda bi, h, i: (0, 0)),
            pl.BlockSpec((w, tk), lambda bi, h, i: (h, bi * nq + i)),
            pl.BlockSpec((1, s, w), lambda bi, h, i: (bi, 0, h)),
            pl.BlockSpec((w, s), lambda bi, h, i: (h, bi)),
        ],
        out_specs=pl.BlockSpec((1, tk, w), lambda bi, h, i: (bi, i, h)),
        out_shape=jax.ShapeDtypeStruct((b, s, DA_HEADS * w), BF16),
        scratch_shapes=[pltpu.VMEM((n_sub, 2, w, tq), F32),
                        pltpu.VMEM((n_sub, 2, tk, tq), F32), pltpu.VMEM((n_sub, 2, tk, tq), BF16),
                        pltpu.VMEM((tq // MXU_WIDTH, MXU_WIDTH, tq), F32)],
        compiler_params=_cparams(("parallel", "parallel", "arbitrary")),
        name="da_attn",
    )(slopes, *lams, sg_t, q_t, k, v_t)


def _mla_attn_body(qnt_ref, qrt_ref, kn_ref, kr_ref, vt_ref, o_ref, acc_ref, s_ref, p_ref, own_ref,
                   *, tq, tk):
    n_full = pl.program_id(2)
    acc_ref[...] = jnp.zeros_like(acc_ref)

    @pl.when(n_full == 0)
    def _():
        key = lax.broadcasted_iota(jnp.int32, (MXU_WIDTH, tq), 0)
        qry = lax.broadcasted_iota(jnp.int32, (MXU_WIDTH, tq), 1)
        for m in range(tq // MXU_WIDTH):
            own_ref[m] = jnp.where(_own_chunk_mask(key, qry, -m * MXU_WIDTH), 0.0, MASK_VALUE)

    subs = [_mla_sub_block(sub, own_ref, qnt_ref, qrt_ref, kn_ref, kr_ref, vt_ref, o_ref,
                           acc_ref, s_ref, p_ref, tq=tq, tk=tk) for sub in range(tk // tq)]
    _pipelined_key_blocks(n_full, subs, tq)


def _mla_sub_block(sub, own_ref, qnt_ref, qrt_ref, kn_ref, kr_ref, vt_ref, o_ref,
                   acc_ref, s_ref, p_ref, *, tq, tk):
    q_cols = slice(sub * tq, (sub + 1) * tq)
    qr_t = qrt_ref[:, q_cols]
    qa_t = jnp.concatenate([qnt_ref[:LANES, q_cols], qr_t], axis=0)
    qb_t = jnp.concatenate([qnt_ref[LANES:, q_cols], qr_t], axis=0)
    acc_a_ref, acc_b_ref = acc_ref.at[sub, 0], acc_ref.at[sub, 1]
    s_ref, p_ref = s_ref.at[sub], p_ref.at[sub]
    no_off = lambda n: None

    def score(j, r):
        k0 = pl.multiple_of(j * tk, tk)
        first_masked, n_tiles = _block_tiles(r, tq, tk)
        specs = {}

        def tile_spec(n):
            if n not in specs:
                specs[n] = (None if n < first_masked else own_ref[n - first_masked], None)
            return specs[n]

        def k_tile(lo):
            def tile(n):
                rows = pl.ds(pl.multiple_of(k0 + n * MXU_WIDTH, MXU_WIDTH), MXU_WIDTH)
                return jnp.concatenate([kn_ref[0, rows, lo:lo + LANES],
                                        kr_ref[0, rows, lo:lo + LANES]], axis=-1)
            return tile

        return (_score_pass(k_tile(0), qa_t, tile_spec, no_off, n_tiles, s_ref.at[0]),
                _score_pass(k_tile(LANES), qb_t, tile_spec, no_off, n_tiles, s_ref.at[1]))

    def prob_pv(j, r, state, mx):
        ma, la, mb, lb = state
        _, n_tiles = _block_tiles(r, tq, tk)
        ma, la, aa = _prob_pass(s_ref.at[0], p_ref.at[0], ma, la, mx[0], no_off, n_tiles)
        mb, lb, ab = _prob_pass(s_ref.at[1], p_ref.at[1], mb, lb, mx[1], no_off, n_tiles)
        keys = n_tiles * MXU_WIDTH
        cols = pl.ds(pl.multiple_of(j * tk, tk), keys)
        acc_a_ref[...] = aa * acc_a_ref[...] + jnp.dot(vt_ref[:LANES, cols], p_ref[0, :keys, :],
                                                      preferred_element_type=F32)
        acc_b_ref[...] = ab * acc_b_ref[...] + jnp.dot(vt_ref[LANES:, cols], p_ref[1, :keys, :],
                                                      preferred_element_type=F32)
        return ma, la, mb, lb

    def finalize(state):
        _, la, _, lb = state
        o_t = jnp.concatenate([acc_a_ref[...] * (1.0 / la), acc_b_ref[...] * (1.0 / lb)], axis=0)
        o_ref[0, q_cols, :] = o_t.T.astype(BF16)

    return score, prob_pv, finalize


def _mla_attn(qn_t, qr_t, kn, kr, v_t, *, tq, tk):
    b, s, _ = kn.shape
    w = 2 * LANES
    nq = s // tk
    n_sub = tk // tq
    kv_spec = pl.BlockSpec((1, s, w), lambda bi, h, i: (bi, 0, h))
    return pl.pallas_call(
        functools.partial(_mla_attn_body, tq=tq, tk=tk),
        grid=(b, MLA_HEADS // 2, nq),
        in_specs=[
            pl.BlockSpec((w, tk), lambda bi, h, i: (h, bi * nq + i)),
            pl.BlockSpec((LANES, tk), lambda bi, h, i: (h, bi * nq + i)),
            kv_spec, kv_spec,
            pl.BlockSpec((w, s), lambda bi, h, i: (h, bi)),
        ],
        out_specs=pl.BlockSpec((1, tk, w), lambda bi, h, i: (bi, i, h)),
        out_shape=jax.ShapeDtypeStruct((b, s, MLA_HEADS * LANES), BF16),
        scratch_shapes=[pltpu.VMEM((n_sub, 2, LANES, tq), F32),
                        pltpu.VMEM((n_sub, 2, tk, tq), F32), pltpu.VMEM((n_sub, 2, tk, tq), BF16),
                        pltpu.VMEM((tq // MXU_WIDTH, MXU_WIDTH, tq), F32)],
        compiler_params=_cparams(("parallel", "parallel", "arbitrary")),
        name="mla_attn",
    )(qn_t, qr_t, kn, kr, v_t)


def _merge_body(x_ref, ya_ref, yb_ref, gate_a_ref, gate_b_ref, wa_ref, wb_ref, wo_ref, o_ref):
    @pl.when(pl.program_id(1) == 0)
    def _():
        o_ref[...] = x_ref[...]

    a = jnp.dot(ya_ref[...], wa_ref[...], preferred_element_type=F32)
    b = jnp.dot(yb_ref[...], wb_ref[...], preferred_element_type=F32)
    m = (gate_a_ref[...].astype(F32) * a + gate_b_ref[...].astype(F32) * b).astype(BF16)
    o_ref[...] += jnp.dot(m, wo_ref[...], preferred_element_type=F32)


def _merge(x, ya, yb, gates, wa, wb, wo, *, tm, tn):
    t, d = x.shape
    nj = d // tn
    tok = pl.BlockSpec((tm, d), lambda i, j: (i, 0))
    return pl.pallas_call(
        _merge_body,
        grid=(t // tm, nj),
        in_specs=[
            tok, tok, tok,
            pl.BlockSpec((tm, tn), lambda i, j: (i, j)),
            pl.BlockSpec((tm, tn), lambda i, j: (i, j + nj)),
            pl.BlockSpec((d, tn), lambda i, j: (0, j)),
            pl.BlockSpec((d, tn), lambda i, j: (0, j)),
            pl.BlockSpec((tn, d), lambda i, j: (j, 0)),
        ],
        out_specs=tok,
        out_shape=jax.ShapeDtypeStruct((t, d), F32),
        compiler_params=_cparams(("parallel", "arbitrary")),
        name="merge",
    )(x, ya, yb, gates, gates, wa, wb, wo)


def _rot_cols(w):
    half = w.shape[-1] // 2
    return jnp.concatenate([-w[..., half:], w[..., :half]], axis=-1)


def _swap_halves(g):
    half = g.shape[-1] // 2
    return jnp.concatenate([g[..., half:], g[..., :half]], axis=-1)


def _pick_tile(n, pref):
    return pref if n % pref == 0 else n


def kernel(x, ffn1_norm_g, ffn1_w1, ffn1_w3, ffn1_w2, mix_norm_g, w_in, da_q_norm_g, da_k_norm_g, da_lambda_q1, da_lambda_k1, da_lambda_q2, da_lambda_k2, da_subln_g, mla_q_a_norm_g, mla_w_qb, mla_kv_a_norm_g, mla_w_kvb, mla_q_norm_g, mla_k_norm_g, w_branch_a, w_branch_b, w_out, ffn2_norm_g, ffn2_w1, ffn2_w3, ffn2_w2):
    bsz, seq, d = x.shape
    t = bsz * seq
    depth = ffn1_norm_g.shape[0]
    tm = _pick_tile(t, 512)
    tm_proj = _pick_tile(t, 1024)
    tm_mla = _pick_tile(seq, 256)
    tq = _pick_tile(seq, 512)
    tk = _pick_tile(seq, 1024)
    row = lambda v: v.reshape(1, -1).astype(F32)
    col_tile = lambda v, n: jnp.broadcast_to(v.astype(F32)[:, None], (v.shape[0], n))

    inv = ROPE_THETA ** (-jnp.arange(0, MLA_ROPE_DIM, 2, dtype=F32) / MLA_ROPE_DIM)
    ang = jnp.arange(seq, dtype=F32)[:, None] * inv[None, :]
    cos2 = jnp.tile(jnp.cos(ang), (1, 4))
    sin2 = jnp.tile(jnp.sin(ang), (1, 4))
    cos_t, sin_t = cos2.T, sin2.T
    slopes = (2.0 ** (-8.0 * jnp.arange(1, DA_HEADS + 1, dtype=F32) / DA_HEADS)) * LOG2E

    xt = x.reshape(t, d)
    for l in range(depth):
        lambda_init = 0.8 - 0.6 * math.exp(-0.3 * l)

        xt = _ffn(xt, row(ffn1_norm_g[l]), ffn1_w1[l].astype(BF16), ffn1_w3[l].astype(BF16),
                  (0.5 * ffn1_w2[l]).astype(BF16), tm=tm, tf=512)

        w = w_in[l]
        o = [0]
        for n in (2 * DA_HEADS * DA_HEAD_DIM, 2 * DA_HEADS * DA_HEAD_DIM, DA_HEADS * DA_V_DIM,
                  MLA_Q_RANK, MLA_KV_RANK, MLA_ROPE_DIM, d, d):
            o.append(o[-1] + n)
        w_kr = w[:, o[5]:o[6]]
        w_kr_rot = _rot_cols(w_kr)
        w_qkv = w[:, o[0]:o[3]].astype(BF16)
        w_lat = jnp.concatenate(
            [w[:, o[3]:o[5]], w_kr, w_kr, w_kr_rot, w_kr_rot], axis=1).astype(BF16)
        w_gate = w[:, o[6]:].astype(BF16)
        q_scale = DA_HEAD_DIM ** -0.5 * LOG2E
        q_da_t, k_da, v_da_t, lat, gates = _in_proj(
            xt, row(mix_norm_g[l]), w_qkv, w_lat, w_gate,
            col_tile(da_q_norm_g[l] * q_scale, tm_proj), row(da_k_norm_g[l]), tm=tm_proj)

        wq = mla_w_qb[l].reshape(MLA_Q_RANK, MLA_HEADS, MLA_QK_DIM)
        wq_nope = wq[:, :, :MLA_NOPE_DIM].reshape(MLA_Q_RANK, MLA_HEADS // 2, 2 * LANES)
        wq_rope = wq[:, :, MLA_NOPE_DIM:]
        wq_rot = _rot_cols(wq_rope).reshape(MLA_Q_RANK, MLA_HEADS // 2, LANES)
        wq_rope = wq_rope.reshape(MLA_Q_RANK, MLA_HEADS // 2, LANES)
        wq_packed = jnp.concatenate([wq_nope, wq_rope, wq_rot], axis=-1).reshape(
            MLA_Q_RANK, MLA_HEADS * 2 * LANES).astype(BF16)
        wkv = mla_w_kvb[l].reshape(MLA_KV_RANK, MLA_HEADS, MLA_NOPE_DIM + MLA_V_DIM)
        wkn = wkv[:, :, :MLA_NOPE_DIM].reshape(MLA_KV_RANK, MLA_HEADS * MLA_NOPE_DIM).astype(BF16)
        wv = wkv[:, :, MLA_NOPE_DIM:].reshape(MLA_KV_RANK, MLA_HEADS * MLA_V_DIM).astype(BF16)
        mla_scale = MLA_QK_DIM ** -0.5 * LOG2E
        gq, gk = mla_q_norm_g[l], mla_k_norm_g[l]
        q_gains_t = (
            col_tile(gq[:MLA_NOPE_DIM] * mla_scale, tm_mla),
            col_tile(jnp.tile(gq[MLA_NOPE_DIM:], 2) * mla_scale, tm_mla),
            col_tile(jnp.tile(_swap_halves(gq[MLA_NOPE_DIM:]), 2) * mla_scale, tm_mla),
        )
        k_gains = (
            row(gk[:MLA_NOPE_DIM]),
            row(jnp.tile(gk[MLA_NOPE_DIM:], 2)),
            row(jnp.tile(_swap_halves(gk[MLA_NOPE_DIM:]), 2)),
        )
        qn_t, qr_t, kn, kr, v_mla_t = _mla_proj(
            lat, row(mla_q_a_norm_g[l]), row(mla_kv_a_norm_g[l]), wq_packed, wkn, wv,
            q_gains_t, k_gains, cos2, sin2, cos_t, sin_t, tm=tm_mla, seq=seq)

        b3 = lambda a: a.reshape(bsz, seq, a.shape[-1])
        lams = (row(da_lambda_q1[l]), row(da_lambda_k1[l]), row(da_lambda_q2[l]), row(da_lambda_k2[l]))
        y_a = _da_attn(slopes, lams, col_tile(da_subln_g[l], tq), q_da_t, b3(k_da), v_da_t,
                       tq=tq, tk=tk, lambda_init=lambda_init)
        y_b = _mla_attn(qn_t, qr_t, b3(kn), b3(kr), v_mla_t, tq=tq, tk=tk)

        xt = _merge(xt, y_a.reshape(t, -1), y_b.reshape(t, -1), gates,
                    w_branch_a[l].astype(BF16), w_branch_b[l].astype(BF16), w_out[l].astype(BF16),
                    tm=tm, tn=512)

        xt = _ffn(xt, row(ffn2_norm_g[l]), ffn2_w1[l].astype(BF16), ffn2_w3[l].astype(BF16),
                  (0.5 * ffn2_w2[l]).astype(BF16), tm=tm, tf=512)
    return xt.reshape(bsz, seq, d)
```

```python
import functools
import math

import jax
import jax.numpy as jnp
from jax import lax
from jax.experimental import pallas as pl
from jax.experimental.pallas import tpu as pltpu

F32 = jnp.float32
BF16 = jnp.bfloat16

EPS = 1e-6
CHUNK = 64
CHUNK_SHIFT = CHUNK.bit_length() - 1
DA_HEADS = 8
DA_HEAD_DIM = 128
DA_V_DIM = 2 * DA_HEAD_DIM
MLA_HEADS = 16
MLA_Q_RANK = 768
MLA_KV_RANK = 512
MLA_NOPE_DIM = 128
MLA_ROPE_DIM = 64
MLA_V_DIM = 128
MLA_QK_DIM = MLA_NOPE_DIM + MLA_ROPE_DIM
ROPE_THETA = 10000.0
LOG2E = math.log2(math.e)
MASK_VALUE = -1e30

LANES = 128
SUBLANES = 8
MXU_WIDTH = 256
V7X_VMEM_BYTES = 64 * 1024 * 1024
VMEM_LIMIT = V7X_VMEM_BYTES - 8 * 1024 * 1024


def _cparams(sem):
    return pltpu.CompilerParams(dimension_semantics=sem, vmem_limit_bytes=VMEM_LIMIT)


def _rms(x, g):
    ms = jnp.mean(x * x, axis=-1, keepdims=True)
    return x * lax.rsqrt(ms + EPS) * g


def _tn_dot(w, h):
    return lax.dot_general(w, h, (((0,), (1,)), ((), ())), preferred_element_type=F32)


def _nt_dot(a, b):
    return lax.dot_general(a, b, (((1,), (1,)), ((), ())), preferred_element_type=F32)


def _ffn_body(x_ref, g_ref, w1_ref, w3_ref, w2_ref, o_ref, h_ref):
    @pl.when(pl.program_id(1) == 0)
    def _():
        x = x_ref[...]
        h_ref[...] = _rms(x, g_ref[...]).astype(BF16)
        o_ref[...] = x

    h = h_ref[...]
    a = jnp.dot(h, w1_ref[...], preferred_element_type=F32)
    b = jnp.dot(h, w3_ref[...], preferred_element_type=F32)
    u = (a * jax.nn.sigmoid(a) * b).astype(BF16)
    o_ref[...] += jnp.dot(u, w2_ref[...], preferred_element_type=F32)


def _ffn(x, g, w1, w3, w2_half, *, tm, tf):
    t, d = x.shape
    f = w1.shape[1]
    return pl.pallas_call(
        _ffn_body,
        grid=(t // tm, f // tf),
        in_specs=[
            pl.BlockSpec((tm, d), lambda i, j: (i, 0)),
            pl.BlockSpec((1, d), lambda i, j: (0, 0)),
            pl.BlockSpec((d, tf), lambda i, j: (0, j)),
            pl.BlockSpec((d, tf), lambda i, j: (0, j)),
            pl.BlockSpec((tf, d), lambda i, j: (j, 0)),
        ],
        out_specs=pl.BlockSpec((tm, d), lambda i, j: (i, 0)),
        out_shape=jax.ShapeDtypeStruct((t, d), F32),
        scratch_shapes=[pltpu.VMEM((tm, d), BF16)],
        compiler_params=_cparams(("parallel", "arbitrary")),
        name="ffn",
    )(x, g, w1, w3, w2_half)


_W_IN_QKV = 2 * DA_HEADS * DA_HEAD_DIM * 2 + DA_HEADS * DA_V_DIM
_W_IN_KROPE = _W_IN_QKV + MLA_Q_RANK + MLA_KV_RANK
_W_IN_GATE = _W_IN_KROPE + MLA_ROPE_DIM
REPACK_ROWS = 128


def _repack_body(w_ref, qkv_ref, lat_ref, gate_ref):
    w = w_ref[...]
    qkv_ref[...] = w[:, :_W_IN_QKV].astype(BF16)
    k_rope = w[:, _W_IN_KROPE:_W_IN_GATE]
    k_rot = _rot_cols(k_rope)
    lat_ref[...] = jnp.concatenate(
        [w[:, _W_IN_QKV:_W_IN_KROPE], k_rope, k_rope, k_rot, k_rot], axis=1).astype(BF16)
    gate_ref[...] = w[:, _W_IN_GATE:].astype(BF16)


def _repack_w_in(w, d):
    r, n_in = w.shape
    tr = _pick_tile(r, REPACK_ROWS)
    n_lat = _W_IN_KROPE - _W_IN_QKV + 4 * MLA_ROPE_DIM
    out = lambda n: pl.BlockSpec((tr, n), lambda i: (i, 0))
    return pl.pallas_call(
        _repack_body,
        grid=(r // tr,),
        in_specs=[pl.BlockSpec((tr, n_in), lambda i: (i, 0))],
        out_specs=[out(_W_IN_QKV), out(n_lat), out(2 * d)],
        out_shape=[jax.ShapeDtypeStruct((r, _W_IN_QKV), BF16),
                   jax.ShapeDtypeStruct((r, n_lat), BF16),
                   jax.ShapeDtypeStruct((r, 2 * d), BF16)],
        compiler_params=_cparams(("parallel",)),
        name="repack_w_in",
    )(w)


PROJ_TN = 512
_Q0, _K0, _V0, _C0, _G0, _NJ = 0, 4, 8, 12, 15, 23


def _head_norm(acc, g):
    parts = []
    for c in range(acc.shape[1] // LANES):
        blk = acc[:, c * LANES:(c + 1) * LANES]
        ms = jnp.mean(blk * blk, axis=-1, keepdims=True)
        parts.append(blk * lax.rsqrt(ms + EPS) * g)
    return jnp.concatenate(parts, axis=-1)


def _head_norm_t(acc_t, g_t):
    parts = []
    for c in range(acc_t.shape[0] // LANES):
        blk = acc_t[c * LANES:(c + 1) * LANES, :]
        ms = jnp.mean(blk * blk, axis=0, keepdims=True)
        parts.append(blk * lax.rsqrt(ms + EPS) * g_t)
    return jnp.concatenate(parts, axis=0)


def _proj_body(x_ref, g_ref, wqkv_ref, wlat_ref, wgate_ref, qg_ref, kg_ref,
               qt_ref, k_ref, vt_ref, c_ref, gate_ref, h_ref):
    j = pl.program_id(1)

    @pl.when(j == 0)
    def _():
        h_ref[...] = _rms(x_ref[...], g_ref[...]).astype(BF16)

    @pl.when(j < _K0)
    def _():
        qt_ref[...] = _head_norm_t(_tn_dot(wqkv_ref[...], h_ref[...]), qg_ref[...]).astype(BF16)

    @pl.when((j >= _K0) & (j < _V0))
    def _():
        acc = jnp.dot(h_ref[...], wqkv_ref[...], preferred_element_type=F32)
        k_ref[...] = _head_norm(acc, kg_ref[...]).astype(BF16)

    @pl.when((j >= _V0) & (j < _C0))
    def _():
        vt_ref[...] = _tn_dot(wqkv_ref[...], h_ref[...]).astype(BF16)

    @pl.when((j >= _C0) & (j < _G0))
    def _():
        c_ref[...] = jnp.dot(h_ref[...], wlat_ref[...], preferred_element_type=F32)

    @pl.when(j >= _G0)
    def _():
        acc = jnp.dot(h_ref[...], wgate_ref[...], preferred_element_type=F32)
        gate_ref[...] = jax.nn.sigmoid(acc).astype(BF16)


def _in_proj(x, g, w_qkv, w_lat, w_gate, qg_t, kg, *, tm):
    t, d = x.shape
    tn = PROJ_TN
    n_q, n_k, n_v, n_c, n_g = _K0 - _Q0, _V0 - _K0, _C0 - _V0, _G0 - _C0, _NJ - _G0

    def w_cols(j0, n):
        return pl.BlockSpec((d, tn), lambda i, j: (0, jnp.clip(j - j0, 0, n - 1)))

    def cols(j0, n):
        return pl.BlockSpec((tm, tn), lambda i, j: (i, jnp.clip(j - j0, 0, n - 1)))

    def rows(j0, n):
        return pl.BlockSpec((tn, tm), lambda i, j: (jnp.clip(j - j0, 0, n - 1), i))

    return pl.pallas_call(
        _proj_body,
        grid=(t // tm, _NJ),
        in_specs=[
            pl.BlockSpec((tm, d), lambda i, j: (i, 0)),
            pl.BlockSpec((1, d), lambda i, j: (0, 0)),
            w_cols(_Q0, n_q + n_k + n_v), w_cols(_C0, n_c), w_cols(_G0, n_g),
            pl.BlockSpec((LANES, tm), lambda i, j: (0, 0)),
            pl.BlockSpec((1, LANES), lambda i, j: (0, 0)),
        ],
        out_specs=[rows(_Q0, n_q), cols(_K0, n_k), rows(_V0, n_v), cols(_C0, n_c), cols(_G0, n_g)],
        out_shape=[
            jax.ShapeDtypeStruct((n_q * tn, t), BF16),
            jax.ShapeDtypeStruct((t, n_k * tn), BF16),
            jax.ShapeDtypeStruct((n_v * tn, t), BF16),
            jax.ShapeDtypeStruct((t, n_c * tn), F32),
            jax.ShapeDtypeStruct((t, n_g * tn), BF16),
        ],
        scratch_shapes=[pltpu.VMEM((tm, d), BF16)],
        compiler_params=_cparams(("parallel", "arbitrary")),
        name="in_proj",
    )(x, g, w_qkv, w_lat, w_gate, qg_t, kg)


def _mla_proj_body(c_ref, qa_g_ref, kva_g_ref, wq_ref, wkn_ref, wv_ref,
                   gqn_ref, gqr_ref, gqt_ref, gk_nope_ref, gk_rope_ref, gk_rot_ref,
                   cos_ref, sin_ref, cost_ref, sint_ref,
                   qnt_ref, qrt_ref, kn_ref, kr_ref, vt_ref):
    c = c_ref[...]
    o_kv = MLA_Q_RANK
    o_kr = MLA_Q_RANK + MLA_KV_RANK
    cqn = _rms(c[:, :o_kv], qa_g_ref[...]).astype(BF16)
    ckvn = _rms(c[:, o_kv:o_kr], kva_g_ref[...]).astype(BF16)
    k_rope = c[:, o_kr:o_kr + LANES]
    k_rot = c[:, o_kr + LANES:o_kr + 2 * LANES]
    first = lax.broadcasted_iota(jnp.int32, (1, LANES), 1) < MLA_ROPE_DIM
    k_roped = k_rope * gk_rope_ref[...] * cos_ref[...] + k_rot * gk_rot_ref[...] * sin_ref[...]
    k_rope_sq = 0.5 * jnp.sum(k_rope * k_rope, axis=-1, keepdims=True)
    inv_d = 1.0 / MLA_QK_DIM
    cos_t = cost_ref[...]
    sin_t = sint_ref[...]
    half = MLA_ROPE_DIM

    for p in range(MLA_HEADS // 2):
        qraw = _nt_dot(wq_ref[p * 4 * LANES:(p + 1) * 4 * LANES, :], cqn)
        na, nb = qraw[:LANES], qraw[LANES:2 * LANES]
        rp, rt = qraw[2 * LANES:3 * LANES], qraw[3 * LANES:]
        rp2 = rp * rp
        sa = jnp.sum(na * na, axis=0, keepdims=True) + jnp.sum(rp2[:half], axis=0, keepdims=True)
        sb = jnp.sum(nb * nb, axis=0, keepdims=True) + jnp.sum(rp2[half:], axis=0, keepdims=True)
        ra = lax.rsqrt(sa * inv_d + EPS)
        rb = lax.rsqrt(sb * inv_d + EPS)
        ra_rows = slice((2 * p) * LANES, (2 * p + 1) * LANES)
        rb_rows = slice((2 * p + 1) * LANES, (2 * p + 2) * LANES)
        qnt_ref[ra_rows, :] = (na * ra * gqn_ref[...]).astype(BF16)
        qnt_ref[rb_rows, :] = (nb * rb * gqn_ref[...]).astype(BF16)
        roped = rp * gqr_ref[...] * cos_t + rt * gqt_ref[...] * sin_t
        qrt_ref[p * LANES:(p + 1) * LANES, :] = jnp.concatenate(
            [roped[:half] * ra, roped[half:] * rb], axis=0).astype(BF16)

        knraw = jnp.dot(ckvn, wkn_ref[:, p * 2 * LANES:(p + 1) * 2 * LANES], preferred_element_type=F32)
        kna, knb = knraw[:, :LANES], knraw[:, LANES:]
        rka = lax.rsqrt((jnp.sum(kna * kna, axis=-1, keepdims=True) + k_rope_sq) * inv_d + EPS)
        rkb = lax.rsqrt((jnp.sum(knb * knb, axis=-1, keepdims=True) + k_rope_sq) * inv_d + EPS)
        kn_ref[:, ra_rows] = (kna * rka * gk_nope_ref[...]).astype(BF16)
        kn_ref[:, rb_rows] = (knb * rkb * gk_nope_ref[...]).astype(BF16)
        kr_ref[:, ra_rows] = jnp.where(first, k_roped * rka, 0.0).astype(BF16)
        kr_ref[:, rb_rows] = jnp.where(first, 0.0, k_roped * rkb).astype(BF16)

        vt_ref[p * 2 * LANES:(p + 1) * 2 * LANES, :] = _nt_dot(
            wv_ref[p * 2 * LANES:(p + 1) * 2 * LANES, :], ckvn).astype(BF16)


def _mla_proj(c, qa_g, kva_g, wq, wkn, wv, q_gains_t, k_gains, cos2, sin2, cos_t, sin_t, *, tm, seq):
    t = c.shape[0]
    n_pos = seq // tm
    const = lambda a: pl.BlockSpec(a.shape, lambda i: (0, 0))
    hw = MLA_HEADS * LANES
    tok = lambda w: pl.BlockSpec((tm, w), lambda i: (i, 0))
    tok_t = lambda w: pl.BlockSpec((w, tm), lambda i: (0, i))
    return pl.pallas_call(
        _mla_proj_body,
        grid=(t // tm,),
        in_specs=[tok(c.shape[1]), const(qa_g), const(kva_g), const(wq), const(wkn), const(wv)]
        + [const(a) for a in q_gains_t] + [const(a) for a in k_gains] + [
            pl.BlockSpec((tm, LANES), lambda i: (i % n_pos, 0)),
            pl.BlockSpec((tm, LANES), lambda i: (i % n_pos, 0)),
            pl.BlockSpec((LANES, tm), lambda i: (0, i % n_pos)),
            pl.BlockSpec((LANES, tm), lambda i: (0, i % n_pos)),
        ],
        out_specs=[tok_t(hw), tok_t(hw // 2), tok(hw), tok(hw), tok_t(hw)],
        out_shape=[
            jax.ShapeDtypeStruct((hw, t), BF16),
            jax.ShapeDtypeStruct((hw // 2, t), BF16),
            jax.ShapeDtypeStruct((t, hw), BF16),
            jax.ShapeDtypeStruct((t, hw), BF16),
            jax.ShapeDtypeStruct((hw, t), BF16),
        ],
        compiler_params=_cparams(("parallel",)),
        name="mla_proj",
    )(c, qa_g, kva_g, wq, wkn, wv, *q_gains_t, *k_gains, cos2, sin2, cos_t, sin_t)


def _score_pass(k_tile, q_t, tile_spec, tile_off, n_tiles, s_ref):
    tq = q_t.shape[1]
    m8 = None
    for n in range(n_tiles):
        add, keep = tile_spec(n)
        u = jnp.dot(k_tile(n), q_t, preferred_element_type=F32)
        if add is not None:
            u = u + add
        if keep is not None:
            u = jnp.where(keep, u, MASK_VALUE)
        s_ref[n * MXU_WIDTH:(n + 1) * MXU_WIDTH, :] = u
        t = jnp.max(u.reshape(MXU_WIDTH // SUBLANES, SUBLANES, tq), axis=0)
        off = tile_off(n)
        if off is not None:
            t = t + off
        m8 = t if m8 is None else jnp.maximum(m8, t)
    return jnp.max(m8, axis=0, keepdims=True)


def _prob_pass(s_ref, p_ref, m, l, m_blk, tile_off, n_tiles):
    tq = s_ref.shape[1]
    m_new = jnp.maximum(m, m_blk)
    alpha = jnp.exp2(m - m_new)
    ps8 = None
    for n in range(n_tiles):
        rows = slice(n * MXU_WIDTH, (n + 1) * MXU_WIDTH)
        off = tile_off(n)
        ref = m_new if off is None else m_new - off
        p = jnp.exp2(s_ref[rows, :] - ref)
        t = jnp.sum(p.reshape(MXU_WIDTH // SUBLANES, SUBLANES, tq), axis=0)
        ps8 = t if ps8 is None else ps8 + t
        p_ref[rows, :] = p.astype(BF16)
    l_new = alpha * l + jnp.sum(ps8, axis=0, keepdims=True)
    return m_new, l_new, alpha


def _pipelined_key_blocks(n_full, subs, tq):
    neg = jnp.full((1, tq), MASK_VALUE, F32)
    zero = jnp.zeros((1, tq), F32)
    init = (neg, zero, neg, zero)

    @pl.when(n_full == 0)
    def _():
        maxima = [score(0, r) for r, (score, _, _) in enumerate(subs)]
        for r, (_, prob_pv, finalize) in enumerate(subs):
            finalize(prob_pv(0, r, init, maxima[r]))

    @pl.when(n_full > 0)
    def _():
        mx = subs[0][0](0, None)
        for r, (score, prob_pv, finalize) in enumerate(subs):
            def body(j, carry, score=score, prob_pv=prob_pv):
                state, mx = carry
                state = prob_pv(j, None, state, mx)
                return state, score(j + 1, None)

            state, mx = lax.fori_loop(0, n_full - 1, body, (init, mx))
            state = prob_pv(n_full - 1, None, state, mx)
            mx_last = score(n_full, r)
            if r + 1 < len(subs):
                mx = subs[r + 1][0](0, None)
            finalize(prob_pv(n_full, r, state, mx_last))


def _block_tiles(r, tq, tk):
    per_q = tq // MXU_WIDTH
    if r is None:
        return tk // MXU_WIDTH, tk // MXU_WIDTH
    return r * per_q, (r + 1) * per_q


def _own_chunk_mask(key, qry, dq):
    return ((key - dq) >> CHUNK_SHIFT) <= (qry >> CHUNK_SHIFT)


def _da_attn_body(slopes_ref, lq1_ref, lk1_ref, lq2_ref, lk2_ref, sg_ref,
                  qt_ref, k_ref, vt_ref, o_ref, acc_ref, s_ref, p_ref, own_ref,
                  *, tq, tk, lambda_init):
    h = pl.program_id(1)
    n_full = pl.program_id(2)
    slope = slopes_ref[h]
    key = lax.broadcasted_iota(jnp.int32, (MXU_WIDTH, tq), 0)
    key_bias = slope * key.astype(F32)
    acc_ref[...] = jnp.zeros_like(acc_ref)

    @pl.when(n_full == 0)
    def _():
        qry = lax.broadcasted_iota(jnp.int32, (MXU_WIDTH, tq), 1)
        mirror_bias = slope * (2 * qry - key).astype(F32)
        for m in range(tq // MXU_WIDTH):
            dq = -m * MXU_WIDTH
            bias = jnp.minimum(key_bias - slope * dq, mirror_bias + slope * dq)
            own_ref[m] = jnp.where(_own_chunk_mask(key, qry, dq), bias, MASK_VALUE)

    subs = [_da_sub_block(sub, n_full, slope, key_bias, own_ref,
                          lq1_ref, lk1_ref, lq2_ref, lk2_ref, sg_ref, qt_ref, k_ref, vt_ref, o_ref,
                          acc_ref, s_ref, p_ref, tq=tq, tk=tk, lambda_init=lambda_init)
            for sub in range(tk // tq)]
    _pipelined_key_blocks(n_full, subs, tq)


def _da_sub_block(sub, n_full, slope, key_bias, own_ref,
                  lq1_ref, lk1_ref, lq2_ref, lk2_ref, sg_ref, qt_ref, k_ref, vt_ref, o_ref,
                  acc_ref, s_ref, p_ref, *, tq, tk, lambda_init):
    q_cols = slice(sub * tq, (sub + 1) * tq)
    q1_t = qt_ref[:DA_HEAD_DIM, q_cols]
    q2_t = qt_ref[DA_HEAD_DIM:, q_cols]
    q0 = n_full * tk + sub * tq
    acc1_ref, acc2_ref = acc_ref.at[sub, 0], acc_ref.at[sub, 1]
    s_ref, p_ref = s_ref.at[sub], p_ref.at[sub]

    def tile_plan(j, r):
        first_masked, n_tiles = _block_tiles(r, tq, tk)
        specs = {}

        def tile_spec(n):
            if n not in specs:
                specs[n] = (key_bias if n < first_masked else own_ref[n - first_masked], None)
            return specs[n]

        def tile_off(n):
            if n >= first_masked:
                return None
            if r is None:
                return slope * (j * tk + n * MXU_WIDTH - q0).astype(F32)
            return slope * float(n * MXU_WIDTH - r * tq)

        return tile_spec, tile_off, n_tiles

    def score(j, r):
        k0 = pl.multiple_of(j * tk, tk)
        tile_spec, tile_off, n_tiles = tile_plan(j, r)

        def k_tile(lo):
            return lambda n: k_ref[0, pl.ds(pl.multiple_of(k0 + n * MXU_WIDTH, MXU_WIDTH), MXU_WIDTH),
                                   lo:lo + DA_HEAD_DIM]

        return (_score_pass(k_tile(0), q1_t, tile_spec, tile_off, n_tiles, s_ref.at[0]),
                _score_pass(k_tile(DA_HEAD_DIM), q2_t, tile_spec, tile_off, n_tiles, s_ref.at[1]))

    def prob_pv(j, r, state, mx):
        m1, l1, m2, l2 = state
        _, tile_off, n_tiles = tile_plan(j, r)
        m1, l1, a1 = _prob_pass(s_ref.at[0], p_ref.at[0], m1, l1, mx[0], tile_off, n_tiles)
        m2, l2, a2 = _prob_pass(s_ref.at[1], p_ref.at[1], m2, l2, mx[1], tile_off, n_tiles)
        keys = n_tiles * MXU_WIDTH
        vb = vt_ref[:, pl.ds(pl.multiple_of(j * tk, tk), keys)]
        acc1_ref[...] = a1 * acc1_ref[...] + jnp.dot(vb, p_ref[0, :keys, :], preferred_element_type=F32)
        acc2_ref[...] = a2 * acc2_ref[...] + jnp.dot(vb, p_ref[1, :keys, :], preferred_element_type=F32)
        return m1, l1, m2, l2

    def finalize(state):
        _, l1, _, l2 = state
        lam = (jnp.exp(jnp.sum(lq1_ref[...] * lk1_ref[...], axis=-1, keepdims=True))
               - jnp.exp(jnp.sum(lq2_ref[...] * lk2_ref[...], axis=-1, keepdims=True))
               + lambda_init)
        o = acc1_ref[...] * (1.0 / l1) - lam * (acc2_ref[...] * (1.0 / l2))
        ms = jnp.mean(o * o, axis=0, keepdims=True)
        y = o * lax.rsqrt(ms + EPS) * sg_ref[...] * (1.0 - lambda_init)
        o_ref[0, q_cols, :] = y.T.astype(BF16)

    return score, prob_pv, finalize


def _da_attn(slopes, lams, sg_t, q_t, k, v_t, *, tq, tk, lambda_init):
    b, s, _ = k.shape
    w = DA_V_DIM
    nq = s // tk
    n_sub = tk // tq
    vec = pl.BlockSpec((1, DA_HEAD_DIM), lambda bi, h, i: (0, 0))
    return pl.pallas_call(
        functools.partial(_da_attn_body, tq=tq, tk=tk, lambda_init=lambda_init),
        grid=(b, DA_HEADS, nq),
        in_specs=[
            pl.BlockSpec(memory_space=pltpu.SMEM),
            vec, vec, vec, vec,
            pl.BlockSpec((w, tq), lambda bi, h, i: (0, 0)),
            pl.BlockSpec((w, tk), lambda bi, h, i: (h, bi * nq + i)),
            pl.BlockSpec((1, s, w), lambda bi, h, i: (bi, 0, h)),
            pl.BlockSpec((w, s), lambda bi, h, i: (h, bi)),
        ],
        out_specs=pl.BlockSpec((1, tk, w), lambda bi, h, i: (bi, i, h)),
        out_shape=jax.ShapeDtypeStruct((b, s, DA_HEADS * w), BF16),
        scratch_shapes=[pltpu.VMEM((n_sub, 2, w, tq), F32),
                        pltpu.VMEM((n_sub, 2, tk, tq), F32), pltpu.VMEM((n_sub, 2, tk, tq), BF16),
                        pltpu.VMEM((tq // MXU_WIDTH, MXU_WIDTH, tq), F32)],
        compiler_params=_cparams(("parallel", "parallel", "arbitrary")),
        name="da_attn",
    )(slopes, *lams, sg_t, q_t, k, v_t)


def _mla_attn_body(qnt_ref, qrt_ref, kn_ref, kr_ref, vt_ref, o_ref, acc_ref, s_ref, p_ref, own_ref,
                   *, tq, tk):
    n_full = pl.program_id(2)
    acc_ref[...] = jnp.zeros_like(acc_ref)

    @pl.when(n_full == 0)
    def _():
        key = lax.broadcasted_iota(jnp.int32, (MXU_WIDTH, tq), 0)
        qry = lax.broadcasted_iota(jnp.int32, (MXU_WIDTH, tq), 1)
        for m in range(tq // MXU_WIDTH):
            own_ref[m] = jnp.where(_own_chunk_mask(key, qry, -m * MXU_WIDTH), 0.0, MASK_VALUE)

    subs = [_mla_sub_block(sub, own_ref, qnt_ref, qrt_ref, kn_ref, kr_ref, vt_ref, o_ref,
                           acc_ref, s_ref, p_ref, tq=tq, tk=tk) for sub in range(tk // tq)]
    _pipelined_key_blocks(n_full, subs, tq)


def _mla_sub_block(sub, own_ref, qnt_ref, qrt_ref, kn_ref, kr_ref, vt_ref, o_ref,
                   acc_ref, s_ref, p_ref, *, tq, tk):
    q_cols = slice(sub * tq, (sub + 1) * tq)
    qr_t = qrt_ref[:, q_cols]
    qa_t = jnp.concatenate([qnt_ref[:LANES, q_cols], qr_t], axis=0)
    qb_t = jnp.concatenate([qnt_ref[LANES:, q_cols], qr_t], axis=0)
    acc_a_ref, acc_b_ref = acc_ref.at[sub, 0], acc_ref.at[sub, 1]
    s_ref, p_ref = s_ref.at[sub], p_ref.at[sub]
    no_off = lambda n: None

    def score(j, r):
        k0 = pl.multiple_of(j * tk, tk)
        first_masked, n_tiles = _block_tiles(r, tq, tk)
        specs = {}

        def tile_spec(n):
            if n not in specs:
                specs[n] = (None if n < first_masked else own_ref[n - first_masked], None)
            return specs[n]

        def k_tile(lo):
            def tile(n):
                rows = pl.ds(pl.multiple_of(k0 + n * MXU_WIDTH, MXU_WIDTH), MXU_WIDTH)
                return jnp.concatenate([kn_ref[0, rows, lo:lo + LANES],
                                        kr_ref[0, rows, lo:lo + LANES]], axis=-1)
            return tile

        return (_score_pass(k_tile(0), qa_t, tile_spec, no_off, n_tiles, s_ref.at[0]),
                _score_pass(k_tile(LANES), qb_t, tile_spec, no_off, n_tiles, s_ref.at[1]))

    def prob_pv(j, r, state, mx):
        ma, la, mb, lb = state
        _, n_tiles = _block_tiles(r, tq, tk)
        ma, la, aa = _prob_pass(s_ref.at[0], p_ref.at[0], ma, la, mx[0], no_off, n_tiles)
        mb, lb, ab = _prob_pass(s_ref.at[1], p_ref.at[1], mb, lb, mx[1], no_off, n_tiles)
        keys = n_tiles * MXU_WIDTH
        cols = pl.ds(pl.multiple_of(j * tk, tk), keys)
        acc_a_ref[...] = aa * acc_a_ref[...] + jnp.dot(vt_ref[:LANES, cols], p_ref[0, :keys, :],
                                                      preferred_element_type=F32)
        acc_b_ref[...] = ab * acc_b_ref[...] + jnp.dot(vt_ref[LANES:, cols], p_ref[1, :keys, :],
                                                      preferred_element_type=F32)
        return ma, la, mb, lb

    def finalize(state):
        _, la, _, lb = state
        o_t = jnp.concatenate([acc_a_ref[...] * (1.0 / la), acc_b_ref[...] * (1.0 / lb)], axis=0)
        o_ref[0, q_cols, :] = o_t.T.astype(BF16)

    return score, prob_pv, finalize


def _mla_attn(qn_t, qr_t, kn, kr, v_t, *, tq, tk):
    b, s, _ = kn.shape
    w = 2 * LANES
    nq = s // tk
    n_sub = tk // tq
    kv_spec = pl.BlockSpec((1, s, w), lambda bi, h, i: (bi, 0, h))
    return pl.pallas_call(
        functools.partial(_mla_attn_body, tq=tq, tk=tk),
        grid=(b, MLA_HEADS // 2, nq),
        in_specs=[
            pl.BlockSpec((w, tk), lambda bi, h, i: (h, bi * nq + i)),
            pl.BlockSpec((LANES, tk), lambda bi, h, i: (h, bi * nq + i)),
            kv_spec, kv_spec,
            pl.BlockSpec((w, s), lambda bi, h, i: (h, bi)),
        ],
        out_specs=pl.BlockSpec((1, tk, w), lambda bi, h, i: (bi, i, h)),
        out_shape=jax.ShapeDtypeStruct((b, s, MLA_HEADS * LANES), BF16),
        scratch_shapes=[pltpu.VMEM((n_sub, 2, LANES, tq), F32),
                        pltpu.VMEM((n_sub, 2, tk, tq), F32), pltpu.VMEM((n_sub, 2, tk, tq), BF16),
                        pltpu.VMEM((tq // MXU_WIDTH, MXU_WIDTH, tq), F32)],
        compiler_params=_cparams(("parallel", "parallel", "arbitrary")),
        name="mla_attn",
    )(qn_t, qr_t, kn, kr, v_t)


def _merge_body(x_ref, ya_ref, yb_ref, gate_a_ref, gate_b_ref, wa_ref, wb_ref, wo_ref, o_ref):
    @pl.when(pl.program_id(1) == 0)
    def _():
        o_ref[...] = x_ref[...]

    a = jnp.dot(ya_ref[...], wa_ref[...], preferred_element_type=F32)
    b = jnp.dot(yb_ref[...], wb_ref[...], preferred_element_type=F32)
    m = (gate_a_ref[...].astype(F32) * a + gate_b_ref[...].astype(F32) * b).astype(BF16)
    o_ref[...] += jnp.dot(m, wo_ref[...], preferred_element_type=F32)


def _merge(x, ya, yb, gates, wa, wb, wo, *, tm, tn):
    t, d = x.shape
    nj = d // tn
    tok = pl.BlockSpec((tm, d), lambda i, j: (i, 0))
    return pl.pallas_call(
        _merge_body,
        grid=(t // tm, nj),
        in_specs=[
            tok, tok, tok,
            pl.BlockSpec((tm, tn), lambda i, j: (i, j)),
            pl.BlockSpec((tm, tn), lambda i, j: (i, j + nj)),
            pl.BlockSpec((d, tn), lambda i, j: (0, j)),
            pl.BlockSpec((d, tn), lambda i, j: (0, j)),
            pl.BlockSpec((tn, d), lambda i, j: (j, 0)),
        ],
        out_specs=tok,
        out_shape=jax.ShapeDtypeStruct((t, d), F32),
        compiler_params=_cparams(("parallel", "arbitrary")),
        name="merge",
    )(x, ya, yb, gates, gates, wa, wb, wo)


def _rot_cols(w):
    half = w.shape[-1] // 2
    return jnp.concatenate([-w[..., half:], w[..., :half]], axis=-1)


def _swap_halves(g):
    half = g.shape[-1] // 2
    return jnp.concatenate([g[..., half:], g[..., :half]], axis=-1)


def _pick_tile(n, pref):
    return pref if n % pref == 0 else n


def kernel(x, ffn1_norm_g, ffn1_w1, ffn1_w3, ffn1_w2, mix_norm_g, w_in, da_q_norm_g, da_k_norm_g, da_lambda_q1, da_lambda_k1, da_lambda_q2, da_lambda_k2, da_subln_g, mla_q_a_norm_g, mla_w_qb, mla_kv_a_norm_g, mla_w_kvb, mla_q_norm_g, mla_k_norm_g, w_branch_a, w_branch_b, w_out, ffn2_norm_g, ffn2_w1, ffn2_w3, ffn2_w2):
    bsz, seq, d = x.shape
    t = bsz * seq
    depth = ffn1_norm_g.shape[0]
    tm = _pick_tile(t, 512)
    tm_proj = _pick_tile(t, 1024)
    tm_mla = _pick_tile(seq, 256)
    tq = _pick_tile(seq, 512)
    tk = _pick_tile(seq, 1024)
    row = lambda v: v.reshape(1, -1).astype(F32)
    col_tile = lambda v, n: jnp.broadcast_to(v.astype(F32)[:, None], (v.shape[0], n))

    inv = ROPE_THETA ** (-jnp.arange(0, MLA_ROPE_DIM, 2, dtype=F32) / MLA_ROPE_DIM)
    ang = jnp.arange(seq, dtype=F32)[:, None] * inv[None, :]
    cos2 = jnp.tile(jnp.cos(ang), (1, 4))
    sin2 = jnp.tile(jnp.sin(ang), (1, 4))
    cos_t, sin_t = cos2.T, sin2.T
    slopes = (2.0 ** (-8.0 * jnp.arange(1, DA_HEADS + 1, dtype=F32) / DA_HEADS)) * LOG2E

    xt = x.reshape(t, d)
    for l in range(depth):
        lambda_init = 0.8 - 0.6 * math.exp(-0.3 * l)

        xt = _ffn(xt, row(ffn1_norm_g[l]), ffn1_w1[l].astype(BF16), ffn1_w3[l].astype(BF16),
                  (0.5 * ffn1_w2[l]).astype(BF16), tm=tm, tf=512)

        w_qkv, w_lat, w_gate = _repack_w_in(w_in[l], d)
        q_scale = DA_HEAD_DIM ** -0.5 * LOG2E
        q_da_t, k_da, v_da_t, lat, gates = _in_proj(
            xt, row(mix_norm_g[l]), w_qkv, w_lat, w_gate,
            col_tile(da_q_norm_g[l] * q_scale, tm_proj), row(da_k_norm_g[l]), tm=tm_proj)

        wq = mla_w_qb[l].reshape(MLA_Q_RANK, MLA_HEADS, MLA_QK_DIM)
        wq_nope = wq[:, :, :MLA_NOPE_DIM].reshape(MLA_Q_RANK, MLA_HEADS // 2, 2 * LANES)
        wq_rope = wq[:, :, MLA_NOPE_DIM:]
        wq_rot = _rot_cols(wq_rope).reshape(MLA_Q_RANK, MLA_HEADS // 2, LANES)
        wq_rope = wq_rope.reshape(MLA_Q_RANK, MLA_HEADS // 2, LANES)
        wq_packed = jnp.concatenate([wq_nope, wq_rope, wq_rot], axis=-1).reshape(
            MLA_Q_RANK, MLA_HEADS * 2 * LANES).T.astype(BF16)
        wkv = mla_w_kvb[l].reshape(MLA_KV_RANK, MLA_HEADS, MLA_NOPE_DIM + MLA_V_DIM)
        wkn = wkv[:, :, :MLA_NOPE_DIM].reshape(MLA_KV_RANK, MLA_HEADS * MLA_NOPE_DIM).astype(BF16)
        wv = wkv[:, :, MLA_NOPE_DIM:].reshape(MLA_KV_RANK, MLA_HEADS * MLA_V_DIM).T.astype(BF16)
        mla_scale = MLA_QK_DIM ** -0.5 * LOG2E
        gq, gk = mla_q_norm_g[l], mla_k_norm_g[l]
        q_gains_t = (
            col_tile(gq[:MLA_NOPE_DIM] * mla_scale, tm_mla),
            col_tile(jnp.tile(gq[MLA_NOPE_DIM:], 2) * mla_scale, tm_mla),
            col_tile(jnp.tile(_swap_halves(gq[MLA_NOPE_DIM:]), 2) * mla_scale, tm_mla),
        )
        k_gains = (
            row(gk[:MLA_NOPE_DIM]),
            row(jnp.tile(gk[MLA_NOPE_DIM:], 2)),
            row(jnp.tile(_swap_halves(gk[MLA_NOPE_DIM:]), 2)),
        )
        qn_t, qr_t, kn, kr, v_mla_t = _mla_proj(
            lat, row(mla_q_a_norm_g[l]), row(mla_kv_a_norm_g[l]), wq_packed, wkn, wv,
            q_gains_t, k_gains, cos2, sin2, cos_t, sin_t, tm=tm_mla, seq=seq)

        b3 = lambda a: a.reshape(bsz, seq, a.shape[-1])
        lams = (row(da_lambda_q1[l]), row(da_lambda_k1[l]), row(da_lambda_q2[l]), row(da_lambda_k2[l]))
        y_a = _da_attn(slopes, lams, col_tile(da_subln_g[l], tq), q_da_t, b3(k_da), v_da_t,
                       tq=tq, tk=tk, lambda_init=lambda_init)
        y_b = _mla_attn(qn_t, qr_t, b3(kn), b3(kr), v_mla_t, tq=tq, tk=tk)

        xt = _merge(xt, y_a.reshape(t, -1), y_b.reshape(t, -1), gates,
                    w_branch_a[l].astype(BF16), w_branch_b[l].astype(BF16), w_out[l].astype(BF16),
                    tm=tm, tn=512)

        xt = _ffn(xt, row(ffn2_norm_g[l]), ffn2_w1[l].astype(BF16), ffn2_w3[l].astype(BF16),
                  (0.5 * ffn2_w2[l]).astype(BF16), tm=tm, tf=512)
    return xt.reshape(bsz, seq, d)
```

```python
import functools
import math

import jax
import jax.numpy as jnp
from jax import lax
from jax.experimental import pallas as pl
from jax.experimental.pallas import tpu as pltpu

F32 = jnp.float32
BF16 = jnp.bfloat16

EPS = 1e-6
CHUNK = 64
CHUNK_SHIFT = CHUNK.bit_length() - 1
DA_HEADS = 8
DA_HEAD_DIM = 128
DA_V_DIM = 2 * DA_HEAD_DIM
MLA_HEADS = 16
MLA_Q_RANK = 768
MLA_KV_RANK = 512
MLA_NOPE_DIM = 128
MLA_ROPE_DIM = 64
MLA_V_DIM = 128
MLA_QK_DIM = MLA_NOPE_DIM + MLA_ROPE_DIM
ROPE_THETA = 10000.0
LOG2E = math.log2(math.e)
MASK_VALUE = -1e30

LANES = 128
SUBLANES = 8
MXU_WIDTH = 256
V7X_VMEM_BYTES = 64 * 1024 * 1024
VMEM_LIMIT = V7X_VMEM_BYTES - 8 * 1024 * 1024


def _cparams(sem):
    return pltpu.CompilerParams(dimension_semantics=sem, vmem_limit_bytes=VMEM_LIMIT)


def _rms(x, g):
    ms = jnp.mean(x * x, axis=-1, keepdims=True)
    return x * lax.rsqrt(ms + EPS) * g


def _tn_dot(w, h):
    return lax.dot_general(w, h, (((0,), (1,)), ((), ())), preferred_element_type=F32)


def _nt_dot(a, b):
    return lax.dot_general(a, b, (((1,), (1,)), ((), ())), preferred_element_type=F32)


def _ffn_body(x_ref, g_ref, w1_ref, w3_ref, w2_ref, o_ref, h_ref):
    @pl.when(pl.program_id(1) == 0)
    def _():
        x = x_ref[...]
        h_ref[...] = _rms(x, g_ref[...]).astype(BF16)
        o_ref[...] = x

    h = h_ref[...]
    a = jnp.dot(h, w1_ref[...], preferred_element_type=F32)
    b = jnp.dot(h, w3_ref[...], preferred_element_type=F32)
    u = (a * jax.nn.sigmoid(a) * b).astype(BF16)
    o_ref[...] += jnp.dot(u, w2_ref[...], preferred_element_type=F32)


def _ffn(x, g, w1, w3, w2_half, *, tm, tf):
    t, d = x.shape
    f = w1.shape[1]
    return pl.pallas_call(
        _ffn_body,
        grid=(t // tm, f // tf),
        in_specs=[
            pl.BlockSpec((tm, d), lambda i, j: (i, 0)),
            pl.BlockSpec((1, d), lambda i, j: (0, 0)),
            pl.BlockSpec((d, tf), lambda i, j: (0, j)),
            pl.BlockSpec((d, tf), lambda i, j: (0, j)),
            pl.BlockSpec((tf, d), lambda i, j: (j, 0)),
        ],
        out_specs=pl.BlockSpec((tm, d), lambda i, j: (i, 0)),
        out_shape=jax.ShapeDtypeStruct((t, d), F32),
        scratch_shapes=[pltpu.VMEM((tm, d), BF16)],
        compiler_params=_cparams(("parallel", "arbitrary")),
        name="ffn",
    )(x, g, w1, w3, w2_half)


_W_IN_QKV = 2 * DA_HEADS * DA_HEAD_DIM * 2 + DA_HEADS * DA_V_DIM
_W_IN_KROPE = _W_IN_QKV + MLA_Q_RANK + MLA_KV_RANK
_W_IN_GATE = _W_IN_KROPE + MLA_ROPE_DIM
REPACK_ROWS = 128


def _repack_body(w_ref, qkv_ref, lat_ref, gate_ref):
    w = w_ref[...]
    qkv_ref[...] = w[:, :_W_IN_QKV].astype(BF16)
    k_rope = w[:, _W_IN_KROPE:_W_IN_GATE]
    k_rot = _rot_cols(k_rope)
    lat_ref[...] = jnp.concatenate(
        [w[:, _W_IN_QKV:_W_IN_KROPE], k_rope, k_rope, k_rot, k_rot], axis=1).astype(BF16)
    gate_ref[...] = w[:, _W_IN_GATE:].astype(BF16)


def _repack_w_in(w, d):
    r, n_in = w.shape
    tr = _pick_tile(r, REPACK_ROWS)
    n_lat = _W_IN_KROPE - _W_IN_QKV + 4 * MLA_ROPE_DIM
    out = lambda n: pl.BlockSpec((tr, n), lambda i: (i, 0))
    return pl.pallas_call(
        _repack_body,
        grid=(r // tr,),
        in_specs=[pl.BlockSpec((tr, n_in), lambda i: (i, 0))],
        out_specs=[out(_W_IN_QKV), out(n_lat), out(2 * d)],
        out_shape=[jax.ShapeDtypeStruct((r, _W_IN_QKV), BF16),
                   jax.ShapeDtypeStruct((r, n_lat), BF16),
                   jax.ShapeDtypeStruct((r, 2 * d), BF16)],
        compiler_params=_cparams(("parallel",)),
        name="repack_w_in",
    )(w)


PROJ_TN = 512
_Q0, _K0, _V0, _C0, _G0, _NJ = 0, 4, 8, 12, 15, 23


def _head_norm(acc, g):
    parts = []
    for c in range(acc.shape[1] // LANES):
        blk = acc[:, c * LANES:(c + 1) * LANES]
        ms = jnp.mean(blk * blk, axis=-1, keepdims=True)
        parts.append(blk * lax.rsqrt(ms + EPS) * g)
    return jnp.concatenate(parts, axis=-1)


def _head_norm_t(acc_t, g_t):
    parts = []
    for c in range(acc_t.shape[0] // LANES):
        blk = acc_t[c * LANES:(c + 1) * LANES, :]
        ms = jnp.mean(blk * blk, axis=0, keepdims=True)
        parts.append(blk * lax.rsqrt(ms + EPS) * g_t)
    return jnp.concatenate(parts, axis=0)


def _proj_body(x_ref, g_ref, wqkv_ref, wlat_ref, wgate_ref, qg_ref, kg_ref,
               qt_ref, k_ref, vt_ref, c_ref, gate_ref, h_ref):
    j = pl.program_id(1)

    @pl.when(j == 0)
    def _():
        h_ref[...] = _rms(x_ref[...], g_ref[...]).astype(BF16)

    @pl.when(j < _K0)
    def _():
        qt_ref[...] = _head_norm_t(_tn_dot(wqkv_ref[...], h_ref[...]), qg_ref[...]).astype(BF16)

    @pl.when((j >= _K0) & (j < _V0))
    def _():
        acc = jnp.dot(h_ref[...], wqkv_ref[...], preferred_element_type=F32)
        k_ref[...] = _head_norm(acc, kg_ref[...]).astype(BF16)

    @pl.when((j >= _V0) & (j < _C0))
    def _():
        vt_ref[...] = _tn_dot(wqkv_ref[...], h_ref[...]).astype(BF16)

    @pl.when((j >= _C0) & (j < _G0))
    def _():
        c_ref[...] = jnp.dot(h_ref[...], wlat_ref[...], preferred_element_type=F32)

    @pl.when(j >= _G0)
    def _():
        acc = jnp.dot(h_ref[...], wgate_ref[...], preferred_element_type=F32)
        gate_ref[...] = jax.nn.sigmoid(acc).astype(BF16)


def _in_proj(x, g, w_qkv, w_lat, w_gate, qg_t, kg, *, tm):
    t, d = x.shape
    tn = PROJ_TN
    n_q, n_k, n_v, n_c, n_g = _K0 - _Q0, _V0 - _K0, _C0 - _V0, _G0 - _C0, _NJ - _G0

    def w_cols(j0, n):
        return pl.BlockSpec((d, tn), lambda i, j: (0, jnp.clip(j - j0, 0, n - 1)))

    def cols(j0, n):
        return pl.BlockSpec((tm, tn), lambda i, j: (i, jnp.clip(j - j0, 0, n - 1)))

    def rows(j0, n):
        return pl.BlockSpec((tn, tm), lambda i, j: (jnp.clip(j - j0, 0, n - 1), i))

    return pl.pallas_call(
        _proj_body,
        grid=(t // tm, _NJ),
        in_specs=[
            pl.BlockSpec((tm, d), lambda i, j: (i, 0)),
            pl.BlockSpec((1, d), lambda i, j: (0, 0)),
            w_cols(_Q0, n_q + n_k + n_v), w_cols(_C0, n_c), w_cols(_G0, n_g),
            pl.BlockSpec((LANES, tm), lambda i, j: (0, 0)),
            pl.BlockSpec((1, LANES), lambda i, j: (0, 0)),
        ],
        out_specs=[rows(_Q0, n_q), cols(_K0, n_k), rows(_V0, n_v), cols(_C0, n_c), cols(_G0, n_g)],
        out_shape=[
            jax.ShapeDtypeStruct((n_q * tn, t), BF16),
            jax.ShapeDtypeStruct((t, n_k * tn), BF16),
            jax.ShapeDtypeStruct((n_v * tn, t), BF16),
            jax.ShapeDtypeStruct((t, n_c * tn), F32),
            jax.ShapeDtypeStruct((t, n_g * tn), BF16),
        ],
        scratch_shapes=[pltpu.VMEM((tm, d), BF16)],
        compiler_params=_cparams(("parallel", "arbitrary")),
        name="in_proj",
    )(x, g, w_qkv, w_lat, w_gate, qg_t, kg)


def _mla_proj_body(c_ref, qa_g_ref, kva_g_ref, wq_ref, wkn_ref, wv_ref,
                   gqn_ref, gqr_ref, gqt_ref, gk_nope_ref, gk_rope_ref, gk_rot_ref,
                   cos_ref, sin_ref, cost_ref, sint_ref,
                   qnt_ref, qrt_ref, kn_ref, kr_ref, vt_ref):
    c = c_ref[...]
    o_kv = MLA_Q_RANK
    o_kr = MLA_Q_RANK + MLA_KV_RANK
    cqn = _rms(c[:, :o_kv], qa_g_ref[...]).astype(BF16)
    ckvn = _rms(c[:, o_kv:o_kr], kva_g_ref[...]).astype(BF16)
    k_rope = c[:, o_kr:o_kr + LANES]
    k_rot = c[:, o_kr + LANES:o_kr + 2 * LANES]
    first = lax.broadcasted_iota(jnp.int32, (1, LANES), 1) < MLA_ROPE_DIM
    k_roped = k_rope * gk_rope_ref[...] * cos_ref[...] + k_rot * gk_rot_ref[...] * sin_ref[...]
    k_rope_sq = 0.5 * jnp.sum(k_rope * k_rope, axis=-1, keepdims=True)
    inv_d = 1.0 / MLA_QK_DIM
    cos_t = cost_ref[...]
    sin_t = sint_ref[...]
    half = MLA_ROPE_DIM

    for p in range(MLA_HEADS // 2):
        qraw = _nt_dot(wq_ref[p * 4 * LANES:(p + 1) * 4 * LANES, :], cqn)
        na, nb = qraw[:LANES], qraw[LANES:2 * LANES]
        rp, rt = qraw[2 * LANES:3 * LANES], qraw[3 * LANES:]
        rp2 = rp * rp
        sa = jnp.sum(na * na, axis=0, keepdims=True) + jnp.sum(rp2[:half], axis=0, keepdims=True)
        sb = jnp.sum(nb * nb, axis=0, keepdims=True) + jnp.sum(rp2[half:], axis=0, keepdims=True)
        ra = lax.rsqrt(sa * inv_d + EPS)
        rb = lax.rsqrt(sb * inv_d + EPS)
        ra_rows = slice((2 * p) * LANES, (2 * p + 1) * LANES)
        rb_rows = slice((2 * p + 1) * LANES, (2 * p + 2) * LANES)
        qnt_ref[ra_rows, :] = (na * ra * gqn_ref[...]).astype(BF16)
        qnt_ref[rb_rows, :] = (nb * rb * gqn_ref[...]).astype(BF16)
        roped = rp * gqr_ref[...] * cos_t + rt * gqt_ref[...] * sin_t
        qrt_ref[p * LANES:(p + 1) * LANES, :] = jnp.concatenate(
            [roped[:half] * ra, roped[half:] * rb], axis=0).astype(BF16)

        knraw = jnp.dot(ckvn, wkn_ref[:, p * 2 * LANES:(p + 1) * 2 * LANES], preferred_element_type=F32)
        kna, knb = knraw[:, :LANES], knraw[:, LANES:]
        rka = lax.rsqrt((jnp.sum(kna * kna, axis=-1, keepdims=True) + k_rope_sq) * inv_d + EPS)
        rkb = lax.rsqrt((jnp.sum(knb * knb, axis=-1, keepdims=True) + k_rope_sq) * inv_d + EPS)
        kn_ref[:, ra_rows] = (kna * rka * gk_nope_ref[...]).astype(BF16)
        kn_ref[:, rb_rows] = (knb * rkb * gk_nope_ref[...]).astype(BF16)
        kr_ref[:, ra_rows] = jnp.where(first, k_roped * rka, 0.0).astype(BF16)
        kr_ref[:, rb_rows] = jnp.where(first, 0.0, k_roped * rkb).astype(BF16)

        vt_ref[p * 2 * LANES:(p + 1) * 2 * LANES, :] = _nt_dot(
            wv_ref[p * 2 * LANES:(p + 1) * 2 * LANES, :], ckvn).astype(BF16)


def _mla_proj(c, qa_g, kva_g, wq, wkn, wv, q_gains_t, k_gains, cos2, sin2, cos_t, sin_t, *, tm, seq):
    t = c.shape[0]
    n_pos = seq // tm
    const = lambda a: pl.BlockSpec(a.shape, lambda i: (0, 0))
    hw = MLA_HEADS * LANES
    tok = lambda w: pl.BlockSpec((tm, w), lambda i: (i, 0))
    tok_t = lambda w: pl.BlockSpec((w, tm), lambda i: (0, i))
    return pl.pallas_call(
        _mla_proj_body,
        grid=(t // tm,),
        in_specs=[tok(c.shape[1]), const(qa_g), const(kva_g), const(wq), const(wkn), const(wv)]
        + [const(a) for a in q_gains_t] + [const(a) for a in k_gains] + [
            pl.BlockSpec((tm, LANES), lambda i: (i % n_pos, 0)),
            pl.BlockSpec((tm, LANES), lambda i: (i % n_pos, 0)),
            pl.BlockSpec((LANES, tm), lambda i: (0, i % n_pos)),
            pl.BlockSpec((LANES, tm), lambda i: (0, i % n_pos)),
        ],
        out_specs=[tok_t(hw), tok_t(hw // 2), tok(hw), tok(hw), tok_t(hw)],
        out_shape=[
            jax.ShapeDtypeStruct((hw, t), BF16),
            jax.ShapeDtypeStruct((hw // 2, t), BF16),
            jax.ShapeDtypeStruct((t, hw), BF16),
            jax.ShapeDtypeStruct((t, hw), BF16),
            jax.ShapeDtypeStruct((hw, t), BF16),
        ],
        compiler_params=_cparams(("parallel",)),
        name="mla_proj",
    )(c, qa_g, kva_g, wq, wkn, wv, *q_gains_t, *k_gains, cos2, sin2, cos_t, sin_t)


def _score_pass(k_tile, q_t, tile_spec, tile_off, n_tiles, s_ref):
    tq = q_t.shape[1]
    m8 = None
    for n in range(n_tiles):
        add, keep = tile_spec(n)
        u = jnp.dot(k_tile(n), q_t, preferred_element_type=F32)
        if add is not None:
            u = u + add
        if keep is not None:
            u = jnp.where(keep, u, MASK_VALUE)
        s_ref[n * MXU_WIDTH:(n + 1) * MXU_WIDTH, :] = u
        t = jnp.max(u.reshape(MXU_WIDTH // SUBLANES, SUBLANES, tq), axis=0)
        off = tile_off(n)
        if off is not None:
            t = t + off
        m8 = t if m8 is None else jnp.maximum(m8, t)
    return jnp.max(m8, axis=0, keepdims=True)


def _prob_pass(s_ref, p_ref, m, l, m_blk, tile_off, n_tiles):
    tq = s_ref.shape[1]
    m_new = jnp.maximum(m, m_blk)
    alpha = jnp.exp2(m - m_new)
    ps8 = None
    for n in range(n_tiles):
        rows = slice(n * MXU_WIDTH, (n + 1) * MXU_WIDTH)
        off = tile_off(n)
        ref = m_new if off is None else m_new - off
        p = jnp.exp2(s_ref[rows, :] - ref)
        t = jnp.sum(p.reshape(MXU_WIDTH // SUBLANES, SUBLANES, tq), axis=0)
        ps8 = t if ps8 is None else ps8 + t
        p_ref[rows, :] = p.astype(BF16)
    l_new = alpha * l + jnp.sum(ps8, axis=0, keepdims=True)
    return m_new, l_new, alpha


def _pipelined_key_blocks(n_full, subs, tq):
    neg = jnp.full((1, tq), MASK_VALUE, F32)
    zero = jnp.zeros((1, tq), F32)
    init = (neg, zero, neg, zero)

    @pl.when(n_full == 0)
    def _():
        maxima = [score(0, r) for r, (score, _, _) in enumerate(subs)]
        for r, (_, prob_pv, finalize) in enumerate(subs):
            finalize(prob_pv(0, r, init, maxima[r]))

    @pl.when(n_full > 0)
    def _():
        mx = subs[0][0](0, None)
        for r, (score, prob_pv, finalize) in enumerate(subs):
            def body(j, carry, score=score, prob_pv=prob_pv):
                state, mx = carry
                state = prob_pv(j, None, state, mx)
                return state, score(j + 1, None)

            state, mx = lax.fori_loop(0, n_full - 1, body, (init, mx))
            state = prob_pv(n_full - 1, None, state, mx)
            mx_last = score(n_full, r)
            if r + 1 < len(subs):
                mx = subs[r + 1][0](0, None)
            finalize(prob_pv(n_full, r, state, mx_last))


def _block_tiles(r, tq, tk):
    per_q = tq // MXU_WIDTH
    if r is None:
        return tk // MXU_WIDTH, tk // MXU_WIDTH
    return r * per_q, (r + 1) * per_q


def _own_chunk_mask(key, qry, dq):
    return ((key - dq) >> CHUNK_SHIFT) <= (qry >> CHUNK_SHIFT)


def _da_attn_body(slopes_ref, lq1_ref, lk1_ref, lq2_ref, lk2_ref, sg_ref,
                  qt_ref, k_ref, vt_ref, o_ref, acc_ref, s_ref, p_ref, own_ref,
                  *, tq, tk, lambda_init):
    h = pl.program_id(1)
    n_full = pl.program_id(2)
    slope = slopes_ref[h]
    key = lax.broadcasted_iota(jnp.int32, (MXU_WIDTH, tq), 0)
    key_bias = slope * key.astype(F32)
    acc_ref[...] = jnp.zeros_like(acc_ref)

    @pl.when(n_full == 0)
    def _():
        qry = lax.broadcasted_iota(jnp.int32, (MXU_WIDTH, tq), 1)
        mirror_bias = slope * (2 * qry - key).astype(F32)
        for m in range(tq // MXU_WIDTH):
            dq = -m * MXU_WIDTH
            bias = jnp.minimum(key_bias - slope * dq, mirror_bias + slope * dq)
            own_ref[m] = jnp.where(_own_chunk_mask(key, qry, dq), bias, MASK_VALUE)

    subs = [_da_sub_block(sub, n_full, slope, key_bias, own_ref,
                          lq1_ref, lk1_ref, lq2_ref, lk2_ref, sg_ref, qt_ref, k_ref, vt_ref, o_ref,
                          acc_ref, s_ref, p_ref, tq=tq, tk=tk, lambda_init=lambda_init)
            for sub in range(tk // tq)]
    _pipelined_key_blocks(n_full, subs, tq)


def _da_sub_block(sub, n_full, slope, key_bias, own_ref,
                  lq1_ref, lk1_ref, lq2_ref, lk2_ref, sg_ref, qt_ref, k_ref, vt_ref, o_ref,
                  acc_ref, s_ref, p_ref, *, tq, tk, lambda_init):
    q_cols = slice(sub * tq, (sub + 1) * tq)
    q1_t = qt_ref[:DA_HEAD_DIM, q_cols]
    q2_t = qt_ref[DA_HEAD_DIM:, q_cols]
    q0 = n_full * tk + sub * tq
    acc1_ref, acc2_ref = acc_ref.at[sub, 0], acc_ref.at[sub, 1]
    s_ref, p_ref = s_ref.at[sub], p_ref.at[sub]

    def tile_plan(j, r):
        first_masked, n_tiles = _block_tiles(r, tq, tk)
        specs = {}

        def tile_spec(n):
            if n not in specs:
                specs[n] = (key_bias if n < first_masked else own_ref[n - first_masked], None)
            return specs[n]

        def tile_off(n):
            if n >= first_masked:
                return None
            if r is None:
                return slope * (j * tk + n * MXU_WIDTH - q0).astype(F32)
            return slope * float(n * MXU_WIDTH - r * tq)

        return tile_spec, tile_off, n_tiles

    def score(j, r):
        k0 = pl.multiple_of(j * tk, tk)
        tile_spec, tile_off, n_tiles = tile_plan(j, r)

        def k_tile(lo):
            return lambda n: k_ref[0, pl.ds(pl.multiple_of(k0 + n * MXU_WIDTH, MXU_WIDTH), MXU_WIDTH),
                                   lo:lo + DA_HEAD_DIM]

        return (_score_pass(k_tile(0), q1_t, tile_spec, tile_off, n_tiles, s_ref.at[0]),
                _score_pass(k_tile(DA_HEAD_DIM), q2_t, tile_spec, tile_off, n_tiles, s_ref.at[1]))

    def prob_pv(j, r, state, mx):
        m1, l1, m2, l2 = state
        _, tile_off, n_tiles = tile_plan(j, r)
        m1, l1, a1 = _prob_pass(s_ref.at[0], p_ref.at[0], m1, l1, mx[0], tile_off, n_tiles)
        m2, l2, a2 = _prob_pass(s_ref.at[1], p_ref.at[1], m2, l2, mx[1], tile_off, n_tiles)
        keys = n_tiles * MXU_WIDTH
        vb = vt_ref[:, pl.ds(pl.multiple_of(j * tk, tk), keys)]
        acc1_ref[...] = a1 * acc1_ref[...] + jnp.dot(vb, p_ref[0, :keys, :], preferred_element_type=F32)
        acc2_ref[...] = a2 * acc2_ref[...] + jnp.dot(vb, p_ref[1, :keys, :], preferred_element_type=F32)
        return m1, l1, m2, l2

    def finalize(state):
        _, l1, _, l2 = state
        lam = (jnp.exp(jnp.sum(lq1_ref[...] * lk1_ref[...], axis=-1, keepdims=True))
               - jnp.exp(jnp.sum(lq2_ref[...] * lk2_ref[...], axis=-1, keepdims=True))
               + lambda_init)
        o = acc1_ref[...] * (1.0 / l1) - lam * (acc2_ref[...] * (1.0 / l2))
        ms = jnp.mean(o * o, axis=0, keepdims=True)
        y = o * lax.rsqrt(ms + EPS) * sg_ref[...] * (1.0 - lambda_init)
        o_ref[0, q_cols, :] = y.T.astype(BF16)

    return score, prob_pv, finalize


def _da_attn(slopes, lams, sg_t, q_t, k, v_t, *, tq, tk, lambda_init):
    b, s, _ = k.shape
    w = DA_V_DIM
    nq = s // tk
    n_sub = tk // tq
    vec = pl.BlockSpec((1, DA_HEAD_DIM), lambda bi, h, i: (0, 0))
    return pl.pallas_call(
        functools.partial(_da_attn_body, tq=tq, tk=tk, lambda_init=lambda_init),
        grid=(b, DA_HEADS, nq),
        in_specs=[
            pl.BlockSpec(memory_space=pltpu.SMEM),
            vec, vec, vec, vec,
            pl.BlockSpec((w, tq), lambda bi, h, i: (0, 0)),
            pl.BlockSpec((w, tk), lambda bi, h, i: (h, bi * nq + i)),
            pl.BlockSpec((1, s, w), lambda bi, h, i: (bi, 0, h)),
            pl.BlockSpec((w, s), lambda bi, h, i: (h, bi)),
        ],
        out_specs=pl.BlockSpec((1, tk, w), lambda bi, h, i: (bi, i, h)),
        out_shape=jax.ShapeDtypeStruct((b, s, DA_HEADS * w), BF16),
        scratch_shapes=[pltpu.VMEM((n_sub, 2, w, tq), F32),
                        pltpu.VMEM((n_sub, 2, tk, tq), F32), pltpu.VMEM((n_sub, 2, tk, tq), BF16),
                        pltpu.VMEM((tq // MXU_WIDTH, MXU_WIDTH, tq), F32)],
        compiler_params=_cparams(("parallel", "parallel", "arbitrary")),
        name="da_attn",
    )(slopes, *lams, sg_t, q_t, k, v_t)


def _mla_attn_body(qnt_ref, qrt_ref, kn_ref, kr_ref, vt_ref, o_ref, acc_ref, s_ref, p_ref, own_ref,
                   *, tq, tk):
    n_full = pl.program_id(2)
    acc_ref[...] = jnp.zeros_like(acc_ref)

    @pl.when(n_full == 0)
    def _():
        key = lax.broadcasted_iota(jnp.int32, (MXU_WIDTH, tq), 0)
        qry = lax.broadcasted_iota(jnp.int32, (MXU_WIDTH, tq), 1)
        for m in range(tq // MXU_WIDTH):
            own_ref[m] = jnp.where(_own_chunk_mask(key, qry, -m * MXU_WIDTH), 0.0, MASK_VALUE)

    subs = [_mla_sub_block(sub, own_ref, qnt_ref, qrt_ref, kn_ref, kr_ref, vt_ref, o_ref,
                           acc_ref, s_ref, p_ref, tq=tq, tk=tk) for sub in range(tk // tq)]
    _pipelined_key_blocks(n_full, subs, tq)


def _mla_sub_block(sub, own_ref, qnt_ref, qrt_ref, kn_ref, kr_ref, vt_ref, o_ref,
                   acc_ref, s_ref, p_ref, *, tq, tk):
    q_cols = slice(sub * tq, (sub + 1) * tq)
    qr_t = qrt_ref[:, q_cols]
    qa_t = jnp.concatenate([qnt_ref[:LANES, q_cols], qr_t], axis=0)
    qb_t = jnp.concatenate([qnt_ref[LANES:, q_cols], qr_t], axis=0)
    acc_a_ref, acc_b_ref = acc_ref.at[sub, 0], acc_ref.at[sub, 1]
    s_ref, p_ref = s_ref.at[sub], p_ref.at[sub]
    no_off = lambda n: None

    def score(j, r):
        k0 = pl.multiple_of(j * tk, tk)
        first_masked, n_tiles = _block_tiles(r, tq, tk)
        specs = {}

        def tile_spec(n):
            if n not in specs:
                specs[n] = (None if n < first_masked else own_ref[n - first_masked], None)
            return specs[n]

        def k_tile(lo):
            def tile(n):
                rows = pl.ds(pl.multiple_of(k0 + n * MXU_WIDTH, MXU_WIDTH), MXU_WIDTH)
                return jnp.concatenate([kn_ref[0, rows, lo:lo + LANES],
                                        kr_ref[0, rows, lo:lo + LANES]], axis=-1)
            return tile

        return (_score_pass(k_tile(0), qa_t, tile_spec, no_off, n_tiles, s_ref.at[0]),
                _score_pass(k_tile(LANES), qb_t, tile_spec, no_off, n_tiles, s_ref.at[1]))

    def prob_pv(j, r, state, mx):
        ma, la, mb, lb = state
        _, n_tiles = _block_tiles(r, tq, tk)
        ma, la, aa = _prob_pass(s_ref.at[0], p_ref.at[0], ma, la, mx[0], no_off, n_tiles)
        mb, lb, ab = _prob_pass(s_ref.at[1], p_ref.at[1], mb, lb, mx[1], no_off, n_tiles)
        keys = n_tiles * MXU_WIDTH
        cols = pl.ds(pl.multiple_of(j * tk, tk), keys)
        acc_a_ref[...] = aa * acc_a_ref[...] + jnp.dot(vt_ref[:LANES, cols], p_ref[0, :keys, :],
                                                      preferred_element_type=F32)
        acc_b_ref[...] = ab * acc_b_ref[...] + jnp.dot(vt_ref[LANES:, cols], p_ref[1, :keys, :],
                                                      preferred_element_type=F32)
        return ma, la, mb, lb

    def finalize(state):
        _, la, _, lb = state
        o_t = jnp.concatenate([acc_a_ref[...] * (1.0 / la), acc_b_ref[...] * (1.0 / lb)], axis=0)
        o_ref[0, q_cols, :] = o_t.T.astype(BF16)

    return score, prob_pv, finalize


def _mla_attn(qn_t, qr_t, kn, kr, v_t, *, tq, tk):
    b, s, _ = kn.shape
    w = 2 * LANES
    nq = s // tk
    n_sub = tk // tq
    kv_spec = pl.BlockSpec((1, s, w), lambda bi, h, i: (bi, 0, h))
    return pl.pallas_call(
        functools.partial(_mla_attn_body, tq=tq, tk=tk),
        grid=(b, MLA_HEADS // 2, nq),
        in_specs=[
            pl.BlockSpec((w, tk), lambda bi, h, i: (h, bi * nq + i)),
            pl.BlockSpec((LANES, tk), lambda bi, h, i: (h, bi * nq + i)),
            kv_spec, kv_spec,
            pl.BlockSpec((w, s), lambda bi, h, i: (h, bi)),
        ],
        out_specs=pl.BlockSpec((1, tk, w), lambda bi, h, i: (bi, i, h)),
        out_shape=jax.ShapeDtypeStruct((b, s, MLA_HEADS * LANES), BF16),
        scratch_shapes=[pltpu.VMEM((n_sub, 2, LANES, tq), F32),
                        pltpu.VMEM((n_sub, 2, tk, tq), F32), pltpu.VMEM((n_sub, 2, tk, tq), BF16),
                        pltpu.VMEM((tq // MXU_WIDTH, MXU_WIDTH, tq), F32)],
        compiler_params=_cparams(("parallel", "parallel", "arbitrary")),
        name="mla_attn",
    )(qn_t, qr_t, kn, kr, v_t)


def _wide_score_pass(k_tile, q_t, plan, s_ref, tq):
    m8 = None
    for n, desc in enumerate(plan):
        rows = slice(n * MXU_WIDTH, (n + 1) * MXU_WIDTH)
        if desc[0] == "wide":
            _, bias_lo, bias_hi, off_row = desc
            u = jnp.dot(k_tile(n), q_t, preferred_element_type=F32)
            if bias_lo is not None or bias_hi is not None:
                lo, hi = u[:, :tq], u[:, tq:]
                lo = lo if bias_lo is None else lo + bias_lo
                hi = hi if bias_hi is None else hi + bias_hi
                u = jnp.concatenate([lo, hi], axis=1)
            s_ref[rows, :] = u
            t = jnp.max(u.reshape(MXU_WIDTH // SUBLANES, SUBLANES, 2 * tq), axis=0)
            if off_row is not None:
                t = t + off_row
            m8 = t if m8 is None else jnp.maximum(m8, t)
        else:
            u = jnp.dot(k_tile(n), q_t[:, tq:], preferred_element_type=F32) + desc[1]
            s_ref[rows, tq:] = u
            t = jnp.max(u.reshape(MXU_WIDTH // SUBLANES, SUBLANES, tq), axis=0)
            m8 = jnp.concatenate([m8[:, :tq], jnp.maximum(m8[:, tq:], t)], axis=1)
    return jnp.max(m8, axis=0, keepdims=True)


def _wide_prob_pass(s_ref, p_ref, m, l, m_blk, plan, tq):
    m_new = jnp.maximum(m, m_blk)
    alpha = jnp.exp2(m - m_new)
    ps8 = None
    for n, desc in enumerate(plan):
        rows = slice(n * MXU_WIDTH, (n + 1) * MXU_WIDTH)
        if desc[0] == "wide":
            off_row = desc[3]
            ref = m_new if off_row is None else m_new - off_row
            p = jnp.exp2(s_ref[rows, :] - ref)
            t = jnp.sum(p.reshape(MXU_WIDTH // SUBLANES, SUBLANES, 2 * tq), axis=0)
            ps8 = t if ps8 is None else ps8 + t
            p_ref[rows, :] = p.astype(BF16)
        else:
            p = jnp.exp2(s_ref[rows, tq:] - m_new[:, tq:])
            t = jnp.sum(p.reshape(MXU_WIDTH // SUBLANES, SUBLANES, tq), axis=0)
            ps8 = jnp.concatenate([ps8[:, :tq], ps8[:, tq:] + t], axis=1)
            p_ref[rows, tq:] = p.astype(BF16)
    l_new = alpha * l + jnp.sum(ps8, axis=0, keepdims=True)
    return m_new, l_new, alpha


def _wide_pv(acc_ref, alpha, vt_ref, v_rows, k0, p_ref, plan, tq):
    n_wide = sum(1 for d in plan if d[0] == "wide")
    n_hi = len(plan) - n_wide
    kw = n_wide * MXU_WIDTH
    acc_ref[...] = alpha * acc_ref[...] + jnp.dot(
        vt_ref[v_rows, pl.ds(k0, kw)], p_ref[:kw, :], preferred_element_type=F32)
    if n_hi:
        kh = n_hi * MXU_WIDTH
        acc_ref[:, tq:] += jnp.dot(vt_ref[v_rows, pl.ds(pl.multiple_of(k0 + kw, MXU_WIDTH), kh)],
                                   p_ref[kw:kw + kh, tq:], preferred_element_type=F32)


def _wide_pipeline(n_full, score, prob_pv, finalize, width):
    neg = jnp.full((1, width), MASK_VALUE, F32)
    zero = jnp.zeros((1, width), F32)
    init = (neg, zero, neg, zero)

    @pl.when(n_full == 0)
    def _():
        finalize(prob_pv(0, True, init, score(0, True)))

    @pl.when(n_full > 0)
    def _():
        def body(j, carry):
            state, mx = carry
            state = prob_pv(j, False, state, mx)
            return state, score(j + 1, False)

        state, mx = lax.fori_loop(0, n_full - 1, body, (init, score(0, False)))
        state = prob_pv(n_full - 1, False, state, mx)
        mx_last = score(n_full, True)
        finalize(prob_pv(n_full, True, state, mx_last))


def _da_wide_body(slopes_ref, lq1_ref, lk1_ref, lq2_ref, lk2_ref, sg_ref,
                  qt_ref, k_ref, vt_ref, o_ref, acc_ref, s_ref, p_ref, own_ref,
                  *, tq, tk, lambda_init):
    h = pl.program_id(1)
    n_full = pl.program_id(2)
    slope = slopes_ref[h]
    per_q = tq // MXU_WIDTH
    q0 = n_full * tk
    key = lax.broadcasted_iota(jnp.int32, (MXU_WIDTH, tq), 0)
    key_bias = slope * key.astype(F32)
    lane = lax.broadcasted_iota(jnp.int32, (1, tk), 1)
    acc_ref[...] = jnp.zeros_like(acc_ref)

    @pl.when(n_full == 0)
    def _():
        qry = lax.broadcasted_iota(jnp.int32, (MXU_WIDTH, tq), 1)
        mirror_bias = slope * (2 * qry - key).astype(F32)
        for m in range(per_q):
            dq = -m * MXU_WIDTH
            bias = jnp.minimum(key_bias - slope * dq, mirror_bias + slope * dq)
            own_ref[m] = jnp.where(_own_chunk_mask(key, qry, dq), bias, MASK_VALUE)

    def plan(j, last):
        tiles = []
        for n in range(tk // MXU_WIDTH):
            if not last:
                off = slope * (j * tk + n * MXU_WIDTH - q0).astype(F32)
                tiles.append(("wide", key_bias, key_bias,
                              off - jnp.where(lane >= tq, slope * tq, 0.0)))
            elif n < per_q:
                off_hi = slope * float(n * MXU_WIDTH - tq)
                tiles.append(("wide", own_ref[n], key_bias, jnp.where(lane >= tq, off_hi, 0.0)))
            else:
                tiles.append(("hi", own_ref[n - per_q]))
        return tiles

    def score(j, last):
        k0 = pl.multiple_of(j * tk, tk)
        tiles = plan(j, last)

        def k_tile(lo):
            return lambda n: k_ref[0, pl.ds(pl.multiple_of(k0 + n * MXU_WIDTH, MXU_WIDTH), MXU_WIDTH),
                                   lo:lo + DA_HEAD_DIM]

        return (_wide_score_pass(k_tile(0), qt_ref[:DA_HEAD_DIM, :], tiles, s_ref.at[0], tq),
                _wide_score_pass(k_tile(DA_HEAD_DIM), qt_ref[DA_HEAD_DIM:, :], tiles, s_ref.at[1], tq))

    def prob_pv(j, last, state, mx):
        m1, l1, m2, l2 = state
        tiles = plan(j, last)
        k0 = pl.multiple_of(j * tk, tk)
        m1, l1, a1 = _wide_prob_pass(s_ref.at[0], p_ref.at[0], m1, l1, mx[0], tiles, tq)
        m2, l2, a2 = _wide_prob_pass(s_ref.at[1], p_ref.at[1], m2, l2, mx[1], tiles, tq)
        every = slice(None)
        _wide_pv(acc_ref.at[0], a1, vt_ref, every, k0, p_ref.at[0], tiles, tq)
        _wide_pv(acc_ref.at[1], a2, vt_ref, every, k0, p_ref.at[1], tiles, tq)
        return m1, l1, m2, l2

    def finalize(state):
        _, l1, _, l2 = state
        lam = (jnp.exp(jnp.sum(lq1_ref[...] * lk1_ref[...], axis=-1, keepdims=True))
               - jnp.exp(jnp.sum(lq2_ref[...] * lk2_ref[...], axis=-1, keepdims=True))
               + lambda_init)
        o = acc_ref[0] * (1.0 / l1) - lam * (acc_ref[1] * (1.0 / l2))
        ms = jnp.mean(o * o, axis=0, keepdims=True)
        y = o * lax.rsqrt(ms + EPS) * sg_ref[...] * (1.0 - lambda_init)
        o_ref[0] = y.T.astype(BF16)

    _wide_pipeline(n_full, score, prob_pv, finalize, tk)


def _da_wide_attn(slopes, lams, sg_t, q_t, k, v_t, *, tq, tk, lambda_init):
    assert tk == 2 * tq
    b, s, _ = k.shape
    w = DA_V_DIM
    nq = s // tk
    vec = pl.BlockSpec((1, DA_HEAD_DIM), lambda bi, h, i: (0, 0))
    return pl.pallas_call(
        functools.partial(_da_wide_body, tq=tq, tk=tk, lambda_init=lambda_init),
        grid=(b, DA_HEADS, nq),
        in_specs=[
            pl.BlockSpec(memory_space=pltpu.SMEM),
            vec, vec, vec, vec,
            pl.BlockSpec((w, tk), lambda bi, h, i: (0, 0)),
            pl.BlockSpec((w, tk), lambda bi, h, i: (h, bi * nq + i)),
            pl.BlockSpec((1, s, w), lambda bi, h, i: (bi, 0, h)),
            pl.BlockSpec((w, s), lambda bi, h, i: (h, bi)),
        ],
        out_specs=pl.BlockSpec((1, tk, w), lambda bi, h, i: (bi, i, h)),
        out_shape=jax.ShapeDtypeStruct((b, s, DA_HEADS * w), BF16),
        scratch_shapes=[pltpu.VMEM((2, w, tk), F32),
                        pltpu.VMEM((2, tk, tk), F32), pltpu.VMEM((2, tk, tk), BF16),
                        pltpu.VMEM((tq // MXU_WIDTH, MXU_WIDTH, tq), F32)],
        compiler_params=_cparams(("parallel", "parallel", "arbitrary")),
        name="da_attn",
    )(slopes, *lams, sg_t, q_t, k, v_t)


def _mla_wide_body(qnt_ref, qrt_ref, kn_ref, kr_ref, vt_ref, o_ref, acc_ref, s_ref, p_ref, own_ref,
                   *, tq, tk):
    n_full = pl.program_id(2)
    per_q = tq // MXU_WIDTH
    acc_ref[...] = jnp.zeros_like(acc_ref)

    @pl.when(n_full == 0)
    def _():
        key = lax.broadcasted_iota(jnp.int32, (MXU_WIDTH, tq), 0)
        qry = lax.broadcasted_iota(jnp.int32, (MXU_WIDTH, tq), 1)
        for m in range(per_q):
            own_ref[m] = jnp.where(_own_chunk_mask(key, qry, -m * MXU_WIDTH), 0.0, MASK_VALUE)

    qr_t = qrt_ref[...]
    qa_t = jnp.concatenate([qnt_ref[:LANES, :], qr_t], axis=0)
    qb_t = jnp.concatenate([qnt_ref[LANES:, :], qr_t], axis=0)

    def plan(last):
        tiles = []
        for n in range(tk // MXU_WIDTH):
            if not last:
                tiles.append(("wide", None, None, None))
            elif n < per_q:
                tiles.append(("wide", own_ref[n], None, None))
            else:
                tiles.append(("hi", own_ref[n - per_q]))
        return tiles

    def score(j, last):
        k0 = pl.multiple_of(j * tk, tk)
        tiles = plan(last)

        def k_tile(lo):
            def tile(n):
                rows = pl.ds(pl.multiple_of(k0 + n * MXU_WIDTH, MXU_WIDTH), MXU_WIDTH)
                return jnp.concatenate([kn_ref[0, rows, lo:lo + LANES],
                                        kr_ref[0, rows, lo:lo + LANES]], axis=-1)
            return tile

        return (_wide_score_pass(k_tile(0), qa_t, tiles, s_ref.at[0], tq),
                _wide_score_pass(k_tile(LANES), qb_t, tiles, s_ref.at[1], tq))

    def prob_pv(j, last, state, mx):
        ma, la, mb, lb = state
        tiles = plan(last)
        k0 = pl.multiple_of(j * tk, tk)
        ma, la, aa = _wide_prob_pass(s_ref.at[0], p_ref.at[0], ma, la, mx[0], tiles, tq)
        mb, lb, ab = _wide_prob_pass(s_ref.at[1], p_ref.at[1], mb, lb, mx[1], tiles, tq)
        _wide_pv(acc_ref.at[0], aa, vt_ref, slice(0, LANES), k0, p_ref.at[0], tiles, tq)
        _wide_pv(acc_ref.at[1], ab, vt_ref, slice(LANES, 2 * LANES), k0, p_ref.at[1], tiles, tq)
        return ma, la, mb, lb

    def finalize(state):
        _, la, _, lb = state
        o_t = jnp.concatenate([acc_ref[0] * (1.0 / la), acc_ref[1] * (1.0 / lb)], axis=0)
        o_ref[0] = o_t.T.astype(BF16)

    _wide_pipeline(n_full, score, prob_pv, finalize, tk)


def _mla_wide_attn(qn_t, qr_t, kn, kr, v_t, *, tq, tk):
    assert tk == 2 * tq
    b, s, _ = kn.shape
    w = 2 * LANES
    nq = s // tk
    kv_spec = pl.BlockSpec((1, s, w), lambda bi, h, i: (bi, 0, h))
    return pl.pallas_call(
        functools.partial(_mla_wide_body, tq=tq, tk=tk),
        grid=(b, MLA_HEADS // 2, nq),
        in_specs=[
            pl.BlockSpec((w, tk), lambda bi, h, i: (h, bi * nq + i)),
            pl.BlockSpec((LANES, tk), lambda bi, h, i: (h, bi * nq + i)),
            kv_spec, kv_spec,
            pl.BlockSpec((w, s), lambda bi, h, i: (h, bi)),
        ],
        out_specs=pl.BlockSpec((1, tk, w), lambda bi, h, i: (bi, i, h)),
        out_shape=jax.ShapeDtypeStruct((b, s, MLA_HEADS * LANES), BF16),
        scratch_shapes=[pltpu.VMEM((2, LANES, tk), F32),
                        pltpu.VMEM((2, tk, tk), F32), pltpu.VMEM((2, tk, tk), BF16),
                        pltpu.VMEM((tq // MXU_WIDTH, MXU_WIDTH, tq), F32)],
        compiler_params=_cparams(("parallel", "parallel", "arbitrary")),
        name="mla_attn",
    )(qn_t, qr_t, kn, kr, v_t)


def _merge_body(x_ref, ya_ref, yb_ref, gate_a_ref, gate_b_ref, wa_ref, wb_ref, wo_ref, o_ref):
    @pl.when(pl.program_id(1) == 0)
    def _():
        o_ref[...] = x_ref[...]

    a = jnp.dot(ya_ref[...], wa_ref[...], preferred_element_type=F32)
    b = jnp.dot(yb_ref[...], wb_ref[...], preferred_element_type=F32)
    m = (gate_a_ref[...].astype(F32) * a + gate_b_ref[...].astype(F32) * b).astype(BF16)
    o_ref[...] += jnp.dot(m, wo_ref[...], preferred_element_type=F32)


def _merge(x, ya, yb, gates, wa, wb, wo, *, tm, tn):
    t, d = x.shape
    nj = d // tn
    tok = pl.BlockSpec((tm, d), lambda i, j: (i, 0))
    return pl.pallas_call(
        _merge_body,
        grid=(t // tm, nj),
        in_specs=[
            tok, tok, tok,
            pl.BlockSpec((tm, tn), lambda i, j: (i, j)),
            pl.BlockSpec((tm, tn), lambda i, j: (i, j + nj)),
            pl.BlockSpec((d, tn), lambda i, j: (0, j)),
            pl.BlockSpec((d, tn), lambda i, j: (0, j)),
            pl.BlockSpec((tn, d), lambda i, j: (j, 0)),
        ],
        out_specs=tok,
        out_shape=jax.ShapeDtypeStruct((t, d), F32),
        compiler_params=_cparams(("parallel", "arbitrary")),
        name="merge",
    )(x, ya, yb, gates, gates, wa, wb, wo)


def _rot_cols(w):
    half = w.shape[-1] // 2
    return jnp.concatenate([-w[..., half:], w[..., :half]], axis=-1)


def _swap_halves(g):
    half = g.shape[-1] // 2
    return jnp.concatenate([g[..., half:], g[..., :half]], axis=-1)


def _pick_tile(n, pref):
    return pref if n % pref == 0 else n


def kernel(x, ffn1_norm_g, ffn1_w1, ffn1_w3, ffn1_w2, mix_norm_g, w_in, da_q_norm_g, da_k_norm_g, da_lambda_q1, da_lambda_k1, da_lambda_q2, da_lambda_k2, da_subln_g, mla_q_a_norm_g, mla_w_qb, mla_kv_a_norm_g, mla_w_kvb, mla_q_norm_g, mla_k_norm_g, w_branch_a, w_branch_b, w_out, ffn2_norm_g, ffn2_w1, ffn2_w3, ffn2_w2):
    bsz, seq, d = x.shape
    t = bsz * seq
    depth = ffn1_norm_g.shape[0]
    tm = _pick_tile(t, 512)
    tm_proj = _pick_tile(t, 1024)
    tm_mla = _pick_tile(seq, 256)
    tq = _pick_tile(seq, 512)
    tk = _pick_tile(seq, 1024)
    row = lambda v: v.reshape(1, -1).astype(F32)
    col_tile = lambda v, n: jnp.broadcast_to(v.astype(F32)[:, None], (v.shape[0], n))

    inv = ROPE_THETA ** (-jnp.arange(0, MLA_ROPE_DIM, 2, dtype=F32) / MLA_ROPE_DIM)
    ang = jnp.arange(seq, dtype=F32)[:, None] * inv[None, :]
    cos2 = jnp.tile(jnp.cos(ang), (1, 4))
    sin2 = jnp.tile(jnp.sin(ang), (1, 4))
    cos_t, sin_t = cos2.T, sin2.T
    slopes = (2.0 ** (-8.0 * jnp.arange(1, DA_HEADS + 1, dtype=F32) / DA_HEADS)) * LOG2E

    xt = x.reshape(t, d)
    for l in range(depth):
        lambda_init = 0.8 - 0.6 * math.exp(-0.3 * l)

        xt = _ffn(xt, row(ffn1_norm_g[l]), ffn1_w1[l].astype(BF16), ffn1_w3[l].astype(BF16),
                  (0.5 * ffn1_w2[l]).astype(BF16), tm=tm, tf=512)

        w_qkv, w_lat, w_gate = _repack_w_in(w_in[l], d)
        q_scale = DA_HEAD_DIM ** -0.5 * LOG2E
        q_da_t, k_da, v_da_t, lat, gates = _in_proj(
            xt, row(mix_norm_g[l]), w_qkv, w_lat, w_gate,
            col_tile(da_q_norm_g[l] * q_scale, tm_proj), row(da_k_norm_g[l]), tm=tm_proj)

        wq = mla_w_qb[l].reshape(MLA_Q_RANK, MLA_HEADS, MLA_QK_DIM)
        wq_nope = wq[:, :, :MLA_NOPE_DIM].reshape(MLA_Q_RANK, MLA_HEADS // 2, 2 * LANES)
        wq_rope = wq[:, :, MLA_NOPE_DIM:]
        wq_rot = _rot_cols(wq_rope).reshape(MLA_Q_RANK, MLA_HEADS // 2, LANES)
        wq_rope = wq_rope.reshape(MLA_Q_RANK, MLA_HEADS // 2, LANES)
        wq_packed = jnp.concatenate([wq_nope, wq_rope, wq_rot], axis=-1).reshape(
            MLA_Q_RANK, MLA_HEADS * 2 * LANES).T.astype(BF16)
        wkv = mla_w_kvb[l].reshape(MLA_KV_RANK, MLA_HEADS, MLA_NOPE_DIM + MLA_V_DIM)
        wkn = wkv[:, :, :MLA_NOPE_DIM].reshape(MLA_KV_RANK, MLA_HEADS * MLA_NOPE_DIM).astype(BF16)
        wv = wkv[:, :, MLA_NOPE_DIM:].reshape(MLA_KV_RANK, MLA_HEADS * MLA_V_DIM).T.astype(BF16)
        mla_scale = MLA_QK_DIM ** -0.5 * LOG2E
        gq, gk = mla_q_norm_g[l], mla_k_norm_g[l]
        q_gains_t = (
            col_tile(gq[:MLA_NOPE_DIM] * mla_scale, tm_mla),
            col_tile(jnp.tile(gq[MLA_NOPE_DIM:], 2) * mla_scale, tm_mla),
            col_tile(jnp.tile(_swap_halves(gq[MLA_NOPE_DIM:]), 2) * mla_scale, tm_mla),
        )
        k_gains = (
            row(gk[:MLA_NOPE_DIM]),
            row(jnp.tile(gk[MLA_NOPE_DIM:], 2)),
            row(jnp.tile(_swap_halves(gk[MLA_NOPE_DIM:]), 2)),
        )
        qn_t, qr_t, kn, kr, v_mla_t = _mla_proj(
            lat, row(mla_q_a_norm_g[l]), row(mla_kv_a_norm_g[l]), wq_packed, wkn, wv,
            q_gains_t, k_gains, cos2, sin2, cos_t, sin_t, tm=tm_mla, seq=seq)

        b3 = lambda a: a.reshape(bsz, seq, a.shape[-1])
        lams = (row(da_lambda_q1[l]), row(da_lambda_k1[l]), row(da_lambda_q2[l]), row(da_lambda_k2[l]))
        y_a = _da_wide_attn(slopes, lams, col_tile(da_subln_g[l], tk), q_da_t, b3(k_da), v_da_t,
                            tq=tq, tk=tk, lambda_init=lambda_init)
        y_b = _mla_wide_attn(qn_t, qr_t, b3(kn), b3(kr), v_mla_t, tq=tq, tk=tk)

        xt = _merge(xt, y_a.reshape(t, -1), y_b.reshape(t, -1), gates,
                    w_branch_a[l].astype(BF16), w_branch_b[l].astype(BF16), w_out[l].astype(BF16),
                    tm=tm, tn=512)

        xt = _ffn(xt, row(ffn2_norm_g[l]), ffn2_w1[l].astype(BF16), ffn2_w3[l].astype(BF16),
                  (0.5 * ffn2_w2[l]).astype(BF16), tm=tm, tf=512)
    return xt.reshape(bsz, seq, d)
```

```python
import functools
import math

import jax
import jax.numpy as jnp
from jax import lax
from jax.experimental import pallas as pl
from jax.experimental.pallas import tpu as pltpu

F32 = jnp.float32
BF16 = jnp.bfloat16

EPS = 1e-6
CHUNK = 64
CHUNK_SHIFT = CHUNK.bit_length() - 1
DA_HEADS = 8
DA_HEAD_DIM = 128
DA_V_DIM = 2 * DA_HEAD_DIM
MLA_HEADS = 16
MLA_Q_RANK = 768
MLA_KV_RANK = 512
MLA_NOPE_DIM = 128
MLA_ROPE_DIM = 64
MLA_V_DIM = 128
MLA_QK_DIM = MLA_NOPE_DIM + MLA_ROPE_DIM
ROPE_THETA = 10000.0
LOG2E = math.log2(math.e)
MASK_VALUE = -1e30

LANES = 128
SUBLANES = 8
MXU_WIDTH = 256
V7X_VMEM_BYTES = 64 * 1024 * 1024
VMEM_LIMIT = V7X_VMEM_BYTES - 8 * 1024 * 1024


def _cparams(sem):
    return pltpu.CompilerParams(dimension_semantics=sem, vmem_limit_bytes=VMEM_LIMIT)


def _rms(x, g):
    ms = jnp.mean(x * x, axis=-1, keepdims=True)
    return x * lax.rsqrt(ms + EPS) * g


def _tn_dot(w, h):
    return lax.dot_general(w, h, (((0,), (1,)), ((), ())), preferred_element_type=F32)


def _nt_dot(a, b):
    return lax.dot_general(a, b, (((1,), (1,)), ((), ())), preferred_element_type=F32)


def _rot_cols(w):
    half = w.shape[-1] // 2
    return jnp.concatenate([-w[..., half:], w[..., :half]], axis=-1)


def _swap_halves(g):
    half = g.shape[-1] // 2
    return jnp.concatenate([g[..., half:], g[..., :half]], axis=-1)


def _pick_tile(n, pref):
    return pref if n % pref == 0 else n


def _ffn_body(x_ref, g_ref, w1_ref, w3_ref, w2_ref, o_ref, h_ref):
    @pl.when(pl.program_id(1) == 0)
    def _():
        x = x_ref[...]
        h_ref[...] = _rms(x, g_ref[...]).astype(BF16)
        o_ref[...] = x

    h = h_ref[...]
    a = jnp.dot(h, w1_ref[...], preferred_element_type=F32)
    b = jnp.dot(h, w3_ref[...], preferred_element_type=F32)
    u = (a * jax.nn.sigmoid(a) * b).astype(BF16)
    o_ref[...] += jnp.dot(u, w2_ref[...], preferred_element_type=F32)


def _ffn(x, g, w1, w3, w2_half, *, tm, tf):
    t, d = x.shape
    f = w1.shape[1]
    return pl.pallas_call(
        _ffn_body,
        grid=(t // tm, f // tf),
        in_specs=[
            pl.BlockSpec((tm, d), lambda i, j: (i, 0)),
            pl.BlockSpec((1, d), lambda i, j: (0, 0)),
            pl.BlockSpec((d, tf), lambda i, j: (0, j)),
            pl.BlockSpec((d, tf), lambda i, j: (0, j)),
            pl.BlockSpec((tf, d), lambda i, j: (j, 0)),
        ],
        out_specs=pl.BlockSpec((tm, d), lambda i, j: (i, 0)),
        out_shape=jax.ShapeDtypeStruct((t, d), F32),
        scratch_shapes=[pltpu.VMEM((tm, d), BF16)],
        compiler_params=_cparams(("parallel", "arbitrary")),
        name="ffn",
    )(x, g, w1, w3, w2_half)


_W_IN_QKV = 2 * DA_HEADS * DA_HEAD_DIM * 2 + DA_HEADS * DA_V_DIM
_W_IN_KROPE = _W_IN_QKV + MLA_Q_RANK + MLA_KV_RANK
_W_IN_GATE = _W_IN_KROPE + MLA_ROPE_DIM
REPACK_ROWS = 128


def _repack_body(w_ref, qkv_ref, lat_ref, gate_ref):
    w = w_ref[...]
    qkv_ref[...] = w[:, :_W_IN_QKV].astype(BF16)
    k_rope = w[:, _W_IN_KROPE:_W_IN_GATE]
    k_rot = _rot_cols(k_rope)
    lat_ref[...] = jnp.concatenate(
        [w[:, _W_IN_QKV:_W_IN_KROPE], k_rope, k_rope, k_rot, k_rot], axis=1).astype(BF16)
    gate_ref[...] = w[:, _W_IN_GATE:].astype(BF16)


def _repack_w_in(w, d):
    r, n_in = w.shape
    tr = _pick_tile(r, REPACK_ROWS)
    n_lat = _W_IN_KROPE - _W_IN_QKV + 4 * MLA_ROPE_DIM
    out = lambda n: pl.BlockSpec((tr, n), lambda i: (i, 0))
    return pl.pallas_call(
        _repack_body,
        grid=(r // tr,),
        in_specs=[pl.BlockSpec((tr, n_in), lambda i: (i, 0))],
        out_specs=[out(_W_IN_QKV), out(n_lat), out(2 * d)],
        out_shape=[jax.ShapeDtypeStruct((r, _W_IN_QKV), BF16),
                   jax.ShapeDtypeStruct((r, n_lat), BF16),
                   jax.ShapeDtypeStruct((r, 2 * d), BF16)],
        compiler_params=_cparams(("parallel",)),
        name="repack_w_in",
    )(w)


PROJ_TN = 512
_Q0, _K0, _V0, _C0, _G0, _NJ = 0, 4, 8, 12, 15, 23


def _head_norm(acc, g):
    parts = []
    for c in range(acc.shape[1] // LANES):
        blk = acc[:, c * LANES:(c + 1) * LANES]
        ms = jnp.mean(blk * blk, axis=-1, keepdims=True)
        parts.append(blk * lax.rsqrt(ms + EPS) * g)
    return jnp.concatenate(parts, axis=-1)


def _head_norm_t(acc_t, g_t):
    parts = []
    for c in range(acc_t.shape[0] // LANES):
        blk = acc_t[c * LANES:(c + 1) * LANES, :]
        ms = jnp.mean(blk * blk, axis=0, keepdims=True)
        parts.append(blk * lax.rsqrt(ms + EPS) * g_t)
    return jnp.concatenate(parts, axis=0)


def _proj_body(x_ref, g_ref, wqkv_ref, wlat_ref, wgate_ref, qg_ref, kg_ref,
               qt_ref, k_ref, vt_ref, c_ref, gate_ref, h_ref):
    j = pl.program_id(1)

    @pl.when(j == 0)
    def _():
        h_ref[...] = _rms(x_ref[...], g_ref[...]).astype(BF16)

    @pl.when(j < _K0)
    def _():
        qt_ref[...] = _head_norm_t(_tn_dot(wqkv_ref[...], h_ref[...]), qg_ref[...]).astype(BF16)

    @pl.when((j >= _K0) & (j < _V0))
    def _():
        acc = jnp.dot(h_ref[...], wqkv_ref[...], preferred_element_type=F32)
        k_ref[...] = _head_norm(acc, kg_ref[...]).astype(BF16)

    @pl.when((j >= _V0) & (j < _C0))
    def _():
        vt_ref[...] = _tn_dot(wqkv_ref[...], h_ref[...]).astype(BF16)

    @pl.when((j >= _C0) & (j < _G0))
    def _():
        c_ref[...] = jnp.dot(h_ref[...], wlat_ref[...], preferred_element_type=F32)

    @pl.when(j >= _G0)
    def _():
        acc = jnp.dot(h_ref[...], wgate_ref[...], preferred_element_type=F32)
        gate_ref[...] = jax.nn.sigmoid(acc).astype(BF16)


def _in_proj(x, g, w_qkv, w_lat, w_gate, qg_t, kg, *, tm):
    t, d = x.shape
    tn = PROJ_TN
    n_q, n_k, n_v, n_c, n_g = _K0 - _Q0, _V0 - _K0, _C0 - _V0, _G0 - _C0, _NJ - _G0

    def w_cols(j0, n):
        return pl.BlockSpec((d, tn), lambda i, j: (0, jnp.clip(j - j0, 0, n - 1)))

    def cols(j0, n):
        return pl.BlockSpec((tm, tn), lambda i, j: (i, jnp.clip(j - j0, 0, n - 1)))

    def rows(j0, n):
        return pl.BlockSpec((tn, tm), lambda i, j: (jnp.clip(j - j0, 0, n - 1), i))

    return pl.pallas_call(
        _proj_body,
        grid=(t // tm, _NJ),
        in_specs=[
            pl.BlockSpec((tm, d), lambda i, j: (i, 0)),
            pl.BlockSpec((1, d), lambda i, j: (0, 0)),
            w_cols(_Q0, n_q + n_k + n_v), w_cols(_C0, n_c), w_cols(_G0, n_g),
            pl.BlockSpec((LANES, tm), lambda i, j: (0, 0)),
            pl.BlockSpec((1, LANES), lambda i, j: (0, 0)),
        ],
        out_specs=[rows(_Q0, n_q), cols(_K0, n_k), rows(_V0, n_v), cols(_C0, n_c), cols(_G0, n_g)],
        out_shape=[
            jax.ShapeDtypeStruct((n_q * tn, t), BF16),
            jax.ShapeDtypeStruct((t, n_k * tn), BF16),
            jax.ShapeDtypeStruct((n_v * tn, t), BF16),
            jax.ShapeDtypeStruct((t, n_c * tn), F32),
            jax.ShapeDtypeStruct((t, n_g * tn), BF16),
        ],
        scratch_shapes=[pltpu.VMEM((tm, d), BF16)],
        compiler_params=_cparams(("parallel", "arbitrary")),
        name="in_proj",
    )(x, g, w_qkv, w_lat, w_gate, qg_t, kg)


def _mla_proj_body(c_ref, qa_g_ref, kva_g_ref, wq_ref, wkn_ref, wv_ref,
                   gqn_ref, gqr_ref, gqt_ref, gk_nope_ref, gk_rope_ref, gk_rot_ref,
                   cos_ref, sin_ref, cost_ref, sint_ref,
                   qnt_ref, qrt_ref, kn_ref, kr_ref, vt_ref):
    c = c_ref[...]
    o_kv = MLA_Q_RANK
    o_kr = MLA_Q_RANK + MLA_KV_RANK
    cqn = _rms(c[:, :o_kv], qa_g_ref[...]).astype(BF16)
    ckvn = _rms(c[:, o_kv:o_kr], kva_g_ref[...]).astype(BF16)
    k_rope = c[:, o_kr:o_kr + LANES]
    k_rot = c[:, o_kr + LANES:o_kr + 2 * LANES]
    first = lax.broadcasted_iota(jnp.int32, (1, LANES), 1) < MLA_ROPE_DIM
    k_roped = k_rope * gk_rope_ref[...] * cos_ref[...] + k_rot * gk_rot_ref[...] * sin_ref[...]
    k_rope_sq = 0.5 * jnp.sum(k_rope * k_rope, axis=-1, keepdims=True)
    inv_d = 1.0 / MLA_QK_DIM
    cos_t = cost_ref[...]
    sin_t = sint_ref[...]
    half = MLA_ROPE_DIM

    for p in range(MLA_HEADS // 2):
        qraw = _nt_dot(wq_ref[p * 4 * LANES:(p + 1) * 4 * LANES, :], cqn)
        na, nb = qraw[:LANES], qraw[LANES:2 * LANES]
        rp, rt = qraw[2 * LANES:3 * LANES], qraw[3 * LANES:]
        rp2 = rp * rp
        sa = jnp.sum(na * na, axis=0, keepdims=True) + jnp.sum(rp2[:half], axis=0, keepdims=True)
        sb = jnp.sum(nb * nb, axis=0, keepdims=True) + jnp.sum(rp2[half:], axis=0, keepdims=True)
        ra = lax.rsqrt(sa * inv_d + EPS)
        rb = lax.rsqrt(sb * inv_d + EPS)
        ra_rows = slice((2 * p) * LANES, (2 * p + 1) * LANES)
        rb_rows = slice((2 * p + 1) * LANES, (2 * p + 2) * LANES)
        qnt_ref[ra_rows, :] = (na * ra * gqn_ref[...]).astype(BF16)
        qnt_ref[rb_rows, :] = (nb * rb * gqn_ref[...]).astype(BF16)
        roped = rp * gqr_ref[...] * cos_t + rt * gqt_ref[...] * sin_t
        qrt_ref[p * LANES:(p + 1) * LANES, :] = jnp.concatenate(
            [roped[:half] * ra, roped[half:] * rb], axis=0).astype(BF16)

        knraw = jnp.dot(ckvn, wkn_ref[:, p * 2 * LANES:(p + 1) * 2 * LANES], preferred_element_type=F32)
        kna, knb = knraw[:, :LANES], knraw[:, LANES:]
        rka = lax.rsqrt((jnp.sum(kna * kna, axis=-1, keepdims=True) + k_rope_sq) * inv_d + EPS)
        rkb = lax.rsqrt((jnp.sum(knb * knb, axis=-1, keepdims=True) + k_rope_sq) * inv_d + EPS)
        kn_ref[:, ra_rows] = (kna * rka * gk_nope_ref[...]).astype(BF16)
        kn_ref[:, rb_rows] = (knb * rkb * gk_nope_ref[...]).astype(BF16)
        kr_ref[:, ra_rows] = jnp.where(first, k_roped * rka, 0.0).astype(BF16)
        kr_ref[:, rb_rows] = jnp.where(first, 0.0, k_roped * rkb).astype(BF16)

        vt_ref[p * 2 * LANES:(p + 1) * 2 * LANES, :] = _nt_dot(
            wv_ref[p * 2 * LANES:(p + 1) * 2 * LANES, :], ckvn).astype(BF16)


def _mla_proj(c, qa_g, kva_g, wq, wkn, wv, q_gains_t, k_gains, cos2, sin2, cos_t, sin_t, *, tm, seq):
    t = c.shape[0]
    n_pos = seq // tm
    const = lambda a: pl.BlockSpec(a.shape, lambda i: (0, 0))
    hw = MLA_HEADS * LANES
    tok = lambda w: pl.BlockSpec((tm, w), lambda i: (i, 0))
    tok_t = lambda w: pl.BlockSpec((w, tm), lambda i: (0, i))
    return pl.pallas_call(
        _mla_proj_body,
        grid=(t // tm,),
        in_specs=[tok(c.shape[1]), const(qa_g), const(kva_g), const(wq), const(wkn), const(wv)]
        + [const(a) for a in q_gains_t] + [const(a) for a in k_gains] + [
            pl.BlockSpec((tm, LANES), lambda i: (i % n_pos, 0)),
            pl.BlockSpec((tm, LANES), lambda i: (i % n_pos, 0)),
            pl.BlockSpec((LANES, tm), lambda i: (0, i % n_pos)),
            pl.BlockSpec((LANES, tm), lambda i: (0, i % n_pos)),
        ],
        out_specs=[tok_t(hw), tok_t(hw // 2), tok(hw), tok(hw), tok_t(hw)],
        out_shape=[
            jax.ShapeDtypeStruct((hw, t), BF16),
            jax.ShapeDtypeStruct((hw // 2, t), BF16),
            jax.ShapeDtypeStruct((t, hw), BF16),
            jax.ShapeDtypeStruct((t, hw), BF16),
            jax.ShapeDtypeStruct((hw, t), BF16),
        ],
        compiler_params=_cparams(("parallel",)),
        name="mla_proj",
    )(c, qa_g, kva_g, wq, wkn, wv, *q_gains_t, *k_gains, cos2, sin2, cos_t, sin_t)


def _own_chunk_mask(key, qry, dq):
    return ((key - dq) >> CHUNK_SHIFT) <= (qry >> CHUNK_SHIFT)


def _score_pass(k_tile, q_t, plan, s_ref, tq):
    m8 = None
    for n, desc in enumerate(plan):
        rows = slice(n * MXU_WIDTH, (n + 1) * MXU_WIDTH)
        if desc[0] == "wide":
            _, bias_lo, bias_hi, off_row = desc
            u = jnp.dot(k_tile(n), q_t, preferred_element_type=F32)
            if bias_lo is not None or bias_hi is not None:
                lo, hi = u[:, :tq], u[:, tq:]
                lo = lo if bias_lo is None else lo + bias_lo
                hi = hi if bias_hi is None else hi + bias_hi
                u = jnp.concatenate([lo, hi], axis=1)
            s_ref[rows, :] = u
            t = jnp.max(u.reshape(MXU_WIDTH // SUBLANES, SUBLANES, 2 * tq), axis=0)
            if off_row is not None:
                t = t + off_row
            m8 = t if m8 is None else jnp.maximum(m8, t)
        else:
            u = jnp.dot(k_tile(n), q_t[:, tq:], preferred_element_type=F32) + desc[1]
            s_ref[rows, tq:] = u
            t = jnp.max(u.reshape(MXU_WIDTH // SUBLANES, SUBLANES, tq), axis=0)
            m8 = jnp.concatenate([m8[:, :tq], jnp.maximum(m8[:, tq:], t)], axis=1)
    return jnp.max(m8, axis=0, keepdims=True)


def _prob_pass(s_ref, p_ref, m, l, m_blk, plan, tq):
    m_new = jnp.maximum(m, m_blk)
    alpha = jnp.exp2(m - m_new)
    ps8 = None
    for n, desc in enumerate(plan):
        rows = slice(n * MXU_WIDTH, (n + 1) * MXU_WIDTH)
        if desc[0] == "wide":
            off_row = desc[3]
            ref = m_new if off_row is None else m_new - off_row
            p = jnp.exp2(s_ref[rows, :] - ref)
            t = jnp.sum(p.reshape(MXU_WIDTH // SUBLANES, SUBLANES, 2 * tq), axis=0)
            ps8 = t if ps8 is None else ps8 + t
            p_ref[rows, :] = p.astype(BF16)
        else:
            p = jnp.exp2(s_ref[rows, tq:] - m_new[:, tq:])
            t = jnp.sum(p.reshape(MXU_WIDTH // SUBLANES, SUBLANES, tq), axis=0)
            ps8 = jnp.concatenate([ps8[:, :tq], ps8[:, tq:] + t], axis=1)
            p_ref[rows, tq:] = p.astype(BF16)
    l_new = alpha * l + jnp.sum(ps8, axis=0, keepdims=True)
    return m_new, l_new, alpha


def _prob_times_v(acc_ref, alpha, vt_ref, v_rows, k0, p_ref, plan, tq):
    n_wide = sum(1 for d in plan if d[0] == "wide")
    n_hi = len(plan) - n_wide
    kw = n_wide * MXU_WIDTH
    acc_ref[...] = alpha * acc_ref[...] + jnp.dot(
        vt_ref[v_rows, pl.ds(k0, kw)], p_ref[:kw, :], preferred_element_type=F32)
    if n_hi:
        kh = n_hi * MXU_WIDTH
        acc_ref[:, tq:] += jnp.dot(vt_ref[v_rows, pl.ds(pl.multiple_of(k0 + kw, MXU_WIDTH), kh)],
                                   p_ref[kw:kw + kh, tq:], preferred_element_type=F32)


def _key_block_pipeline(n_full, score, prob_pv, finalize, width):
    neg = jnp.full((1, width), MASK_VALUE, F32)
    zero = jnp.zeros((1, width), F32)
    init = (neg, zero, neg, zero)

    @pl.when(n_full == 0)
    def _():
        finalize(prob_pv(0, True, init, score(0, True)))

    @pl.when(n_full > 0)
    def _():
        def body(j, carry):
            state, mx = carry
            state = prob_pv(j, False, state, mx)
            return state, score(j + 1, False)

        state, mx = lax.fori_loop(0, n_full - 1, body, (init, score(0, False)))
        state = prob_pv(n_full - 1, False, state, mx)
        mx_last = score(n_full, True)
        finalize(prob_pv(n_full, True, state, mx_last))


def _da_attn_body(slopes_ref, lq1_ref, lk1_ref, lq2_ref, lk2_ref, sg_ref,
                  qt_ref, k_ref, vt_ref, o_ref, acc_ref, s_ref, p_ref, own_ref,
                  *, tq, tk, lambda_init):
    h = pl.program_id(1)
    n_full = pl.program_id(2)
    slope = slopes_ref[h]
    per_q = tq // MXU_WIDTH
    q0 = n_full * tk
    key = lax.broadcasted_iota(jnp.int32, (MXU_WIDTH, tq), 0)
    key_bias = slope * key.astype(F32)
    lane = lax.broadcasted_iota(jnp.int32, (1, tk), 1)
    acc_ref[...] = jnp.zeros_like(acc_ref)

    @pl.when(n_full == 0)
    def _():
        qry = lax.broadcasted_iota(jnp.int32, (MXU_WIDTH, tq), 1)
        mirror_bias = slope * (2 * qry - key).astype(F32)
        for m in range(per_q):
            dq = -m * MXU_WIDTH
            bias = jnp.minimum(key_bias - slope * dq, mirror_bias + slope * dq)
            own_ref[m] = jnp.where(_own_chunk_mask(key, qry, dq), bias, MASK_VALUE)

    def plan(j, last):
        tiles = []
        for n in range(tk // MXU_WIDTH):
            if not last:
                off = slope * (j * tk + n * MXU_WIDTH - q0).astype(F32)
                tiles.append(("wide", key_bias, key_bias,
                              off - jnp.where(lane >= tq, slope * tq, 0.0)))
            elif n < per_q:
                off_hi = slope * float(n * MXU_WIDTH - tq)
                tiles.append(("wide", own_ref[n], key_bias, jnp.where(lane >= tq, off_hi, 0.0)))
            else:
                tiles.append(("hi", own_ref[n - per_q]))
        return tiles

    def score(j, last):
        k0 = pl.multiple_of(j * tk, tk)
        tiles = plan(j, last)

        def k_tile(lo):
            return lambda n: k_ref[0, pl.ds(pl.multiple_of(k0 + n * MXU_WIDTH, MXU_WIDTH), MXU_WIDTH),
                                   lo:lo + DA_HEAD_DIM]

        return (_score_pass(k_tile(0), qt_ref[:DA_HEAD_DIM, :], tiles, s_ref.at[0], tq),
                _score_pass(k_tile(DA_HEAD_DIM), qt_ref[DA_HEAD_DIM:, :], tiles, s_ref.at[1], tq))

    def prob_pv(j, last, state, mx):
        m1, l1, m2, l2 = state
        tiles = plan(j, last)
        k0 = pl.multiple_of(j * tk, tk)
        m1, l1, a1 = _prob_pass(s_ref.at[0], p_ref.at[0], m1, l1, mx[0], tiles, tq)
        m2, l2, a2 = _prob_pass(s_ref.at[1], p_ref.at[1], m2, l2, mx[1], tiles, tq)
        every = slice(None)
        _prob_times_v(acc_ref.at[0], a1, vt_ref, every, k0, p_ref.at[0], tiles, tq)
        _prob_times_v(acc_ref.at[1], a2, vt_ref, every, k0, p_ref.at[1], tiles, tq)
        return m1, l1, m2, l2

    def finalize(state):
        _, l1, _, l2 = state
        lam = (jnp.exp(jnp.sum(lq1_ref[...] * lk1_ref[...], axis=-1, keepdims=True))
               - jnp.exp(jnp.sum(lq2_ref[...] * lk2_ref[...], axis=-1, keepdims=True))
               + lambda_init)
        o = acc_ref[0] * (1.0 / l1) - lam * (acc_ref[1] * (1.0 / l2))
        ms = jnp.mean(o * o, axis=0, keepdims=True)
        y = o * lax.rsqrt(ms + EPS) * sg_ref[...] * (1.0 - lambda_init)
        o_ref[0] = y.T.astype(BF16)

    _key_block_pipeline(n_full, score, prob_pv, finalize, tk)


def _da_attn(slopes, lams, sg_t, q_t, k, v_t, *, tq, tk, lambda_init):
    assert tk == 2 * tq
    b, s, _ = k.shape
    w = DA_V_DIM
    nq = s // tk
    vec = pl.BlockSpec((1, DA_HEAD_DIM), lambda bi, h, i: (0, 0))
    return pl.pallas_call(
        functools.partial(_da_attn_body, tq=tq, tk=tk, lambda_init=lambda_init),
        grid=(b, DA_HEADS, nq),
        in_specs=[
            pl.BlockSpec(memory_space=pltpu.SMEM),
            vec, vec, vec, vec,
            pl.BlockSpec((w, tk), lambda bi, h, i: (0, 0)),
            pl.BlockSpec((w, tk), lambda bi, h, i: (h, bi * nq + i)),
            pl.BlockSpec((1, s, w), lambda bi, h, i: (bi, 0, h)),
            pl.BlockSpec((w, s), lambda bi, h, i: (h, bi)),
        ],
        out_specs=pl.BlockSpec((1, tk, w), lambda bi, h, i: (bi, i, h)),
        out_shape=jax.ShapeDtypeStruct((b, s, DA_HEADS * w), BF16),
        scratch_shapes=[pltpu.VMEM((2, w, tk), F32),
                        pltpu.VMEM((2, tk, tk), F32), pltpu.VMEM((2, tk, tk), BF16),
                        pltpu.VMEM((tq // MXU_WIDTH, MXU_WIDTH, tq), F32)],
        compiler_params=_cparams(("parallel", "parallel", "arbitrary")),
        name="da_attn",
    )(slopes, *lams, sg_t, q_t, k, v_t)


def _mla_attn_body(qnt_ref, qrt_ref, kn_ref, kr_ref, vt_ref, o_ref, acc_ref, s_ref, p_ref, own_ref,
                   *, tq, tk):
    n_full = pl.program_id(2)
    per_q = tq // MXU_WIDTH
    acc_ref[...] = jnp.zeros_like(acc_ref)

    @pl.when(n_full == 0)
    def _():
        key = lax.broadcasted_iota(jnp.int32, (MXU_WIDTH, tq), 0)
        qry = lax.broadcasted_iota(jnp.int32, (MXU_WIDTH, tq), 1)
        for m in range(per_q):
            own_ref[m] = jnp.where(_own_chunk_mask(key, qry, -m * MXU_WIDTH), 0.0, MASK_VALUE)

    qr_t = qrt_ref[...]
    qa_t = jnp.concatenate([qnt_ref[:LANES, :], qr_t], axis=0)
    qb_t = jnp.concatenate([qnt_ref[LANES:, :], qr_t], axis=0)

    def plan(last):
        tiles = []
        for n in range(tk // MXU_WIDTH):
            if not last:
                tiles.append(("wide", None, None, None))
            elif n < per_q:
                tiles.append(("wide", own_ref[n], None, None))
            else:
                tiles.append(("hi", own_ref[n - per_q]))
        return tiles

    def score(j, last):
        k0 = pl.multiple_of(j * tk, tk)
        tiles = plan(last)

        def k_tile(lo):
            def tile(n):
                rows = pl.ds(pl.multiple_of(k0 + n * MXU_WIDTH, MXU_WIDTH), MXU_WIDTH)
                return jnp.concatenate([kn_ref[0, rows, lo:lo + LANES],
                                        kr_ref[0, rows, lo:lo + LANES]], axis=-1)
            return tile

        return (_score_pass(k_tile(0), qa_t, tiles, s_ref.at[0], tq),
                _score_pass(k_tile(LANES), qb_t, tiles, s_ref.at[1], tq))

    def prob_pv(j, last, state, mx):
        ma, la, mb, lb = state
        tiles = plan(last)
        k0 = pl.multiple_of(j * tk, tk)
        ma, la, aa = _prob_pass(s_ref.at[0], p_ref.at[0], ma, la, mx[0], tiles, tq)
        mb, lb, ab = _prob_pass(s_ref.at[1], p_ref.at[1], mb, lb, mx[1], tiles, tq)
        _prob_times_v(acc_ref.at[0], aa, vt_ref, slice(0, LANES), k0, p_ref.at[0], tiles, tq)
        _prob_times_v(acc_ref.at[1], ab, vt_ref, slice(LANES, 2 * LANES), k0, p_ref.at[1], tiles, tq)
        return ma, la, mb, lb

    def finalize(state):
        _, la, _, lb = state
        o_t = jnp.concatenate([acc_ref[0] * (1.0 / la), acc_ref[1] * (1.0 / lb)], axis=0)
        o_ref[0] = o_t.T.astype(BF16)

    _key_block_pipeline(n_full, score, prob_pv, finalize, tk)


def _mla_attn(qn_t, qr_t, kn, kr, v_t, *, tq, tk):
    assert tk == 2 * tq
    b, s, _ = kn.shape
    w = 2 * LANES
    nq = s // tk
    kv_spec = pl.BlockSpec((1, s, w), lambda bi, h, i: (bi, 0, h))
    return pl.pallas_call(
        functools.partial(_mla_attn_body, tq=tq, tk=tk),
        grid=(b, MLA_HEADS // 2, nq),
        in_specs=[
            pl.BlockSpec((w, tk), lambda bi, h, i: (h, bi * nq + i)),
            pl.BlockSpec((LANES, tk), lambda bi, h, i: (h, bi * nq + i)),
            kv_spec, kv_spec,
            pl.BlockSpec((w, s), lambda bi, h, i: (h, bi)),
        ],
        out_specs=pl.BlockSpec((1, tk, w), lambda bi, h, i: (bi, i, h)),
        out_shape=jax.ShapeDtypeStruct((b, s, MLA_HEADS * LANES), BF16),
        scratch_shapes=[pltpu.VMEM((2, LANES, tk), F32),
                        pltpu.VMEM((2, tk, tk), F32), pltpu.VMEM((2, tk, tk), BF16),
                        pltpu.VMEM((tq // MXU_WIDTH, MXU_WIDTH, tq), F32)],
        compiler_params=_cparams(("parallel", "parallel", "arbitrary")),
        name="mla_attn",
    )(qn_t, qr_t, kn, kr, v_t)


def _merge_body(x_ref, ya_ref, yb_ref, gate_a_ref, gate_b_ref, wa_ref, wb_ref, wo_ref, o_ref):
    @pl.when(pl.program_id(1) == 0)
    def _():
        o_ref[...] = x_ref[...]

    a = jnp.dot(ya_ref[...], wa_ref[...], preferred_element_type=F32)
    b = jnp.dot(yb_ref[...], wb_ref[...], preferred_element_type=F32)
    m = (gate_a_ref[...].astype(F32) * a + gate_b_ref[...].astype(F32) * b).astype(BF16)
    o_ref[...] += jnp.dot(m, wo_ref[...], preferred_element_type=F32)


def _merge(x, ya, yb, gates, wa, wb, wo, *, tm, tn):
    t, d = x.shape
    nj = d // tn
    tok = pl.BlockSpec((tm, d), lambda i, j: (i, 0))
    return pl.pallas_call(
        _merge_body,
        grid=(t // tm, nj),
        in_specs=[
            tok, tok, tok,
            pl.BlockSpec((tm, tn), lambda i, j: (i, j)),
            pl.BlockSpec((tm, tn), lambda i, j: (i, j + nj)),
            pl.BlockSpec((d, tn), lambda i, j: (0, j)),
            pl.BlockSpec((d, tn), lambda i, j: (0, j)),
            pl.BlockSpec((tn, d), lambda i, j: (j, 0)),
        ],
        out_specs=tok,
        out_shape=jax.ShapeDtypeStruct((t, d), F32),
        compiler_params=_cparams(("parallel", "arbitrary")),
        name="merge",
    )(x, ya, yb, gates, gates, wa, wb, wo)


def kernel(x, ffn1_norm_g, ffn1_w1, ffn1_w3, ffn1_w2, mix_norm_g, w_in, da_q_norm_g, da_k_norm_g, da_lambda_q1, da_lambda_k1, da_lambda_q2, da_lambda_k2, da_subln_g, mla_q_a_norm_g, mla_w_qb, mla_kv_a_norm_g, mla_w_kvb, mla_q_norm_g, mla_k_norm_g, w_branch_a, w_branch_b, w_out, ffn2_norm_g, ffn2_w1, ffn2_w3, ffn2_w2):
    bsz, seq, d = x.shape
    t = bsz * seq
    depth = ffn1_norm_g.shape[0]
    tm = _pick_tile(t, 512)
    tm_proj = _pick_tile(t, 1024)
    tm_mla = _pick_tile(seq, 256)
    tq = _pick_tile(seq, 512)
    tk = 2 * tq
    row = lambda v: v.reshape(1, -1).astype(F32)
    col_tile = lambda v, n: jnp.broadcast_to(v.astype(F32)[:, None], (v.shape[0], n))

    inv = ROPE_THETA ** (-jnp.arange(0, MLA_ROPE_DIM, 2, dtype=F32) / MLA_ROPE_DIM)
    ang = jnp.arange(seq, dtype=F32)[:, None] * inv[None, :]
    cos2 = jnp.tile(jnp.cos(ang), (1, 4))
    sin2 = jnp.tile(jnp.sin(ang), (1, 4))
    cos_t, sin_t = cos2.T, sin2.T
    slopes = (2.0 ** (-8.0 * jnp.arange(1, DA_HEADS + 1, dtype=F32) / DA_HEADS)) * LOG2E

    xt = x.reshape(t, d)
    for l in range(depth):
        lambda_init = 0.8 - 0.6 * math.exp(-0.3 * l)

        xt = _ffn(xt, row(ffn1_norm_g[l]), ffn1_w1[l].astype(BF16), ffn1_w3[l].astype(BF16),
                  (0.5 * ffn1_w2[l]).astype(BF16), tm=tm, tf=512)

        w_qkv, w_lat, w_gate = _repack_w_in(w_in[l], d)
        q_scale = DA_HEAD_DIM ** -0.5 * LOG2E
        q_da_t, k_da, v_da_t, lat, gates = _in_proj(
            xt, row(mix_norm_g[l]), w_qkv, w_lat, w_gate,
            col_tile(da_q_norm_g[l] * q_scale, tm_proj), row(da_k_norm_g[l]), tm=tm_proj)

        wq = mla_w_qb[l].reshape(MLA_Q_RANK, MLA_HEADS, MLA_QK_DIM)
        wq_nope = wq[:, :, :MLA_NOPE_DIM].reshape(MLA_Q_RANK, MLA_HEADS // 2, 2 * LANES)
        wq_rope = wq[:, :, MLA_NOPE_DIM:]
        wq_rot = _rot_cols(wq_rope).reshape(MLA_Q_RANK, MLA_HEADS // 2, LANES)
        wq_rope = wq_rope.reshape(MLA_Q_RANK, MLA_HEADS // 2, LANES)
        wq_packed = jnp.concatenate([wq_nope, wq_rope, wq_rot], axis=-1).reshape(
            MLA_Q_RANK, MLA_HEADS * 2 * LANES).T.astype(BF16)
        wkv = mla_w_kvb[l].reshape(MLA_KV_RANK, MLA_HEADS, MLA_NOPE_DIM + MLA_V_DIM)
        wkn = wkv[:, :, :MLA_NOPE_DIM].reshape(MLA_KV_RANK, MLA_HEADS * MLA_NOPE_DIM).astype(BF16)
        wv = wkv[:, :, MLA_NOPE_DIM:].reshape(MLA_KV_RANK, MLA_HEADS * MLA_V_DIM).T.astype(BF16)
        mla_scale = MLA_QK_DIM ** -0.5 * LOG2E
        gq, gk = mla_q_norm_g[l], mla_k_norm_g[l]
        q_gains_t = (
            col_tile(gq[:MLA_NOPE_DIM] * mla_scale, tm_mla),
            col_tile(jnp.tile(gq[MLA_NOPE_DIM:], 2) * mla_scale, tm_mla),
            col_tile(jnp.tile(_swap_halves(gq[MLA_NOPE_DIM:]), 2) * mla_scale, tm_mla),
        )
        k_gains = (
            row(gk[:MLA_NOPE_DIM]),
            row(jnp.tile(gk[MLA_NOPE_DIM:], 2)),
            row(jnp.tile(_swap_halves(gk[MLA_NOPE_DIM:]), 2)),
        )
        qn_t, qr_t, kn, kr, v_mla_t = _mla_proj(
            lat, row(mla_q_a_norm_g[l]), row(mla_kv_a_norm_g[l]), wq_packed, wkn, wv,
            q_gains_t, k_gains, cos2, sin2, cos_t, sin_t, tm=tm_mla, seq=seq)

        b3 = lambda a: a.reshape(bsz, seq, a.shape[-1])
        lams = (row(da_lambda_q1[l]), row(da_lambda_k1[l]), row(da_lambda_q2[l]), row(da_lambda_k2[l]))
        y_a = _da_attn(slopes, lams, col_tile(da_subln_g[l], tk), q_da_t, b3(k_da), v_da_t,
                       tq=tq, tk=tk, lambda_init=lambda_init)
        y_b = _mla_attn(qn_t, qr_t, b3(kn), b3(kr), v_mla_t, tq=tq, tk=tk)

        xt = _merge(xt, y_a.reshape(t, -1), y_b.reshape(t, -1), gates,
                    w_branch_a[l].astype(BF16), w_branch_b[l].astype(BF16), w_out[l].astype(BF16),
                    tm=tm, tn=512)

        xt = _ffn(xt, row(ffn2_norm_g[l]), ffn2_w1[l].astype(BF16), ffn2_w3[l].astype(BF16),
                  (0.5 * ffn2_w2[l]).astype(BF16), tm=tm, tf=512)
    return xt.reshape(bsz, seq, d)
```

```python
import functools
import math

import jax
import jax.numpy as jnp
from jax import lax
from jax.experimental import pallas as pl
from jax.experimental.pallas import tpu as pltpu

F32 = jnp.float32
BF16 = jnp.bfloat16

EPS = 1e-6
CHUNK = 64
CHUNK_SHIFT = CHUNK.bit_length() - 1
DA_HEADS = 8
DA_HEAD_DIM = 128
DA_V_DIM = 2 * DA_HEAD_DIM
MLA_HEADS = 16
MLA_Q_RANK = 768
MLA_KV_RANK = 512
MLA_NOPE_DIM = 128
MLA_ROPE_DIM = 64
MLA_V_DIM = 128
MLA_QK_DIM = MLA_NOPE_DIM + MLA_ROPE_DIM
ROPE_THETA = 10000.0
LOG2E = math.log2(math.e)
MASK_VALUE = -1e30

LANES = 128
SUBLANES = 8
MXU_WIDTH = 256
V7X_VMEM_BYTES = 64 * 1024 * 1024
VMEM_LIMIT = V7X_VMEM_BYTES - 8 * 1024 * 1024


def _cparams(sem):
    return pltpu.CompilerParams(dimension_semantics=sem, vmem_limit_bytes=VMEM_LIMIT)


def _rms(x, g):
    ms = jnp.mean(x * x, axis=-1, keepdims=True)
    return x * lax.rsqrt(ms + EPS) * g


def _nt_dot(a, b):
    return lax.dot_general(a, b, (((1,), (1,)), ((), ())), preferred_element_type=F32)


def _rot_cols(w):
    half = w.shape[-1] // 2
    return jnp.concatenate([-w[..., half:], w[..., :half]], axis=-1)


def _swap_halves(g):
    half = g.shape[-1] // 2
    return jnp.concatenate([g[..., half:], g[..., :half]], axis=-1)


def _pick_tile(n, pref):
    return pref if n % pref == 0 else n


def _ffn_body(x_ref, g_ref, w1_ref, w3_ref, w2_ref, o_ref, h_ref):
    @pl.when(pl.program_id(1) == 0)
    def _():
        x = x_ref[...]
        h_ref[...] = _rms(x, g_ref[...]).astype(BF16)
        o_ref[...] = x

    h = h_ref[...]
    a = jnp.dot(h, w1_ref[...], preferred_element_type=F32)
    b = jnp.dot(h, w3_ref[...], preferred_element_type=F32)
    u = (a * jax.nn.sigmoid(a) * b).astype(BF16)
    o_ref[...] += jnp.dot(u, w2_ref[...], preferred_element_type=F32)


def _ffn(x, g, w1, w3, w2_half, *, tm, tf):
    t, d = x.shape
    f = w1.shape[1]
    return pl.pallas_call(
        _ffn_body,
        grid=(t // tm, f // tf),
        in_specs=[
            pl.BlockSpec((tm, d), lambda i, j: (i, 0)),
            pl.BlockSpec((1, d), lambda i, j: (0, 0)),
            pl.BlockSpec((d, tf), lambda i, j: (0, j)),
            pl.BlockSpec((d, tf), lambda i, j: (0, j)),
            pl.BlockSpec((tf, d), lambda i, j: (j, 0)),
        ],
        out_specs=pl.BlockSpec((tm, d), lambda i, j: (i, 0)),
        out_shape=jax.ShapeDtypeStruct((t, d), F32),
        scratch_shapes=[pltpu.VMEM((tm, d), BF16)],
        compiler_params=_cparams(("parallel", "arbitrary")),
        name="ffn",
    )(x, g, w1, w3, w2_half)


_W_IN_QKV = 2 * DA_HEADS * DA_HEAD_DIM * 2 + DA_HEADS * DA_V_DIM
_W_IN_KROPE = _W_IN_QKV + MLA_Q_RANK + MLA_KV_RANK
_W_IN_GATE = _W_IN_KROPE + MLA_ROPE_DIM


def _split_w_in(w):
    wt = w.T
    k_rope = wt[_W_IN_KROPE:_W_IN_GATE]
    half = MLA_ROPE_DIM // 2
    k_rot = jnp.concatenate([-k_rope[half:], k_rope[:half]], axis=0)
    w_lat = jnp.concatenate([wt[_W_IN_QKV:_W_IN_KROPE], k_rope, k_rope, k_rot, k_rot], axis=0)
    return wt[:_W_IN_QKV].astype(BF16), w_lat.astype(BF16), wt[_W_IN_GATE:].astype(BF16)


PROJ_TN = 512
_Q0, _K0, _V0, _C0, _G0, _NJ = 0, 4, 8, 12, 15, 23


def _head_norm(acc, g):
    parts = []
    for c in range(acc.shape[1] // LANES):
        blk = acc[:, c * LANES:(c + 1) * LANES]
        ms = jnp.mean(blk * blk, axis=-1, keepdims=True)
        parts.append(blk * lax.rsqrt(ms + EPS) * g)
    return jnp.concatenate(parts, axis=-1)


def _head_norm_t(acc_t, g_t):
    parts = []
    for c in range(acc_t.shape[0] // LANES):
        blk = acc_t[c * LANES:(c + 1) * LANES, :]
        ms = jnp.mean(blk * blk, axis=0, keepdims=True)
        parts.append(blk * lax.rsqrt(ms + EPS) * g_t)
    return jnp.concatenate(parts, axis=0)


def _proj_body(x_ref, g_ref, wqkv_ref, wlat_ref, wgate_ref, qg_ref, kg_ref,
               qt_ref, k_ref, vt_ref, c_ref, gate_ref, h_ref):
    j = pl.program_id(1)

    @pl.when(j == 0)
    def _():
        h_ref[...] = _rms(x_ref[...], g_ref[...]).astype(BF16)

    @pl.when(j < _K0)
    def _():
        qt_ref[...] = _head_norm_t(_nt_dot(wqkv_ref[...], h_ref[...]), qg_ref[...]).astype(BF16)

    @pl.when((j >= _K0) & (j < _V0))
    def _():
        k_ref[...] = _head_norm(_nt_dot(h_ref[...], wqkv_ref[...]), kg_ref[...]).astype(BF16)

    @pl.when((j >= _V0) & (j < _C0))
    def _():
        vt_ref[...] = _nt_dot(wqkv_ref[...], h_ref[...]).astype(BF16)

    @pl.when((j >= _C0) & (j < _G0))
    def _():
        c_ref[...] = _nt_dot(h_ref[...], wlat_ref[...])

    @pl.when(j >= _G0)
    def _():
        gate_ref[...] = jax.nn.sigmoid(_nt_dot(h_ref[...], wgate_ref[...])).astype(BF16)


def _in_proj(x, g, w_qkv, w_lat, w_gate, qg_t, kg, *, tm):
    t, d = x.shape
    tn = PROJ_TN
    n_q, n_k, n_v, n_c, n_g = _K0 - _Q0, _V0 - _K0, _C0 - _V0, _G0 - _C0, _NJ - _G0

    def w_cols(j0, n):
        return pl.BlockSpec((tn, d), lambda i, j: (jnp.clip(j - j0, 0, n - 1), 0))

    def cols(j0, n):
        return pl.BlockSpec((tm, tn), lambda i, j: (i, jnp.clip(j - j0, 0, n - 1)))

    def rows(j0, n):
        return pl.BlockSpec((tn, tm), lambda i, j: (jnp.clip(j - j0, 0, n - 1), i))

    return pl.pallas_call(
        _proj_body,
        grid=(t // tm, _NJ),
        in_specs=[
            pl.BlockSpec((tm, d), lambda i, j: (i, 0)),
            pl.BlockSpec((1, d), lambda i, j: (0, 0)),
            w_cols(_Q0, n_q + n_k + n_v), w_cols(_C0, n_c), w_cols(_G0, n_g),
            pl.BlockSpec((LANES, tm), lambda i, j: (0, 0)),
            pl.BlockSpec((1, LANES), lambda i, j: (0, 0)),
        ],
        out_specs=[rows(_Q0, n_q), cols(_K0, n_k), rows(_V0, n_v), cols(_C0, n_c), cols(_G0, n_g)],
        out_shape=[
            jax.ShapeDtypeStruct((n_q * tn, t), BF16),
            jax.ShapeDtypeStruct((t, n_k * tn), BF16),
            jax.ShapeDtypeStruct((n_v * tn, t), BF16),
            jax.ShapeDtypeStruct((t, n_c * tn), F32),
            jax.ShapeDtypeStruct((t, n_g * tn), BF16),
        ],
        scratch_shapes=[pltpu.VMEM((tm, d), BF16)],
        compiler_params=_cparams(("parallel", "arbitrary")),
        name="in_proj",
    )(x, g, w_qkv, w_lat, w_gate, qg_t, kg)


def _mla_proj_body(c_ref, qa_g_ref, kva_g_ref, wq_ref, wkn_ref, wv_ref,
                   gqn_ref, gqr_ref, gqt_ref, gk_nope_ref, gk_rope_ref, gk_rot_ref,
                   cos_ref, sin_ref, cost_ref, sint_ref,
                   qnt_ref, qrt_ref, kn_ref, kr_ref, vt_ref):
    c = c_ref[...]
    o_kv = MLA_Q_RANK
    o_kr = MLA_Q_RANK + MLA_KV_RANK
    cqn = _rms(c[:, :o_kv], qa_g_ref[...]).astype(BF16)
    ckvn = _rms(c[:, o_kv:o_kr], kva_g_ref[...]).astype(BF16)
    k_rope = c[:, o_kr:o_kr + LANES]
    k_rot = c[:, o_kr + LANES:o_kr + 2 * LANES]
    first = lax.broadcasted_iota(jnp.int32, (1, LANES), 1) < MLA_ROPE_DIM
    k_roped = k_rope * gk_rope_ref[...] * cos_ref[...] + k_rot * gk_rot_ref[...] * sin_ref[...]
    k_rope_sq = 0.5 * jnp.sum(k_rope * k_rope, axis=-1, keepdims=True)
    inv_d = 1.0 / MLA_QK_DIM
    cos_t = cost_ref[...]
    sin_t = sint_ref[...]
    half = MLA_ROPE_DIM

    for p in range(MLA_HEADS // 2):
        qraw = _nt_dot(wq_ref[p * 4 * LANES:(p + 1) * 4 * LANES, :], cqn)
        na, nb = qraw[:LANES], qraw[LANES:2 * LANES]
        rp, rt = qraw[2 * LANES:3 * LANES], qraw[3 * LANES:]
        rp2 = rp * rp
        sa = jnp.sum(na * na, axis=0, keepdims=True) + jnp.sum(rp2[:half], axis=0, keepdims=True)
        sb = jnp.sum(nb * nb, axis=0, keepdims=True) + jnp.sum(rp2[half:], axis=0, keepdims=True)
        ra = lax.rsqrt(sa * inv_d + EPS)
        rb = lax.rsqrt(sb * inv_d + EPS)
        ra_rows = slice((2 * p) * LANES, (2 * p + 1) * LANES)
        rb_rows = slice((2 * p + 1) * LANES, (2 * p + 2) * LANES)
        qnt_ref[ra_rows, :] = (na * ra * gqn_ref[...]).astype(BF16)
        qnt_ref[rb_rows, :] = (nb * rb * gqn_ref[...]).astype(BF16)
        roped = rp * gqr_ref[...] * cos_t + rt * gqt_ref[...] * sin_t
        qrt_ref[p * LANES:(p + 1) * LANES, :] = jnp.concatenate(
            [roped[:half] * ra, roped[half:] * rb], axis=0).astype(BF16)

        knraw = jnp.dot(ckvn, wkn_ref[:, p * 2 * LANES:(p + 1) * 2 * LANES], preferred_element_type=F32)
        kna, knb = knraw[:, :LANES], knraw[:, LANES:]
        rka = lax.rsqrt((jnp.sum(kna * kna, axis=-1, keepdims=True) + k_rope_sq) * inv_d + EPS)
        rkb = lax.rsqrt((jnp.sum(knb * knb, axis=-1, keepdims=True) + k_rope_sq) * inv_d + EPS)
        kn_ref[:, ra_rows] = (kna * rka * gk_nope_ref[...]).astype(BF16)
        kn_ref[:, rb_rows] = (knb * rkb * gk_nope_ref[...]).astype(BF16)
        kr_ref[:, ra_rows] = jnp.where(first, k_roped * rka, 0.0).astype(BF16)
        kr_ref[:, rb_rows] = jnp.where(first, 0.0, k_roped * rkb).astype(BF16)

        vt_ref[p * 2 * LANES:(p + 1) * 2 * LANES, :] = _nt_dot(
            wv_ref[p * 2 * LANES:(p + 1) * 2 * LANES, :], ckvn).astype(BF16)


def _mla_proj(c, qa_g, kva_g, wq, wkn, wv, q_gains_t, k_gains, cos2, sin2, cos_t, sin_t, *, tm, seq):
    t = c.shape[0]
    n_pos = seq // tm
    const = lambda a: pl.BlockSpec(a.shape, lambda i: (0, 0))
    hw = MLA_HEADS * LANES
    tok = lambda w: pl.BlockSpec((tm, w), lambda i: (i, 0))
    tok_t = lambda w: pl.BlockSpec((w, tm), lambda i: (0, i))
    return pl.pallas_call(
        _mla_proj_body,
        grid=(t // tm,),
        in_specs=[tok(c.shape[1]), const(qa_g), const(kva_g), const(wq), const(wkn), const(wv)]
        + [const(a) for a in q_gains_t] + [const(a) for a in k_gains] + [
            pl.BlockSpec((tm, LANES), lambda i: (i % n_pos, 0)),
            pl.BlockSpec((tm, LANES), lambda i: (i % n_pos, 0)),
            pl.BlockSpec((LANES, tm), lambda i: (0, i % n_pos)),
            pl.BlockSpec((LANES, tm), lambda i: (0, i % n_pos)),
        ],
        out_specs=[tok_t(hw), tok_t(hw // 2), tok(hw), tok(hw), tok_t(hw)],
        out_shape=[
            jax.ShapeDtypeStruct((hw, t), BF16),
            jax.ShapeDtypeStruct((hw // 2, t), BF16),
            jax.ShapeDtypeStruct((t, hw), BF16),
            jax.ShapeDtypeStruct((t, hw), BF16),
            jax.ShapeDtypeStruct((hw, t), BF16),
        ],
        compiler_params=_cparams(("parallel",)),
        name="mla_proj",
    )(c, qa_g, kva_g, wq, wkn, wv, *q_gains_t, *k_gains, cos2, sin2, cos_t, sin_t)


def _own_chunk_mask(key, qry, dq):
    return ((key - dq) >> CHUNK_SHIFT) <= (qry >> CHUNK_SHIFT)


def _score_pass(k_tile, q_t, plan, s_ref, tq):
    m8 = None
    for n, desc in enumerate(plan):
        rows = slice(n * MXU_WIDTH, (n + 1) * MXU_WIDTH)
        if desc[0] == "wide":
            _, bias_lo, bias_hi, off_row = desc
            u = jnp.dot(k_tile(n), q_t, preferred_element_type=F32)
            if bias_lo is not None or bias_hi is not None:
                lo, hi = u[:, :tq], u[:, tq:]
                lo = lo if bias_lo is None else lo + bias_lo
                hi = hi if bias_hi is None else hi + bias_hi
                u = jnp.concatenate([lo, hi], axis=1)
            s_ref[rows, :] = u
            t = jnp.max(u.reshape(MXU_WIDTH // SUBLANES, SUBLANES, 2 * tq), axis=0)
            if off_row is not None:
                t = t + off_row
            m8 = t if m8 is None else jnp.maximum(m8, t)
        else:
            u = jnp.dot(k_tile(n), q_t[:, tq:], preferred_element_type=F32) + desc[1]
            s_ref[rows, tq:] = u
            t = jnp.max(u.reshape(MXU_WIDTH // SUBLANES, SUBLANES, tq), axis=0)
            m8 = jnp.concatenate([m8[:, :tq], jnp.maximum(m8[:, tq:], t)], axis=1)
    return jnp.max(m8, axis=0, keepdims=True)


def _prob_pass(s_ref, p_ref, m, l, m_blk, plan, tq):
    m_new = jnp.maximum(m, m_blk)
    alpha = jnp.exp2(m - m_new)
    ps8 = None
    for n, desc in enumerate(plan):
        rows = slice(n * MXU_WIDTH, (n + 1) * MXU_WIDTH)
        if desc[0] == "wide":
            off_row = desc[3]
            ref = m_new if off_row is None else m_new - off_row
            p = jnp.exp2(s_ref[rows, :] - ref)
            t = jnp.sum(p.reshape(MXU_WIDTH // SUBLANES, SUBLANES, 2 * tq), axis=0)
            ps8 = t if ps8 is None else ps8 + t
            p_ref[rows, :] = p.astype(BF16)
        else:
            p = jnp.exp2(s_ref[rows, tq:] - m_new[:, tq:])
            t = jnp.sum(p.reshape(MXU_WIDTH // SUBLANES, SUBLANES, tq), axis=0)
            ps8 = jnp.concatenate([ps8[:, :tq], ps8[:, tq:] + t], axis=1)
            p_ref[rows, tq:] = p.astype(BF16)
    l_new = alpha * l + jnp.sum(ps8, axis=0, keepdims=True)
    return m_new, l_new, alpha


def _prob_times_v(acc_ref, alpha, vt_ref, v_rows, k0, p_ref, plan, tq):
    n_wide = sum(1 for d in plan if d[0] == "wide")
    n_hi = len(plan) - n_wide
    kw = n_wide * MXU_WIDTH
    acc_ref[...] = alpha * acc_ref[...] + jnp.dot(
        vt_ref[v_rows, pl.ds(k0, kw)], p_ref[:kw, :], preferred_element_type=F32)
    if n_hi:
        kh = n_hi * MXU_WIDTH
        acc_ref[:, tq:] += jnp.dot(vt_ref[v_rows, pl.ds(pl.multiple_of(k0 + kw, MXU_WIDTH), kh)],
                                   p_ref[kw:kw + kh, tq:], preferred_element_type=F32)


def _key_block_pipeline(n_full, score, prob_pv, finalize, width):
    neg = jnp.full((1, width), MASK_VALUE, F32)
    zero = jnp.zeros((1, width), F32)
    init = (neg, zero, neg, zero)

    @pl.when(n_full == 0)
    def _():
        finalize(prob_pv(0, True, init, score(0, True)))

    @pl.when(n_full > 0)
    def _():
        def body(j, carry):
            state, mx = carry
            state = prob_pv(j, False, state, mx)
            return state, score(j + 1, False)

        state, mx = lax.fori_loop(0, n_full - 1, body, (init, score(0, False)))
        state = prob_pv(n_full - 1, False, state, mx)
        mx_last = score(n_full, True)
        finalize(prob_pv(n_full, True, state, mx_last))


def _da_attn_body(slopes_ref, lq1_ref, lk1_ref, lq2_ref, lk2_ref, sg_ref,
                  qt_ref, k_ref, vt_ref, o_ref, acc_ref, s_ref, p_ref, own_ref,
                  *, tq, tk, lambda_init):
    h = pl.program_id(1)
    n_full = pl.program_id(2)
    slope = slopes_ref[h]
    per_q = tq // MXU_WIDTH
    q0 = n_full * tk
    key = lax.broadcasted_iota(jnp.int32, (MXU_WIDTH, tq), 0)
    key_bias = slope * key.astype(F32)
    lane = lax.broadcasted_iota(jnp.int32, (1, tk), 1)
    acc_ref[...] = jnp.zeros_like(acc_ref)

    @pl.when(n_full == 0)
    def _():
        qry = lax.broadcasted_iota(jnp.int32, (MXU_WIDTH, tq), 1)
        mirror_bias = slope * (2 * qry - key).astype(F32)
        for m in range(per_q):
            dq = -m * MXU_WIDTH
            bias = jnp.minimum(key_bias - slope * dq, mirror_bias + slope * dq)
            own_ref[m] = jnp.where(_own_chunk_mask(key, qry, dq), bias, MASK_VALUE)

    def plan(j, last):
        tiles = []
        for n in range(tk // MXU_WIDTH):
            if not last:
                off = slope * (j * tk + n * MXU_WIDTH - q0).astype(F32)
                tiles.append(("wide", key_bias, key_bias,
                              off - jnp.where(lane >= tq, slope * tq, 0.0)))
            elif n < per_q:
                off_hi = slope * float(n * MXU_WIDTH - tq)
                tiles.append(("wide", own_ref[n], key_bias, jnp.where(lane >= tq, off_hi, 0.0)))
            else:
                tiles.append(("hi", own_ref[n - per_q]))
        return tiles

    def score(j, last):
        k0 = pl.multiple_of(j * tk, tk)
        tiles = plan(j, last)

        def k_tile(lo):
            return lambda n: k_ref[0, pl.ds(pl.multiple_of(k0 + n * MXU_WIDTH, MXU_WIDTH), MXU_WIDTH),
                                   lo:lo + DA_HEAD_DIM]

        return (_score_pass(k_tile(0), qt_ref[:DA_HEAD_DIM, :], tiles, s_ref.at[0], tq),
                _score_pass(k_tile(DA_HEAD_DIM), qt_ref[DA_HEAD_DIM:, :], tiles, s_ref.at[1], tq))

    def prob_pv(j, last, state, mx):
        m1, l1, m2, l2 = state
        tiles = plan(j, last)
        k0 = pl.multiple_of(j * tk, tk)
        m1, l1, a1 = _prob_pass(s_ref.at[0], p_ref.at[0], m1, l1, mx[0], tiles, tq)
        m2, l2, a2 = _prob_pass(s_ref.at[1], p_ref.at[1], m2, l2, mx[1], tiles, tq)
        every = slice(None)
        _prob_times_v(acc_ref.at[0], a1, vt_ref, every, k0, p_ref.at[0], tiles, tq)
        _prob_times_v(acc_ref.at[1], a2, vt_ref, every, k0, p_ref.at[1], tiles, tq)
        return m1, l1, m2, l2

    def finalize(state):
        _, l1, _, l2 = state
        lam = (jnp.exp(jnp.sum(lq1_ref[...] * lk1_ref[...], axis=-1, keepdims=True))
               - jnp.exp(jnp.sum(lq2_ref[...] * lk2_ref[...], axis=-1, keepdims=True))
               + lambda_init)
        o = acc_ref[0] * (1.0 / l1) - lam * (acc_ref[1] * (1.0 / l2))
        ms = jnp.mean(o * o, axis=0, keepdims=True)
        y = o * lax.rsqrt(ms + EPS) * sg_ref[...] * (1.0 - lambda_init)
        o_ref[0] = y.T.astype(BF16)

    _key_block_pipeline(n_full, score, prob_pv, finalize, tk)


def _da_attn(slopes, lams, sg_t, q_t, k, v_t, *, tq, tk, lambda_init):
    assert tk == 2 * tq
    b, s, _ = k.shape
    w = DA_V_DIM
    nq = s // tk
    vec = pl.BlockSpec((1, DA_HEAD_DIM), lambda bi, h, i: (0, 0))
    return pl.pallas_call(
        functools.partial(_da_attn_body, tq=tq, tk=tk, lambda_init=lambda_init),
        grid=(b, DA_HEADS, nq),
        in_specs=[
            pl.BlockSpec(memory_space=pltpu.SMEM),
            vec, vec, vec, vec,
            pl.BlockSpec((w, tk), lambda bi, h, i: (0, 0)),
            pl.BlockSpec((w, tk), lambda bi, h, i: (h, bi * nq + i)),
            pl.BlockSpec((1, s, w), lambda bi, h, i: (bi, 0, h)),
            pl.BlockSpec((w, s), lambda bi, h, i: (h, bi)),
        ],
        out_specs=pl.BlockSpec((1, tk, w), lambda bi, h, i: (bi, i, h)),
        out_shape=jax.ShapeDtypeStruct((b, s, DA_HEADS * w), BF16),
        scratch_shapes=[pltpu.VMEM((2, w, tk), F32),
                        pltpu.VMEM((2, tk, tk), F32), pltpu.VMEM((2, tk, tk), BF16),
                        pltpu.VMEM((tq // MXU_WIDTH, MXU_WIDTH, tq), F32)],
        compiler_params=_cparams(("parallel", "parallel", "arbitrary")),
        name="da_attn",
    )(slopes, *lams, sg_t, q_t, k, v_t)


def _mla_attn_body(qnt_ref, qrt_ref, kn_ref, kr_ref, vt_ref, o_ref, acc_ref, s_ref, p_ref, own_ref,
                   *, tq, tk):
    n_full = pl.program_id(2)
    per_q = tq // MXU_WIDTH
    acc_ref[...] = jnp.zeros_like(acc_ref)

    @pl.when(n_full == 0)
    def _():
        key = lax.broadcasted_iota(jnp.int32, (MXU_WIDTH, tq), 0)
        qry = lax.broadcasted_iota(jnp.int32, (MXU_WIDTH, tq), 1)
        for m in range(per_q):
            own_ref[m] = jnp.where(_own_chunk_mask(key, qry, -m * MXU_WIDTH), 0.0, MASK_VALUE)

    qr_t = qrt_ref[...]
    qa_t = jnp.concatenate([qnt_ref[:LANES, :], qr_t], axis=0)
    qb_t = jnp.concatenate([qnt_ref[LANES:, :], qr_t], axis=0)

    def plan(last):
        tiles = []
        for n in range(tk // MXU_WIDTH):
            if not last:
                tiles.append(("wide", None, None, None))
            elif n < per_q:
                tiles.append(("wide", own_ref[n], None, None))
            else:
                tiles.append(("hi", own_ref[n - per_q]))
        return tiles

    def score(j, last):
        k0 = pl.multiple_of(j * tk, tk)
        tiles = plan(last)

        def k_tile(lo):
            def tile(n):
                rows = pl.ds(pl.multiple_of(k0 + n * MXU_WIDTH, MXU_WIDTH), MXU_WIDTH)
                return jnp.concatenate([kn_ref[0, rows, lo:lo + LANES],
                                        kr_ref[0, rows, lo:lo + LANES]], axis=-1)
            return tile

        return (_score_pass(k_tile(0), qa_t, tiles, s_ref.at[0], tq),
                _score_pass(k_tile(LANES), qb_t, tiles, s_ref.at[1], tq))

    def prob_pv(j, last, state, mx):
        ma, la, mb, lb = state
        tiles = plan(last)
        k0 = pl.multiple_of(j * tk, tk)
        ma, la, aa = _prob_pass(s_ref.at[0], p_ref.at[0], ma, la, mx[0], tiles, tq)
        mb, lb, ab = _prob_pass(s_ref.at[1], p_ref.at[1], mb, lb, mx[1], tiles, tq)
        _prob_times_v(acc_ref.at[0], aa, vt_ref, slice(0, LANES), k0, p_ref.at[0], tiles, tq)
        _prob_times_v(acc_ref.at[1], ab, vt_ref, slice(LANES, 2 * LANES), k0, p_ref.at[1], tiles, tq)
        return ma, la, mb, lb

    def finalize(state):
        _, la, _, lb = state
        o_t = jnp.concatenate([acc_ref[0] * (1.0 / la), acc_ref[1] * (1.0 / lb)], axis=0)
        o_ref[0] = o_t.T.astype(BF16)

    _key_block_pipeline(n_full, score, prob_pv, finalize, tk)


def _mla_attn(qn_t, qr_t, kn, kr, v_t, *, tq, tk):
    assert tk == 2 * tq
    b, s, _ = kn.shape
    w = 2 * LANES
    nq = s // tk
    kv_spec = pl.BlockSpec((1, s, w), lambda bi, h, i: (bi, 0, h))
    return pl.pallas_call(
        functools.partial(_mla_attn_body, tq=tq, tk=tk),
        grid=(b, MLA_HEADS // 2, nq),
        in_specs=[
            pl.BlockSpec((w, tk), lambda bi, h, i: (h, bi * nq + i)),
            pl.BlockSpec((LANES, tk), lambda bi, h, i: (h, bi * nq + i)),
            kv_spec, kv_spec,
            pl.BlockSpec((w, s), lambda bi, h, i: (h, bi)),
        ],
        out_specs=pl.BlockSpec((1, tk, w), lambda bi, h, i: (bi, i, h)),
        out_shape=jax.ShapeDtypeStruct((b, s, MLA_HEADS * LANES), BF16),
        scratch_shapes=[pltpu.VMEM((2, LANES, tk), F32),
                        pltpu.VMEM((2, tk, tk), F32), pltpu.VMEM((2, tk, tk), BF16),
                        pltpu.VMEM((tq // MXU_WIDTH, MXU_WIDTH, tq), F32)],
        compiler_params=_cparams(("parallel", "parallel", "arbitrary")),
        name="mla_attn",
    )(qn_t, qr_t, kn, kr, v_t)


def _merge_body(x_ref, ya_ref, yb_ref, gate_a_ref, gate_b_ref, wa_ref, wb_ref, wo_ref, o_ref):
    @pl.when(pl.program_id(1) == 0)
    def _():
        o_ref[...] = x_ref[...]

    a = jnp.dot(ya_ref[...], wa_ref[...], preferred_element_type=F32)
    b = jnp.dot(yb_ref[...], wb_ref[...], preferred_element_type=F32)
    m = (gate_a_ref[...].astype(F32) * a + gate_b_ref[...].astype(F32) * b).astype(BF16)
    o_ref[...] += jnp.dot(m, wo_ref[...], preferred_element_type=F32)


def _merge(x, ya, yb, gates, wa, wb, wo, *, tm, tn):
    t, d = x.shape
    nj = d // tn
    tok = pl.BlockSpec((tm, d), lambda i, j: (i, 0))
    return pl.pallas_call(
        _merge_body,
        grid=(t // tm, nj),
        in_specs=[
            tok, tok, tok,
            pl.BlockSpec((tm, tn), lambda i, j: (i, j)),
            pl.BlockSpec((tm, tn), lambda i, j: (i, j + nj)),
            pl.BlockSpec((d, tn), lambda i, j: (0, j)),
            pl.BlockSpec((d, tn), lambda i, j: (0, j)),
            pl.BlockSpec((tn, d), lambda i, j: (j, 0)),
        ],
        out_specs=tok,
        out_shape=jax.ShapeDtypeStruct((t, d), F32),
        compiler_params=_cparams(("parallel", "arbitrary")),
        name="merge",
    )(x, ya, yb, gates, gates, wa, wb, wo)


def kernel(x, ffn1_norm_g, ffn1_w1, ffn1_w3, ffn1_w2, mix_norm_g, w_in, da_q_norm_g, da_k_norm_g, da_lambda_q1, da_lambda_k1, da_lambda_q2, da_lambda_k2, da_subln_g, mla_q_a_norm_g, mla_w_qb, mla_kv_a_norm_g, mla_w_kvb, mla_q_norm_g, mla_k_norm_g, w_branch_a, w_branch_b, w_out, ffn2_norm_g, ffn2_w1, ffn2_w3, ffn2_w2):
    bsz, seq, d = x.shape
    t = bsz * seq
    depth = ffn1_norm_g.shape[0]
    tm = _pick_tile(t, 512)
    tm_proj = _pick_tile(t, 1024)
    tm_mla = _pick_tile(seq, 256)
    tq = _pick_tile(seq, 512)
    tk = 2 * tq
    row = lambda v: v.reshape(1, -1).astype(F32)
    col_tile = lambda v, n: jnp.broadcast_to(v.astype(F32)[:, None], (v.shape[0], n))

    inv = ROPE_THETA ** (-jnp.arange(0, MLA_ROPE_DIM, 2, dtype=F32) / MLA_ROPE_DIM)
    ang = jnp.arange(seq, dtype=F32)[:, None] * inv[None, :]
    cos2 = jnp.tile(jnp.cos(ang), (1, 4))
    sin2 = jnp.tile(jnp.sin(ang), (1, 4))
    cos_t, sin_t = cos2.T, sin2.T
    slopes = (2.0 ** (-8.0 * jnp.arange(1, DA_HEADS + 1, dtype=F32) / DA_HEADS)) * LOG2E

    xt = x.reshape(t, d)
    for l in range(depth):
        lambda_init = 0.8 - 0.6 * math.exp(-0.3 * l)

        xt = _ffn(xt, row(ffn1_norm_g[l]), ffn1_w1[l].astype(BF16), ffn1_w3[l].astype(BF16),
                  (0.5 * ffn1_w2[l]).astype(BF16), tm=tm, tf=512)

        w_qkv, w_lat, w_gate = _split_w_in(w_in[l])
        q_scale = DA_HEAD_DIM ** -0.5 * LOG2E
        q_da_t, k_da, v_da_t, lat, gates = _in_proj(
            xt, row(mix_norm_g[l]), w_qkv, w_lat, w_gate,
            col_tile(da_q_norm_g[l] * q_scale, tm_proj), row(da_k_norm_g[l]), tm=tm_proj)

        wq = mla_w_qb[l].reshape(MLA_Q_RANK, MLA_HEADS, MLA_QK_DIM)
        wq_nope = wq[:, :, :MLA_NOPE_DIM].reshape(MLA_Q_RANK, MLA_HEADS // 2, 2 * LANES)
        wq_rope = wq[:, :, MLA_NOPE_DIM:]
        wq_rot = _rot_cols(wq_rope).reshape(MLA_Q_RANK, MLA_HEADS // 2, LANES)
        wq_rope = wq_rope.reshape(MLA_Q_RANK, MLA_HEADS // 2, LANES)
        wq_packed = jnp.concatenate([wq_nope, wq_rope, wq_rot], axis=-1).reshape(
            MLA_Q_RANK, MLA_HEADS * 2 * LANES).T.astype(BF16)
        wkv = mla_w_kvb[l].reshape(MLA_KV_RANK, MLA_HEADS, MLA_NOPE_DIM + MLA_V_DIM)
        wkn = wkv[:, :, :MLA_NOPE_DIM].reshape(MLA_KV_RANK, MLA_HEADS * MLA_NOPE_DIM).astype(BF16)
        wv = wkv[:, :, MLA_NOPE_DIM:].reshape(MLA_KV_RANK, MLA_HEADS * MLA_V_DIM).T.astype(BF16)
        mla_scale = MLA_QK_DIM ** -0.5 * LOG2E
        gq, gk = mla_q_norm_g[l], mla_k_norm_g[l]
        q_gains_t = (
            col_tile(gq[:MLA_NOPE_DIM] * mla_scale, tm_mla),
            col_tile(jnp.tile(gq[MLA_NOPE_DIM:], 2) * mla_scale, tm_mla),
            col_tile(jnp.tile(_swap_halves(gq[MLA_NOPE_DIM:]), 2) * mla_scale, tm_mla),
        )
        k_gains = (
            row(gk[:MLA_NOPE_DIM]),
            row(jnp.tile(gk[MLA_NOPE_DIM:], 2)),
            row(jnp.tile(_swap_halves(gk[MLA_NOPE_DIM:]), 2)),
        )
        qn_t, qr_t, kn, kr, v_mla_t = _mla_proj(
            lat, row(mla_q_a_norm_g[l]), row(mla_kv_a_norm_g[l]), wq_packed, wkn, wv,
            q_gains_t, k_gains, cos2, sin2, cos_t, sin_t, tm=tm_mla, seq=seq)

        b3 = lambda a: a.reshape(bsz, seq, a.shape[-1])
        lams = (row(da_lambda_q1[l]), row(da_lambda_k1[l]), row(da_lambda_q2[l]), row(da_lambda_k2[l]))
        y_a = _da_attn(slopes, lams, col_tile(da_subln_g[l], tk), q_da_t, b3(k_da), v_da_t,
                       tq=tq, tk=tk, lambda_init=lambda_init)
        y_b = _mla_attn(qn_t, qr_t, b3(kn), b3(kr), v_mla_t, tq=tq, tk=tk)

        xt = _merge(xt, y_a.reshape(t, -1), y_b.reshape(t, -1), gates,
                    w_branch_a[l].astype(BF16), w_branch_b[l].astype(BF16), w_out[l].astype(BF16),
                    tm=tm, tn=512)

        xt = _ffn(xt, row(ffn2_norm_g[l]), ffn2_w1[l].astype(BF16), ffn2_w3[l].astype(BF16),
                  (0.5 * ffn2_w2[l]).astype(BF16), tm=tm, tf=512)
    return xt.reshape(bsz, seq, d)
```

```python
import functools
import math

import jax
import jax.numpy as jnp
from jax import lax
from jax.experimental import pallas as pl
from jax.experimental.pallas import tpu as pltpu

F32 = jnp.float32
BF16 = jnp.bfloat16

EPS = 1e-6
CHUNK = 64
CHUNK_SHIFT = CHUNK.bit_length() - 1
DA_HEADS = 8
DA_HEAD_DIM = 128
DA_V_DIM = 2 * DA_HEAD_DIM
MLA_HEADS = 16
MLA_Q_RANK = 768
MLA_KV_RANK = 512
MLA_NOPE_DIM = 128
MLA_ROPE_DIM = 64
MLA_V_DIM = 128
MLA_QK_DIM = MLA_NOPE_DIM + MLA_ROPE_DIM
ROPE_THETA = 10000.0
LOG2E = math.log2(math.e)
MASK_VALUE = -1e30

LANES = 128
SUBLANES = 8
MXU_WIDTH = 256
V7X_VMEM_BYTES = 64 * 1024 * 1024
VMEM_LIMIT = V7X_VMEM_BYTES - 8 * 1024 * 1024


def _cparams(sem):
    return pltpu.CompilerParams(dimension_semantics=sem, vmem_limit_bytes=VMEM_LIMIT)


def _rms(x, g):
    ms = jnp.mean(x * x, axis=-1, keepdims=True)
    return x * lax.rsqrt(ms + EPS) * g


def _nt_dot(a, b):
    return lax.dot_general(a, b, (((1,), (1,)), ((), ())), preferred_element_type=F32)


def _rot_cols(w):
    half = w.shape[-1] // 2
    return jnp.concatenate([-w[..., half:], w[..., :half]], axis=-1)


def _swap_halves(g):
    half = g.shape[-1] // 2
    return jnp.concatenate([g[..., half:], g[..., :half]], axis=-1)


def _pick_tile(n, pref):
    return pref if n % pref == 0 else n


def _ffn_body(x_ref, g_ref, w1_ref, w3_ref, w2_ref, o_ref, h_ref):
    @pl.when(pl.program_id(1) == 0)
    def _():
        x = x_ref[...]
        h_ref[...] = _rms(x, g_ref[...]).astype(BF16)
        o_ref[...] = x

    h = h_ref[...]
    a = jnp.dot(h, w1_ref[...], preferred_element_type=F32)
    b = jnp.dot(h, w3_ref[...], preferred_element_type=F32)
    u = (a * jax.nn.sigmoid(a) * b).astype(BF16)
    o_ref[...] += jnp.dot(u, w2_ref[...], preferred_element_type=F32)


def _ffn(x, g, w1, w3, w2_half, *, tm, tf):
    t, d = x.shape
    f = w1.shape[1]
    return pl.pallas_call(
        _ffn_body,
        grid=(t // tm, f // tf),
        in_specs=[
            pl.BlockSpec((tm, d), lambda i, j: (i, 0)),
            pl.BlockSpec((1, d), lambda i, j: (0, 0)),
            pl.BlockSpec((d, tf), lambda i, j: (0, j)),
            pl.BlockSpec((d, tf), lambda i, j: (0, j)),
            pl.BlockSpec((tf, d), lambda i, j: (j, 0)),
        ],
        out_specs=pl.BlockSpec((tm, d), lambda i, j: (i, 0)),
        out_shape=jax.ShapeDtypeStruct((t, d), F32),
        scratch_shapes=[pltpu.VMEM((tm, d), BF16)],
        compiler_params=_cparams(("parallel", "arbitrary")),
        name="ffn",
    )(x, g, w1, w3, w2_half)


_W_IN_QKV = 2 * DA_HEADS * DA_HEAD_DIM * 2 + DA_HEADS * DA_V_DIM
_W_IN_KROPE = _W_IN_QKV + MLA_Q_RANK + MLA_KV_RANK
_W_IN_GATE = _W_IN_KROPE + MLA_ROPE_DIM


def _split_w_in(w):
    wt = w.T
    k_rope = wt[_W_IN_KROPE:_W_IN_GATE]
    half = MLA_ROPE_DIM // 2
    k_rot = jnp.concatenate([-k_rope[half:], k_rope[:half]], axis=0)
    w_lat = jnp.concatenate([wt[_W_IN_QKV:_W_IN_KROPE], k_rope, k_rope, k_rot, k_rot], axis=0)
    return wt[:_W_IN_QKV].astype(BF16), w_lat.astype(BF16), wt[_W_IN_GATE:].astype(BF16)


PROJ_TN = 512
_Q0, _K0, _V0, _C0, _G0, _NJ = 0, 4, 8, 12, 15, 23


def _head_norm(acc, g):
    parts = []
    for c in range(acc.shape[1] // LANES):
        blk = acc[:, c * LANES:(c + 1) * LANES]
        ms = jnp.mean(blk * blk, axis=-1, keepdims=True)
        parts.append(blk * lax.rsqrt(ms + EPS) * g)
    return jnp.concatenate(parts, axis=-1)


def _head_norm_t(acc_t, g_t):
    parts = []
    for c in range(acc_t.shape[0] // LANES):
        blk = acc_t[c * LANES:(c + 1) * LANES, :]
        ms = jnp.mean(blk * blk, axis=0, keepdims=True)
        parts.append(blk * lax.rsqrt(ms + EPS) * g_t)
    return jnp.concatenate(parts, axis=0)


def _proj_body(x_ref, g_ref, wqkv_ref, wlat_ref, wgate_ref, qg_ref, kg_ref,
               qt_ref, k_ref, vt_ref, c_ref, gate_ref, h_ref):
    j = pl.program_id(1)

    @pl.when(j == 0)
    def _():
        h_ref[...] = _rms(x_ref[...], g_ref[...]).astype(BF16)

    @pl.when(j < _K0)
    def _():
        qt_ref[...] = _head_norm_t(_nt_dot(wqkv_ref[...], h_ref[...]), qg_ref[...]).astype(BF16)

    @pl.when((j >= _K0) & (j < _V0))
    def _():
        k_ref[...] = _head_norm(_nt_dot(h_ref[...], wqkv_ref[...]), kg_ref[...]).astype(BF16)

    @pl.when((j >= _V0) & (j < _C0))
    def _():
        vt_ref[...] = _nt_dot(wqkv_ref[...], h_ref[...]).astype(BF16)

    @pl.when((j >= _C0) & (j < _G0))
    def _():
        c_ref[...] = _nt_dot(h_ref[...], wlat_ref[...])

    @pl.when(j >= _G0)
    def _():
        z = _nt_dot(h_ref[...], wgate_ref[...])
        gate_ref[...] = (0.5 * jnp.tanh(0.5 * z) + 0.5).astype(BF16)


def _in_proj(x, g, w_qkv, w_lat, w_gate, qg_t, kg, *, tm):
    t, d = x.shape
    tn = PROJ_TN
    n_q, n_k, n_v, n_c, n_g = _K0 - _Q0, _V0 - _K0, _C0 - _V0, _G0 - _C0, _NJ - _G0

    def w_cols(j0, n):
        return pl.BlockSpec((tn, d), lambda i, j: (jnp.clip(j - j0, 0, n - 1), 0))

    def cols(j0, n):
        return pl.BlockSpec((tm, tn), lambda i, j: (i, jnp.clip(j - j0, 0, n - 1)))

    def rows(j0, n):
        return pl.BlockSpec((tn, tm), lambda i, j: (jnp.clip(j - j0, 0, n - 1), i))

    return pl.pallas_call(
        _proj_body,
        grid=(t // tm, _NJ),
        in_specs=[
            pl.BlockSpec((tm, d), lambda i, j: (i, 0)),
            pl.BlockSpec((1, d), lambda i, j: (0, 0)),
            w_cols(_Q0, n_q + n_k + n_v), w_cols(_C0, n_c), w_cols(_G0, n_g),
            pl.BlockSpec((LANES, tm), lambda i, j: (0, 0)),
            pl.BlockSpec((1, LANES), lambda i, j: (0, 0)),
        ],
        out_specs=[rows(_Q0, n_q), cols(_K0, n_k), rows(_V0, n_v), cols(_C0, n_c), cols(_G0, n_g)],
        out_shape=[
            jax.ShapeDtypeStruct((n_q * tn, t), BF16),
            jax.ShapeDtypeStruct((t, n_k * tn), BF16),
            jax.ShapeDtypeStruct((n_v * tn, t), BF16),
            jax.ShapeDtypeStruct((t, n_c * tn), F32),
            jax.ShapeDtypeStruct((t, n_g * tn), BF16),
        ],
        scratch_shapes=[pltpu.VMEM((tm, d), BF16)],
        compiler_params=_cparams(("parallel", "arbitrary")),
        name="in_proj",
    )(x, g, w_qkv, w_lat, w_gate, qg_t, kg)


def _mla_proj_body(c_ref, qa_g_ref, kva_g_ref, wq_ref, wkn_ref, wv_ref,
                   gqn_ref, gqr_ref, gqt_ref, gk_nope_ref, gk_rope_ref, gk_rot_ref,
                   cos_ref, sin_ref, cost_ref, sint_ref,
                   qnt_ref, qrt_ref, kn_ref, kr_ref, vt_ref):
    c = c_ref[...]
    o_kv = MLA_Q_RANK
    o_kr = MLA_Q_RANK + MLA_KV_RANK
    cqn = _rms(c[:, :o_kv], qa_g_ref[...]).astype(BF16)
    ckvn = _rms(c[:, o_kv:o_kr], kva_g_ref[...]).astype(BF16)
    k_rope = c[:, o_kr:o_kr + LANES]
    k_rot = c[:, o_kr + LANES:o_kr + 2 * LANES]
    first = lax.broadcasted_iota(jnp.int32, (1, LANES), 1) < MLA_ROPE_DIM
    k_roped = k_rope * gk_rope_ref[...] * cos_ref[...] + k_rot * gk_rot_ref[...] * sin_ref[...]
    k_rope_sq = 0.5 * jnp.sum(k_rope * k_rope, axis=-1, keepdims=True)
    inv_d = 1.0 / MLA_QK_DIM
    cos_t = cost_ref[...]
    sin_t = sint_ref[...]
    half = MLA_ROPE_DIM

    for p in range(MLA_HEADS // 2):
        qraw = _nt_dot(wq_ref[p * 4 * LANES:(p + 1) * 4 * LANES, :], cqn)
        na, nb = qraw[:LANES], qraw[LANES:2 * LANES]
        rp, rt = qraw[2 * LANES:3 * LANES], qraw[3 * LANES:]
        rp2 = rp * rp
        sa = jnp.sum(na * na, axis=0, keepdims=True) + jnp.sum(rp2[:half], axis=0, keepdims=True)
        sb = jnp.sum(nb * nb, axis=0, keepdims=True) + jnp.sum(rp2[half:], axis=0, keepdims=True)
        ra = lax.rsqrt(sa * inv_d + EPS)
        rb = lax.rsqrt(sb * inv_d + EPS)
        ra_rows = slice((2 * p) * LANES, (2 * p + 1) * LANES)
        rb_rows = slice((2 * p + 1) * LANES, (2 * p + 2) * LANES)
        qnt_ref[ra_rows, :] = (na * ra * gqn_ref[...]).astype(BF16)
        qnt_ref[rb_rows, :] = (nb * rb * gqn_ref[...]).astype(BF16)
        roped = rp * gqr_ref[...] * cos_t + rt * gqt_ref[...] * sin_t
        qrt_ref[p * LANES:(p + 1) * LANES, :] = jnp.concatenate(
            [roped[:half] * ra, roped[half:] * rb], axis=0).astype(BF16)

        knraw = jnp.dot(ckvn, wkn_ref[:, p * 2 * LANES:(p + 1) * 2 * LANES], preferred_element_type=F32)
        kna, knb = knraw[:, :LANES], knraw[:, LANES:]
        rka = lax.rsqrt((jnp.sum(kna * kna, axis=-1, keepdims=True) + k_rope_sq) * inv_d + EPS)
        rkb = lax.rsqrt((jnp.sum(knb * knb, axis=-1, keepdims=True) + k_rope_sq) * inv_d + EPS)
        kn_ref[:, ra_rows] = (kna * rka * gk_nope_ref[...]).astype(BF16)
        kn_ref[:, rb_rows] = (knb * rkb * gk_nope_ref[...]).astype(BF16)
        kr_ref[:, ra_rows] = jnp.where(first, k_roped * rka, 0.0).astype(BF16)
        kr_ref[:, rb_rows] = jnp.where(first, 0.0, k_roped * rkb).astype(BF16)

        vt_ref[p * 2 * LANES:(p + 1) * 2 * LANES, :] = _nt_dot(
            wv_ref[p * 2 * LANES:(p + 1) * 2 * LANES, :], ckvn).astype(BF16)


def _mla_proj(c, qa_g, kva_g, wq, wkn, wv, q_gains_t, k_gains, cos2, sin2, cos_t, sin_t, *, tm, seq):
    t = c.shape[0]
    n_pos = seq // tm
    const = lambda a: pl.BlockSpec(a.shape, lambda i: (0, 0))
    hw = MLA_HEADS * LANES
    tok = lambda w: pl.BlockSpec((tm, w), lambda i: (i, 0))
    tok_t = lambda w: pl.BlockSpec((w, tm), lambda i: (0, i))
    return pl.pallas_call(
        _mla_proj_body,
        grid=(t // tm,),
        in_specs=[tok(c.shape[1]), const(qa_g), const(kva_g), const(wq), const(wkn), const(wv)]
        + [const(a) for a in q_gains_t] + [const(a) for a in k_gains] + [
            pl.BlockSpec((tm, LANES), lambda i: (i % n_pos, 0)),
            pl.BlockSpec((tm, LANES), lambda i: (i % n_pos, 0)),
            pl.BlockSpec((LANES, tm), lambda i: (0, i % n_pos)),
            pl.BlockSpec((LANES, tm), lambda i: (0, i % n_pos)),
        ],
        out_specs=[tok_t(hw), tok_t(hw // 2), tok(hw), tok(hw), tok_t(hw)],
        out_shape=[
            jax.ShapeDtypeStruct((hw, t), BF16),
            jax.ShapeDtypeStruct((hw // 2, t), BF16),
            jax.ShapeDtypeStruct((t, hw), BF16),
            jax.ShapeDtypeStruct((t, hw), BF16),
            jax.ShapeDtypeStruct((hw, t), BF16),
        ],
        compiler_params=_cparams(("parallel",)),
        name="mla_proj",
    )(c, qa_g, kva_g, wq, wkn, wv, *q_gains_t, *k_gains, cos2, sin2, cos_t, sin_t)


def _own_chunk_mask(key, qry, dq):
    return ((key - dq) >> CHUNK_SHIFT) <= (qry >> CHUNK_SHIFT)


def _score_pass(k_tile, q_t, plan, s_ref, tq):
    m8 = None
    for n, desc in enumerate(plan):
        rows = slice(n * MXU_WIDTH, (n + 1) * MXU_WIDTH)
        if desc[0] == "wide":
            _, bias_lo, bias_hi, off_row = desc
            u = jnp.dot(k_tile(n), q_t, preferred_element_type=F32)
            if bias_lo is not None or bias_hi is not None:
                lo, hi = u[:, :tq], u[:, tq:]
                lo = lo if bias_lo is None else lo + bias_lo
                hi = hi if bias_hi is None else hi + bias_hi
                u = jnp.concatenate([lo, hi], axis=1)
            s_ref[rows, :] = u
            t = jnp.max(u.reshape(MXU_WIDTH // SUBLANES, SUBLANES, 2 * tq), axis=0)
            if off_row is not None:
                t = t + off_row
            m8 = t if m8 is None else jnp.maximum(m8, t)
        else:
            u = jnp.dot(k_tile(n), q_t[:, tq:], preferred_element_type=F32) + desc[1]
            s_ref[rows, tq:] = u
            t = jnp.max(u.reshape(MXU_WIDTH // SUBLANES, SUBLANES, tq), axis=0)
            m8 = jnp.concatenate([m8[:, :tq], jnp.maximum(m8[:, tq:], t)], axis=1)
    return jnp.max(m8, axis=0, keepdims=True)


def _prob_pass(s_ref, p_ref, m, l, m_blk, plan, tq):
    m_new = jnp.maximum(m, m_blk)
    alpha = jnp.exp2(m - m_new)
    ps8 = None
    for n, desc in enumerate(plan):
        rows = slice(n * MXU_WIDTH, (n + 1) * MXU_WIDTH)
        if desc[0] == "wide":
            off_row = desc[3]
            ref = m_new if off_row is None else m_new - off_row
            p = jnp.exp2(s_ref[rows, :] - ref)
            t = jnp.sum(p.reshape(MXU_WIDTH // SUBLANES, SUBLANES, 2 * tq), axis=0)
            ps8 = t if ps8 is None else ps8 + t
            p_ref[rows, :] = p.astype(BF16)
        else:
            p = jnp.exp2(s_ref[rows, tq:] - m_new[:, tq:])
            t = jnp.sum(p.reshape(MXU_WIDTH // SUBLANES, SUBLANES, tq), axis=0)
            ps8 = jnp.concatenate([ps8[:, :tq], ps8[:, tq:] + t], axis=1)
            p_ref[rows, tq:] = p.astype(BF16)
    l_new = alpha * l + jnp.sum(ps8, axis=0, keepdims=True)
    return m_new, l_new, alpha


def _prob_times_v(acc_ref, alpha, vt_ref, v_rows, k0, p_ref, plan, tq):
    n_wide = sum(1 for d in plan if d[0] == "wide")
    n_hi = len(plan) - n_wide
    kw = n_wide * MXU_WIDTH
    acc_ref[...] = alpha * acc_ref[...] + jnp.dot(
        vt_ref[v_rows, pl.ds(k0, kw)], p_ref[:kw, :], preferred_element_type=F32)
    if n_hi:
        kh = n_hi * MXU_WIDTH
        acc_ref[:, tq:] += jnp.dot(vt_ref[v_rows, pl.ds(pl.multiple_of(k0 + kw, MXU_WIDTH), kh)],
                                   p_ref[kw:kw + kh, tq:], preferred_element_type=F32)


def _key_block_pipeline(n_full, score, prob_pv, finalize, width):
    neg = jnp.full((1, width), MASK_VALUE, F32)
    zero = jnp.zeros((1, width), F32)
    init = (neg, zero, neg, zero)

    @pl.when(n_full == 0)
    def _():
        finalize(prob_pv(0, True, init, score(0, True)))

    @pl.when(n_full > 0)
    def _():
        def body(j, carry):
            state, mx = carry
            state = prob_pv(j, False, state, mx)
            return state, score(j + 1, False)

        state, mx = lax.fori_loop(0, n_full - 1, body, (init, score(0, False)))
        state = prob_pv(n_full - 1, False, state, mx)
        mx_last = score(n_full, True)
        finalize(prob_pv(n_full, True, state, mx_last))


def _da_attn_body(slopes_ref, lq1_ref, lk1_ref, lq2_ref, lk2_ref, sg_ref,
                  qt_ref, k_ref, vt_ref, o_ref, acc_ref, s_ref, p_ref, own_ref,
                  *, tq, tk, lambda_init):
    h = pl.program_id(1)
    n_full = pl.program_id(2)
    slope = slopes_ref[h]
    per_q = tq // MXU_WIDTH
    q0 = n_full * tk
    key = lax.broadcasted_iota(jnp.int32, (MXU_WIDTH, tq), 0)
    key_bias = slope * key.astype(F32)
    lane = lax.broadcasted_iota(jnp.int32, (1, tk), 1)
    acc_ref[...] = jnp.zeros_like(acc_ref)

    @pl.when(n_full == 0)
    def _():
        qry = lax.broadcasted_iota(jnp.int32, (MXU_WIDTH, tq), 1)
        mirror_bias = slope * (2 * qry - key).astype(F32)
        for m in range(per_q):
            dq = -m * MXU_WIDTH
            bias = jnp.minimum(key_bias - slope * dq, mirror_bias + slope * dq)
            own_ref[m] = jnp.where(_own_chunk_mask(key, qry, dq), bias, MASK_VALUE)

    def plan(j, last):
        tiles = []
        for n in range(tk // MXU_WIDTH):
            if not last:
                off = slope * (j * tk + n * MXU_WIDTH - q0).astype(F32)
                tiles.append(("wide", key_bias, key_bias,
                              off - jnp.where(lane >= tq, slope * tq, 0.0)))
            elif n < per_q:
                off_hi = slope * float(n * MXU_WIDTH - tq)
                tiles.append(("wide", own_ref[n], key_bias, jnp.where(lane >= tq, off_hi, 0.0)))
            else:
                tiles.append(("hi", own_ref[n - per_q]))
        return tiles

    def score(j, last):
        k0 = pl.multiple_of(j * tk, tk)
        tiles = plan(j, last)

        def k_tile(lo):
            return lambda n: k_ref[0, pl.ds(pl.multiple_of(k0 + n * MXU_WIDTH, MXU_WIDTH), MXU_WIDTH),
                                   lo:lo + DA_HEAD_DIM]

        return (_score_pass(k_tile(0), qt_ref[:DA_HEAD_DIM, :], tiles, s_ref.at[0], tq),
                _score_pass(k_tile(DA_HEAD_DIM), qt_ref[DA_HEAD_DIM:, :], tiles, s_ref.at[1], tq))

    def prob_pv(j, last, state, mx):
        m1, l1, m2, l2 = state
        tiles = plan(j, last)
        k0 = pl.multiple_of(j * tk, tk)
        m1, l1, a1 = _prob_pass(s_ref.at[0], p_ref.at[0], m1, l1, mx[0], tiles, tq)
        m2, l2, a2 = _prob_pass(s_ref.at[1], p_ref.at[1], m2, l2, mx[1], tiles, tq)
        every = slice(None)
        _prob_times_v(acc_ref.at[0], a1, vt_ref, every, k0, p_ref.at[0], tiles, tq)
        _prob_times_v(acc_ref.at[1], a2, vt_ref, every, k0, p_ref.at[1], tiles, tq)
        return m1, l1, m2, l2

    def finalize(state):
        _, l1, _, l2 = state
        lam = (jnp.exp(jnp.sum(lq1_ref[...] * lk1_ref[...], axis=-1, keepdims=True))
               - jnp.exp(jnp.sum(lq2_ref[...] * lk2_ref[...], axis=-1, keepdims=True))
               + lambda_init)
        o = acc_ref[0] * (1.0 / l1) - lam * (acc_ref[1] * (1.0 / l2))
        ms = jnp.mean(o * o, axis=0, keepdims=True)
        y = o * lax.rsqrt(ms + EPS) * sg_ref[...] * (1.0 - lambda_init)
        o_ref[0] = y.T.astype(BF16)

    _key_block_pipeline(n_full, score, prob_pv, finalize, tk)


def _da_attn(slopes, lams, sg_t, q_t, k, v_t, *, tq, tk, lambda_init):
    assert tk == 2 * tq
    b, s, _ = k.shape
    w = DA_V_DIM
    nq = s // tk
    vec = pl.BlockSpec((1, DA_HEAD_DIM), lambda bi, h, i: (0, 0))
    return pl.pallas_call(
        functools.partial(_da_attn_body, tq=tq, tk=tk, lambda_init=lambda_init),
        grid=(b, DA_HEADS, nq),
        in_specs=[
            pl.BlockSpec(memory_space=pltpu.SMEM),
            vec, vec, vec, vec,
            pl.BlockSpec((w, tk), lambda bi, h, i: (0, 0)),
            pl.BlockSpec((w, tk), lambda bi, h, i: (h, bi * nq + i)),
            pl.BlockSpec((1, s, w), lambda bi, h, i: (bi, 0, h)),
            pl.BlockSpec((w, s), lambda bi, h, i: (h, bi)),
        ],
        out_specs=pl.BlockSpec((1, tk, w), lambda bi, h, i: (bi, i, h)),
        out_shape=jax.ShapeDtypeStruct((b, s, DA_HEADS * w), BF16),
        scratch_shapes=[pltpu.VMEM((2, w, tk), F32),
                        pltpu.VMEM((2, tk, tk), F32), pltpu.VMEM((2, tk, tk), BF16),
                        pltpu.VMEM((tq // MXU_WIDTH, MXU_WIDTH, tq), F32)],
        compiler_params=_cparams(("parallel", "parallel", "arbitrary")),
        name="da_attn",
    )(slopes, *lams, sg_t, q_t, k, v_t)


def _mla_attn_body(qnt_ref, qrt_ref, kn_ref, kr_ref, vt_ref, o_ref, acc_ref, s_ref, p_ref, own_ref,
                   *, tq, tk):
    n_full = pl.program_id(2)
    per_q = tq // MXU_WIDTH
    acc_ref[...] = jnp.zeros_like(acc_ref)

    @pl.when(n_full == 0)
    def _():
        key = lax.broadcasted_iota(jnp.int32, (MXU_WIDTH, tq), 0)
        qry = lax.broadcasted_iota(jnp.int32, (MXU_WIDTH, tq), 1)
        for m in range(per_q):
            own_ref[m] = jnp.where(_own_chunk_mask(key, qry, -m * MXU_WIDTH), 0.0, MASK_VALUE)

    qr_t = qrt_ref[...]
    qa_t = jnp.concatenate([qnt_ref[:LANES, :], qr_t], axis=0)
    qb_t = jnp.concatenate([qnt_ref[LANES:, :], qr_t], axis=0)

    def plan(last):
        tiles = []
        for n in range(tk // MXU_WIDTH):
            if not last:
                tiles.append(("wide", None, None, None))
            elif n < per_q:
                tiles.append(("wide", own_ref[n], None, None))
            else:
                tiles.append(("hi", own_ref[n - per_q]))
        return tiles

    def score(j, last):
        k0 = pl.multiple_of(j * tk, tk)
        tiles = plan(last)

        def k_tile(lo):
            def tile(n):
                rows = pl.ds(pl.multiple_of(k0 + n * MXU_WIDTH, MXU_WIDTH), MXU_WIDTH)
                return jnp.concatenate([kn_ref[0, rows, lo:lo + LANES],
                                        kr_ref[0, rows, lo:lo + LANES]], axis=-1)
            return tile

        return (_score_pass(k_tile(0), qa_t, tiles, s_ref.at[0], tq),
                _score_pass(k_tile(LANES), qb_t, tiles, s_ref.at[1], tq))

    def prob_pv(j, last, state, mx):
        ma, la, mb, lb = state
        tiles = plan(last)
        k0 = pl.multiple_of(j * tk, tk)
        ma, la, aa = _prob_pass(s_ref.at[0], p_ref.at[0], ma, la, mx[0], tiles, tq)
        mb, lb, ab = _prob_pass(s_ref.at[1], p_ref.at[1], mb, lb, mx[1], tiles, tq)
        _prob_times_v(acc_ref.at[0], aa, vt_ref, slice(0, LANES), k0, p_ref.at[0], tiles, tq)
        _prob_times_v(acc_ref.at[1], ab, vt_ref, slice(LANES, 2 * LANES), k0, p_ref.at[1], tiles, tq)
        return ma, la, mb, lb

    def finalize(state):
        _, la, _, lb = state
        o_t = jnp.concatenate([acc_ref[0] * (1.0 / la), acc_ref[1] * (1.0 / lb)], axis=0)
        o_ref[0] = o_t.T.astype(BF16)

    _key_block_pipeline(n_full, score, prob_pv, finalize, tk)


def _mla_attn(qn_t, qr_t, kn, kr, v_t, *, tq, tk):
    assert tk == 2 * tq
    b, s, _ = kn.shape
    w = 2 * LANES
    nq = s // tk
    kv_spec = pl.BlockSpec((1, s, w), lambda bi, h, i: (bi, 0, h))
    return pl.pallas_call(
        functools.partial(_mla_attn_body, tq=tq, tk=tk),
        grid=(b, MLA_HEADS // 2, nq),
        in_specs=[
            pl.BlockSpec((w, tk), lambda bi, h, i: (h, bi * nq + i)),
            pl.BlockSpec((LANES, tk), lambda bi, h, i: (h, bi * nq + i)),
            kv_spec, kv_spec,
            pl.BlockSpec((w, s), lambda bi, h, i: (h, bi)),
        ],
        out_specs=pl.BlockSpec((1, tk, w), lambda bi, h, i: (bi, i, h)),
        out_shape=jax.ShapeDtypeStruct((b, s, MLA_HEADS * LANES), BF16),
        scratch_shapes=[pltpu.VMEM((2, LANES, tk), F32),
                        pltpu.VMEM((2, tk, tk), F32), pltpu.VMEM((2, tk, tk), BF16),
                        pltpu.VMEM((tq // MXU_WIDTH, MXU_WIDTH, tq), F32)],
        compiler_params=_cparams(("parallel", "parallel", "arbitrary")),
        name="mla_attn",
    )(qn_t, qr_t, kn, kr, v_t)


def _merge_body(x_ref, ya_ref, yb_ref, gate_a_ref, gate_b_ref, wa_ref, wb_ref, wo_ref, o_ref):
    @pl.when(pl.program_id(1) == 0)
    def _():
        o_ref[...] = x_ref[...]

    a = jnp.dot(ya_ref[...], wa_ref[...], preferred_element_type=F32)
    b = jnp.dot(yb_ref[...], wb_ref[...], preferred_element_type=F32)
    m = (gate_a_ref[...].astype(F32) * a + gate_b_ref[...].astype(F32) * b).astype(BF16)
    o_ref[...] += jnp.dot(m, wo_ref[...], preferred_element_type=F32)


def _merge(x, ya, yb, gates, wa, wb, wo, *, tm, tn):
    t, d = x.shape
    nj = d // tn
    tok = pl.BlockSpec((tm, d), lambda i, j: (i, 0))
    return pl.pallas_call(
        _merge_body,
        grid=(t // tm, nj),
        in_specs=[
            tok, tok, tok,
            pl.BlockSpec((tm, tn), lambda i, j: (i, j)),
            pl.BlockSpec((tm, tn), lambda i, j: (i, j + nj)),
            pl.BlockSpec((d, tn), lambda i, j: (0, j)),
            pl.BlockSpec((d, tn), lambda i, j: (0, j)),
            pl.BlockSpec((tn, d), lambda i, j: (j, 0)),
        ],
        out_specs=tok,
        out_shape=jax.ShapeDtypeStruct((t, d), F32),
        compiler_params=_cparams(("parallel", "arbitrary")),
        name="merge",
    )(x, ya, yb, gates, gates, wa, wb, wo)


def kernel(x, ffn1_norm_g, ffn1_w1, ffn1_w3, ffn1_w2, mix_norm_g, w_in, da_q_norm_g, da_k_norm_g, da_lambda_q1, da_lambda_k1, da_lambda_q2, da_lambda_k2, da_subln_g, mla_q_a_norm_g, mla_w_qb, mla_kv_a_norm_g, mla_w_kvb, mla_q_norm_g, mla_k_norm_g, w_branch_a, w_branch_b, w_out, ffn2_norm_g, ffn2_w1, ffn2_w3, ffn2_w2):
    bsz, seq, d = x.shape
    t = bsz * seq
    depth = ffn1_norm_g.shape[0]
    tm = _pick_tile(t, 512)
    tm_proj = _pick_tile(t, 1024)
    tm_mla = _pick_tile(seq, 512)
    tq = _pick_tile(seq, 512)
    tk = 2 * tq
    row = lambda v: v.reshape(1, -1).astype(F32)
    col_tile = lambda v, n: jnp.broadcast_to(v.astype(F32)[:, None], (v.shape[0], n))

    inv = ROPE_THETA ** (-jnp.arange(0, MLA_ROPE_DIM, 2, dtype=F32) / MLA_ROPE_DIM)
    ang = jnp.arange(seq, dtype=F32)[:, None] * inv[None, :]
    cos2 = jnp.tile(jnp.cos(ang), (1, 4))
    sin2 = jnp.tile(jnp.sin(ang), (1, 4))
    cos_t, sin_t = cos2.T, sin2.T
    slopes = (2.0 ** (-8.0 * jnp.arange(1, DA_HEADS + 1, dtype=F32) / DA_HEADS)) * LOG2E

    xt = x.reshape(t, d)
    for l in range(depth):
        lambda_init = 0.8 - 0.6 * math.exp(-0.3 * l)

        xt = _ffn(xt, row(ffn1_norm_g[l]), ffn1_w1[l].astype(BF16), ffn1_w3[l].astype(BF16),
                  (0.5 * ffn1_w2[l]).astype(BF16), tm=tm, tf=512)

        w_qkv, w_lat, w_gate = _split_w_in(w_in[l])
        q_scale = DA_HEAD_DIM ** -0.5 * LOG2E
        q_da_t, k_da, v_da_t, lat, gates = _in_proj(
            xt, row(mix_norm_g[l]), w_qkv, w_lat, w_gate,
            col_tile(da_q_norm_g[l] * q_scale, tm_proj), row(da_k_norm_g[l]), tm=tm_proj)

        wq = mla_w_qb[l].reshape(MLA_Q_RANK, MLA_HEADS, MLA_QK_DIM)
        wq_nope = wq[:, :, :MLA_NOPE_DIM].reshape(MLA_Q_RANK, MLA_HEADS // 2, 2 * LANES)
        wq_rope = wq[:, :, MLA_NOPE_DIM:]
        wq_rot = _rot_cols(wq_rope).reshape(MLA_Q_RANK, MLA_HEADS // 2, LANES)
        wq_rope = wq_rope.reshape(MLA_Q_RANK, MLA_HEADS // 2, LANES)
        wq_packed = jnp.concatenate([wq_nope, wq_rope, wq_rot], axis=-1).reshape(
            MLA_Q_RANK, MLA_HEADS * 2 * LANES).T.astype(BF16)
        wkv = mla_w_kvb[l].reshape(MLA_KV_RANK, MLA_HEADS, MLA_NOPE_DIM + MLA_V_DIM)
        wkn = wkv[:, :, :MLA_NOPE_DIM].reshape(MLA_KV_RANK, MLA_HEADS * MLA_NOPE_DIM).astype(BF16)
        wv = wkv[:, :, MLA_NOPE_DIM:].reshape(MLA_KV_RANK, MLA_HEADS * MLA_V_DIM).T.astype(BF16)
        mla_scale = MLA_QK_DIM ** -0.5 * LOG2E
        gq, gk = mla_q_norm_g[l], mla_k_norm_g[l]
        q_gains_t = (
            col_tile(gq[:MLA_NOPE_DIM] * mla_scale, tm_mla),
            col_tile(jnp.tile(gq[MLA_NOPE_DIM:], 2) * mla_scale, tm_mla),
            col_tile(jnp.tile(_swap_halves(gq[MLA_NOPE_DIM:]), 2) * mla_scale, tm_mla),
        )
        k_gains = (
            row(gk[:MLA_NOPE_DIM]),
            row(jnp.tile(gk[MLA_NOPE_DIM:], 2)),
            row(jnp.tile(_swap_halves(gk[MLA_NOPE_DIM:]), 2)),
        )
        qn_t, qr_t, kn, kr, v_mla_t = _mla_proj(
            lat, row(mla_q_a_norm_g[l]), row(mla_kv_a_norm_g[l]), wq_packed, wkn, wv,
            q_gains_t, k_gains, cos2, sin2, cos_t, sin_t, tm=tm_mla, seq=seq)

        b3 = lambda a: a.reshape(bsz, seq, a.shape[-1])
        lams = (row(da_lambda_q1[l]), row(da_lambda_k1[l]), row(da_lambda_q2[l]), row(da_lambda_k2[l]))
        y_a = _da_attn(slopes, lams, col_tile(da_subln_g[l], tk), q_da_t, b3(k_da), v_da_t,
                       tq=tq, tk=tk, lambda_init=lambda_init)
        y_b = _mla_attn(qn_t, qr_t, b3(kn), b3(kr), v_mla_t, tq=tq, tk=tk)

        xt = _merge(xt, y_a.reshape(t, -1), y_b.reshape(t, -1), gates,
                    w_branch_a[l].astype(BF16), w_branch_b[l].astype(BF16), w_out[l].astype(BF16),
                    tm=tm, tn=512)

        xt = _ffn(xt, row(ffn2_norm_g[l]), ffn2_w1[l].astype(BF16), ffn2_w3[l].astype(BF16),
                  (0.5 * ffn2_w2[l]).astype(BF16), tm=tm, tf=512)
    return xt.reshape(bsz, seq, d)
```

```python
import functools
import math

import jax
import jax.numpy as jnp
from jax import lax
from jax.experimental import pallas as pl
from jax.experimental.pallas import tpu as pltpu

F32 = jnp.float32
BF16 = jnp.bfloat16

EPS = 1e-6
CHUNK = 64
CHUNK_SHIFT = CHUNK.bit_length() - 1
DA_HEADS = 8
DA_HEAD_DIM = 128
DA_V_DIM = 2 * DA_HEAD_DIM
MLA_HEADS = 16
MLA_Q_RANK = 768
MLA_KV_RANK = 512
MLA_NOPE_DIM = 128
MLA_ROPE_DIM = 64
MLA_V_DIM = 128
MLA_QK_DIM = MLA_NOPE_DIM + MLA_ROPE_DIM
ROPE_THETA = 10000.0
LOG2E = math.log2(math.e)
MASK_VALUE = -1e30

LANES = 128
SUBLANES = 8
MXU_WIDTH = 256
V7X_VMEM_BYTES = 64 * 1024 * 1024
VMEM_LIMIT = V7X_VMEM_BYTES - 8 * 1024 * 1024


def _cparams(sem):
    return pltpu.CompilerParams(dimension_semantics=sem, vmem_limit_bytes=VMEM_LIMIT)


def _rms(x, g):
    ms = jnp.mean(x * x, axis=-1, keepdims=True)
    return x * lax.rsqrt(ms + EPS) * g


def _nt_dot(a, b):
    return lax.dot_general(a, b, (((1,), (1,)), ((), ())), preferred_element_type=F32)


def _rot_cols(w):
    half = w.shape[-1] // 2
    return jnp.concatenate([-w[..., half:], w[..., :half]], axis=-1)


def _swap_halves(g):
    half = g.shape[-1] // 2
    return jnp.concatenate([g[..., half:], g[..., :half]], axis=-1)


def _pick_tile(n, pref):
    return pref if n % pref == 0 else n


def _ffn_body(x_ref, g_ref, w1_ref, w3_ref, w2_ref, o_ref, h_ref):
    @pl.when(pl.program_id(1) == 0)
    def _():
        x = x_ref[...]
        h_ref[...] = _rms(x, g_ref[...]).astype(BF16)
        o_ref[...] = x

    h = h_ref[...]
    a = jnp.dot(h, w1_ref[...], preferred_element_type=F32)
    b = jnp.dot(h, w3_ref[...], preferred_element_type=F32)
    u = (a * jax.nn.sigmoid(a) * b).astype(BF16)
    o_ref[...] += jnp.dot(u, w2_ref[...], preferred_element_type=F32)


def _ffn(x, g, w1, w3, w2_half, *, tm, tf):
    t, d = x.shape
    f = w1.shape[1]
    return pl.pallas_call(
        _ffn_body,
        grid=(t // tm, f // tf),
        in_specs=[
            pl.BlockSpec((tm, d), lambda i, j: (i, 0)),
            pl.BlockSpec((1, d), lambda i, j: (0, 0)),
            pl.BlockSpec((d, tf), lambda i, j: (0, j)),
            pl.BlockSpec((d, tf), lambda i, j: (0, j)),
            pl.BlockSpec((tf, d), lambda i, j: (j, 0)),
        ],
        out_specs=pl.BlockSpec((tm, d), lambda i, j: (i, 0)),
        out_shape=jax.ShapeDtypeStruct((t, d), F32),
        scratch_shapes=[pltpu.VMEM((tm, d), BF16)],
        compiler_params=_cparams(("parallel", "arbitrary")),
        name="ffn",
    )(x, g, w1, w3, w2_half)


_W_IN_QKV = 2 * DA_HEADS * DA_HEAD_DIM * 2 + DA_HEADS * DA_V_DIM
_W_IN_KROPE = _W_IN_QKV + MLA_Q_RANK + MLA_KV_RANK
_W_IN_GATE = _W_IN_KROPE + MLA_ROPE_DIM


def _split_w_in(w):
    wt = w.T
    k_rope = wt[_W_IN_KROPE:_W_IN_GATE]
    half = MLA_ROPE_DIM // 2
    k_rot = jnp.concatenate([-k_rope[half:], k_rope[:half]], axis=0)
    w_lat = jnp.concatenate([wt[_W_IN_QKV:_W_IN_KROPE], k_rope, k_rope, k_rot, k_rot], axis=0)
    return wt[:_W_IN_QKV].astype(BF16), w_lat.astype(BF16), wt[_W_IN_GATE:].astype(BF16)


PROJ_TN = 512
_Q0, _K0, _V0, _C0, _G0, _NJ = 0, 4, 8, 12, 15, 23


def _head_norm(acc, g):
    parts = []
    for c in range(acc.shape[1] // LANES):
        blk = acc[:, c * LANES:(c + 1) * LANES]
        ms = jnp.mean(blk * blk, axis=-1, keepdims=True)
        parts.append(blk * lax.rsqrt(ms + EPS) * g)
    return jnp.concatenate(parts, axis=-1)


def _head_norm_t(acc_t, g_t):
    parts = []
    for c in range(acc_t.shape[0] // LANES):
        blk = acc_t[c * LANES:(c + 1) * LANES, :]
        ms = jnp.mean(blk * blk, axis=0, keepdims=True)
        parts.append(blk * lax.rsqrt(ms + EPS) * g_t)
    return jnp.concatenate(parts, axis=0)


def _proj_body(x_ref, g_ref, wqkv_ref, wlat_ref, wgate_ref, qg_ref, kg_ref,
               qt_ref, k_ref, vt_ref, c_ref, gate_ref, h_ref):
    j = pl.program_id(1)

    @pl.when(j == 0)
    def _():
        h_ref[...] = _rms(x_ref[...], g_ref[...]).astype(BF16)

    @pl.when(j < _K0)
    def _():
        qt_ref[...] = _head_norm_t(_nt_dot(wqkv_ref[...], h_ref[...]), qg_ref[...]).astype(BF16)

    @pl.when((j >= _K0) & (j < _V0))
    def _():
        k_ref[...] = _head_norm(_nt_dot(h_ref[...], wqkv_ref[...]), kg_ref[...]).astype(BF16)

    @pl.when((j >= _V0) & (j < _C0))
    def _():
        vt_ref[...] = _nt_dot(wqkv_ref[...], h_ref[...]).astype(BF16)

    @pl.when((j >= _C0) & (j < _G0))
    def _():
        c_ref[...] = _nt_dot(h_ref[...], wlat_ref[...])

    @pl.when(j >= _G0)
    def _():
        gate_ref[...] = jax.nn.sigmoid(_nt_dot(h_ref[...], wgate_ref[...])).astype(BF16)


def _in_proj(x, g, w_qkv, w_lat, w_gate, qg_t, kg, *, tm):
    t, d = x.shape
    tn = PROJ_TN
    n_q, n_k, n_v, n_c, n_g = _K0 - _Q0, _V0 - _K0, _C0 - _V0, _G0 - _C0, _NJ - _G0

    def w_cols(j0, n):
        return pl.BlockSpec((tn, d), lambda i, j: (jnp.clip(j - j0, 0, n - 1), 0))

    def cols(j0, n):
        return pl.BlockSpec((tm, tn), lambda i, j: (i, jnp.clip(j - j0, 0, n - 1)))

    def rows(j0, n):
        return pl.BlockSpec((tn, tm), lambda i, j: (jnp.clip(j - j0, 0, n - 1), i))

    return pl.pallas_call(
        _proj_body,
        grid=(t // tm, _NJ),
        in_specs=[
            pl.BlockSpec((tm, d), lambda i, j: (i, 0)),
            pl.BlockSpec((1, d), lambda i, j: (0, 0)),
            w_cols(_Q0, n_q + n_k + n_v), w_cols(_C0, n_c), w_cols(_G0, n_g),
            pl.BlockSpec((LANES, tm), lambda i, j: (0, 0)),
            pl.BlockSpec((1, LANES), lambda i, j: (0, 0)),
        ],
        out_specs=[rows(_Q0, n_q), cols(_K0, n_k), rows(_V0, n_v), cols(_C0, n_c), cols(_G0, n_g)],
        out_shape=[
            jax.ShapeDtypeStruct((n_q * tn, t), BF16),
            jax.ShapeDtypeStruct((t, n_k * tn), BF16),
            jax.ShapeDtypeStruct((n_v * tn, t), BF16),
            jax.ShapeDtypeStruct((t, n_c * tn), F32),
            jax.ShapeDtypeStruct((t, n_g * tn), BF16),
        ],
        scratch_shapes=[pltpu.VMEM((tm, d), BF16)],
        compiler_params=_cparams(("parallel", "arbitrary")),
        name="in_proj",
    )(x, g, w_qkv, w_lat, w_gate, qg_t, kg)


def _mla_proj_body(c_ref, qa_g_ref, kva_g_ref, wq_ref, wkn_ref, wv_ref,
                   gqn_ref, gqr_ref, gqt_ref, gk_nope_ref, gk_rope_ref, gk_rot_ref,
                   cos_ref, sin_ref, cost_ref, sint_ref,
                   qnt_ref, qrt_ref, kn_ref, kr_ref, vt_ref):
    c = c_ref[...]
    o_kv = MLA_Q_RANK
    o_kr = MLA_Q_RANK + MLA_KV_RANK
    cqn = _rms(c[:, :o_kv], qa_g_ref[...]).astype(BF16)
    ckvn = _rms(c[:, o_kv:o_kr], kva_g_ref[...]).astype(BF16)
    k_rope = c[:, o_kr:o_kr + LANES]
    k_rot = c[:, o_kr + LANES:o_kr + 2 * LANES]
    first = lax.broadcasted_iota(jnp.int32, (1, LANES), 1) < MLA_ROPE_DIM
    k_roped = k_rope * gk_rope_ref[...] * cos_ref[...] + k_rot * gk_rot_ref[...] * sin_ref[...]
    k_rope_sq = 0.5 * jnp.sum(k_rope * k_rope, axis=-1, keepdims=True)
    inv_d = 1.0 / MLA_QK_DIM
    cos_t = cost_ref[...]
    sin_t = sint_ref[...]
    half = MLA_ROPE_DIM

    for p in range(MLA_HEADS // 2):
        qraw = _nt_dot(wq_ref[p * 4 * LANES:(p + 1) * 4 * LANES, :], cqn)
        na, nb = qraw[:LANES], qraw[LANES:2 * LANES]
        rp, rt = qraw[2 * LANES:3 * LANES], qraw[3 * LANES:]
        rp2 = rp * rp
        sa = jnp.sum(na * na, axis=0, keepdims=True) + jnp.sum(rp2[:half], axis=0, keepdims=True)
        sb = jnp.sum(nb * nb, axis=0, keepdims=True) + jnp.sum(rp2[half:], axis=0, keepdims=True)
        ra = lax.rsqrt(sa * inv_d + EPS)
        rb = lax.rsqrt(sb * inv_d + EPS)
        ra_rows = slice((2 * p) * LANES, (2 * p + 1) * LANES)
        rb_rows = slice((2 * p + 1) * LANES, (2 * p + 2) * LANES)
        qnt_ref[ra_rows, :] = (na * ra * gqn_ref[...]).astype(BF16)
        qnt_ref[rb_rows, :] = (nb * rb * gqn_ref[...]).astype(BF16)
        roped = rp * gqr_ref[...] * cos_t + rt * gqt_ref[...] * sin_t
        qrt_ref[p * LANES:(p + 1) * LANES, :] = jnp.concatenate(
            [roped[:half] * ra, roped[half:] * rb], axis=0).astype(BF16)

        knraw = jnp.dot(ckvn, wkn_ref[:, p * 2 * LANES:(p + 1) * 2 * LANES], preferred_element_type=F32)
        kna, knb = knraw[:, :LANES], knraw[:, LANES:]
        rka = lax.rsqrt((jnp.sum(kna * kna, axis=-1, keepdims=True) + k_rope_sq) * inv_d + EPS)
        rkb = lax.rsqrt((jnp.sum(knb * knb, axis=-1, keepdims=True) + k_rope_sq) * inv_d + EPS)
        kn_ref[:, ra_rows] = (kna * rka * gk_nope_ref[...]).astype(BF16)
        kn_ref[:, rb_rows] = (knb * rkb * gk_nope_ref[...]).astype(BF16)
        kr_ref[:, ra_rows] = jnp.where(first, k_roped * rka, 0.0).astype(BF16)
        kr_ref[:, rb_rows] = jnp.where(first, 0.0, k_roped * rkb).astype(BF16)

        vt_ref[p * 2 * LANES:(p + 1) * 2 * LANES, :] = _nt_dot(
            wv_ref[p * 2 * LANES:(p + 1) * 2 * LANES, :], ckvn).astype(BF16)


def _mla_proj(c, qa_g, kva_g, wq, wkn, wv, q_gains_t, k_gains, cos2, sin2, cos_t, sin_t, *, tm, seq):
    t = c.shape[0]
    n_pos = seq // tm
    const = lambda a: pl.BlockSpec(a.shape, lambda i: (0, 0))
    hw = MLA_HEADS * LANES
    tok = lambda w: pl.BlockSpec((tm, w), lambda i: (i, 0))
    tok_t = lambda w: pl.BlockSpec((w, tm), lambda i: (0, i))
    return pl.pallas_call(
        _mla_proj_body,
        grid=(t // tm,),
        in_specs=[tok(c.shape[1]), const(qa_g), const(kva_g), const(wq), const(wkn), const(wv)]
        + [const(a) for a in q_gains_t] + [const(a) for a in k_gains] + [
            pl.BlockSpec((tm, LANES), lambda i: (i % n_pos, 0)),
            pl.BlockSpec((tm, LANES), lambda i: (i % n_pos, 0)),
            pl.BlockSpec((LANES, tm), lambda i: (0, i % n_pos)),
            pl.BlockSpec((LANES, tm), lambda i: (0, i % n_pos)),
        ],
        out_specs=[tok_t(hw), tok_t(hw // 2), tok(hw), tok(hw), tok_t(hw)],
        out_shape=[
            jax.ShapeDtypeStruct((hw, t), BF16),
            jax.ShapeDtypeStruct((hw // 2, t), BF16),
            jax.ShapeDtypeStruct((t, hw), BF16),
            jax.ShapeDtypeStruct((t, hw), BF16),
            jax.ShapeDtypeStruct((hw, t), BF16),
        ],
        compiler_params=_cparams(("parallel",)),
        name="mla_proj",
    )(c, qa_g, kva_g, wq, wkn, wv, *q_gains_t, *k_gains, cos2, sin2, cos_t, sin_t)


def _own_chunk_mask(key, qry, dq):
    return ((key - dq) >> CHUNK_SHIFT) <= (qry >> CHUNK_SHIFT)


def _score_pass(k_tile, q_t, plan, s_ref, tq):
    m8 = None
    for n, desc in enumerate(plan):
        rows = slice(n * MXU_WIDTH, (n + 1) * MXU_WIDTH)
        if desc[0] == "wide":
            _, bias_lo, bias_hi, off_row = desc
            u = jnp.dot(k_tile(n), q_t, preferred_element_type=F32)
            if bias_lo is not None or bias_hi is not None:
                lo, hi = u[:, :tq], u[:, tq:]
                lo = lo if bias_lo is None else lo + bias_lo
                hi = hi if bias_hi is None else hi + bias_hi
                u = jnp.concatenate([lo, hi], axis=1)
            s_ref[rows, :] = u
            t = jnp.max(u.reshape(MXU_WIDTH // SUBLANES, SUBLANES, 2 * tq), axis=0)
            if off_row is not None:
                t = t + off_row
            m8 = t if m8 is None else jnp.maximum(m8, t)
        else:
            u = jnp.dot(k_tile(n), q_t[:, tq:], preferred_element_type=F32) + desc[1]
            s_ref[rows, tq:] = u
            t = jnp.max(u.reshape(MXU_WIDTH // SUBLANES, SUBLANES, tq), axis=0)
            m8 = jnp.concatenate([m8[:, :tq], jnp.maximum(m8[:, tq:], t)], axis=1)
    return jnp.max(m8, axis=0, keepdims=True)


def _prob_pass(s_ref, p_ref, m, l, m_blk, plan, tq):
    m_new = jnp.maximum(m, m_blk)
    alpha = jnp.exp2(m - m_new)
    ps8 = None
    for n, desc in enumerate(plan):
        rows = slice(n * MXU_WIDTH, (n + 1) * MXU_WIDTH)
        if desc[0] == "wide":
            off_row = desc[3]
            ref = m_new if off_row is None else m_new - off_row
            p = jnp.exp2(s_ref[rows, :] - ref)
            t = jnp.sum(p.reshape(MXU_WIDTH // SUBLANES, SUBLANES, 2 * tq), axis=0)
            ps8 = t if ps8 is None else ps8 + t
            p_ref[rows, :] = p.astype(BF16)
        else:
            p = jnp.exp2(s_ref[rows, tq:] - m_new[:, tq:])
            t = jnp.sum(p.reshape(MXU_WIDTH // SUBLANES, SUBLANES, tq), axis=0)
            ps8 = jnp.concatenate([ps8[:, :tq], ps8[:, tq:] + t], axis=1)
            p_ref[rows, tq:] = p.astype(BF16)
    l_new = alpha * l + jnp.sum(ps8, axis=0, keepdims=True)
    return m_new, l_new, alpha


def _prob_times_v(acc_ref, alpha, vt_ref, v_rows, k0, p_ref, plan, tq):
    n_wide = sum(1 for d in plan if d[0] == "wide")
    n_hi = len(plan) - n_wide
    kw = n_wide * MXU_WIDTH
    acc_ref[...] = alpha * acc_ref[...] + jnp.dot(
        vt_ref[v_rows, pl.ds(k0, kw)], p_ref[:kw, :], preferred_element_type=F32)
    if n_hi:
        kh = n_hi * MXU_WIDTH
        acc_ref[:, tq:] += jnp.dot(vt_ref[v_rows, pl.ds(pl.multiple_of(k0 + kw, MXU_WIDTH), kh)],
                                   p_ref[kw:kw + kh, tq:], preferred_element_type=F32)


def _key_block_pipeline(n_full, score, prob_pv, finalize, width):
    neg = jnp.full((1, width), MASK_VALUE, F32)
    zero = jnp.zeros((1, width), F32)
    init = (neg, zero, neg, zero)

    @pl.when(n_full == 0)
    def _():
        finalize(prob_pv(0, True, init, score(0, True)))

    @pl.when(n_full > 0)
    def _():
        def body(j, carry):
            state, mx = carry
            state = prob_pv(j, False, state, mx)
            return state, score(j + 1, False)

        state, mx = lax.fori_loop(0, n_full - 1, body, (init, score(0, False)))
        state = prob_pv(n_full - 1, False, state, mx)
        mx_last = score(n_full, True)
        finalize(prob_pv(n_full, True, state, mx_last))


def _da_attn_body(slopes_ref, lq1_ref, lk1_ref, lq2_ref, lk2_ref, sg_ref,
                  qt_ref, k_ref, vt_ref, o_ref, acc_ref, s_ref, p_ref, own_ref,
                  *, tq, tk, lambda_init):
    h = pl.program_id(1)
    n_full = pl.program_id(2)
    slope = slopes_ref[h]
    per_q = tq // MXU_WIDTH
    q0 = n_full * tk
    key = lax.broadcasted_iota(jnp.int32, (MXU_WIDTH, tq), 0)
    key_bias = slope * key.astype(F32)
    lane = lax.broadcasted_iota(jnp.int32, (1, tk), 1)
    acc_ref[...] = jnp.zeros_like(acc_ref)

    @pl.when(n_full == 0)
    def _():
        qry = lax.broadcasted_iota(jnp.int32, (MXU_WIDTH, tq), 1)
        mirror_bias = slope * (2 * qry - key).astype(F32)
        for m in range(per_q):
            dq = -m * MXU_WIDTH
            bias = jnp.minimum(key_bias - slope * dq, mirror_bias + slope * dq)
            own_ref[m] = jnp.where(_own_chunk_mask(key, qry, dq), bias, MASK_VALUE)

    def plan(j, last):
        tiles = []
        for n in range(tk // MXU_WIDTH):
            if not last:
                off = slope * (j * tk + n * MXU_WIDTH - q0).astype(F32)
                tiles.append(("wide", key_bias, key_bias,
                              off - jnp.where(lane >= tq, slope * tq, 0.0)))
            elif n < per_q:
                off_hi = slope * float(n * MXU_WIDTH - tq)
                tiles.append(("wide", own_ref[n], key_bias, jnp.where(lane >= tq, off_hi, 0.0)))
            else:
                tiles.append(("hi", own_ref[n - per_q]))
        return tiles

    def score(j, last):
        k0 = pl.multiple_of(j * tk, tk)
        tiles = plan(j, last)

        def k_tile(lo):
            return lambda n: k_ref[0, pl.ds(pl.multiple_of(k0 + n * MXU_WIDTH, MXU_WIDTH), MXU_WIDTH),
                                   lo:lo + DA_HEAD_DIM]

        return (_score_pass(k_tile(0), qt_ref[:DA_HEAD_DIM, :], tiles, s_ref.at[0], tq),
                _score_pass(k_tile(DA_HEAD_DIM), qt_ref[DA_HEAD_DIM:, :], tiles, s_ref.at[1], tq))

    def prob_pv(j, last, state, mx):
        m1, l1, m2, l2 = state
        tiles = plan(j, last)
        k0 = pl.multiple_of(j * tk, tk)
        m1, l1, a1 = _prob_pass(s_ref.at[0], p_ref.at[0], m1, l1, mx[0], tiles, tq)
        m2, l2, a2 = _prob_pass(s_ref.at[1], p_ref.at[1], m2, l2, mx[1], tiles, tq)
        every = slice(None)
        _prob_times_v(acc_ref.at[0], a1, vt_ref, every, k0, p_ref.at[0], tiles, tq)
        _prob_times_v(acc_ref.at[1], a2, vt_ref, every, k0, p_ref.at[1], tiles, tq)
        return m1, l1, m2, l2

    def finalize(state):
        _, l1, _, l2 = state
        lam = (jnp.exp(jnp.sum(lq1_ref[...] * lk1_ref[...], axis=-1, keepdims=True))
               - jnp.exp(jnp.sum(lq2_ref[...] * lk2_ref[...], axis=-1, keepdims=True))
               + lambda_init)
        o = acc_ref[0] * (1.0 / l1) - lam * (acc_ref[1] * (1.0 / l2))
        ms = jnp.mean(o * o, axis=0, keepdims=True)
        y = o * lax.rsqrt(ms + EPS) * sg_ref[...] * (1.0 - lambda_init)
        o_ref[0] = y.T.astype(BF16)

    _key_block_pipeline(n_full, score, prob_pv, finalize, tk)


def _da_attn(slopes, lams, sg_t, q_t, k, v_t, *, tq, tk, lambda_init):
    assert tk == 2 * tq
    b, s, _ = k.shape
    w = DA_V_DIM
    nq = s // tk
    vec = pl.BlockSpec((1, DA_HEAD_DIM), lambda bi, h, i: (0, 0))
    return pl.pallas_call(
        functools.partial(_da_attn_body, tq=tq, tk=tk, lambda_init=lambda_init),
        grid=(b, DA_HEADS, nq),
        in_specs=[
            pl.BlockSpec(memory_space=pltpu.SMEM),
            vec, vec, vec, vec,
            pl.BlockSpec((w, tk), lambda bi, h, i: (0, 0)),
            pl.BlockSpec((w, tk), lambda bi, h, i: (h, bi * nq + i)),
            pl.BlockSpec((1, s, w), lambda bi, h, i: (bi, 0, h)),
            pl.BlockSpec((w, s), lambda bi, h, i: (h, bi)),
        ],
        out_specs=pl.BlockSpec((1, tk, w), lambda bi, h, i: (bi, i, h)),
        out_shape=jax.ShapeDtypeStruct((b, s, DA_HEADS * w), BF16),
        scratch_shapes=[pltpu.VMEM((2, w, tk), F32),
                        pltpu.VMEM((2, tk, tk), F32), pltpu.VMEM((2, tk, tk), BF16),
                        pltpu.VMEM((tq // MXU_WIDTH, MXU_WIDTH, tq), F32)],
        compiler_params=_cparams(("parallel", "parallel", "arbitrary")),
        name="da_attn",
    )(slopes, *lams, sg_t, q_t, k, v_t)


def _mla_attn_body(qnt_ref, qrt_ref, kn_ref, kr_ref, vt_ref, o_ref, acc_ref, s_ref, p_ref, own_ref,
                   *, tq, tk):
    n_full = pl.program_id(2)
    per_q = tq // MXU_WIDTH
    acc_ref[...] = jnp.zeros_like(acc_ref)

    @pl.when(n_full == 0)
    def _():
        key = lax.broadcasted_iota(jnp.int32, (MXU_WIDTH, tq), 0)
        qry = lax.broadcasted_iota(jnp.int32, (MXU_WIDTH, tq), 1)
        for m in range(per_q):
            own_ref[m] = jnp.where(_own_chunk_mask(key, qry, -m * MXU_WIDTH), 0.0, MASK_VALUE)

    qr_t = qrt_ref[...]
    qa_t = jnp.concatenate([qnt_ref[:LANES, :], qr_t], axis=0)
    qb_t = jnp.concatenate([qnt_ref[LANES:, :], qr_t], axis=0)

    def plan(last):
        tiles = []
        for n in range(tk // MXU_WIDTH):
            if not last:
                tiles.append(("wide", None, None, None))
            elif n < per_q:
                tiles.append(("wide", own_ref[n], None, None))
            else:
                tiles.append(("hi", own_ref[n - per_q]))
        return tiles

    def score(j, last):
        k0 = pl.multiple_of(j * tk, tk)
        tiles = plan(last)

        def k_tile(lo):
            def tile(n):
                rows = pl.ds(pl.multiple_of(k0 + n * MXU_WIDTH, MXU_WIDTH), MXU_WIDTH)
                return jnp.concatenate([kn_ref[0, rows, lo:lo + LANES],
                                        kr_ref[0, rows, lo:lo + LANES]], axis=-1)
            return tile

        return (_score_pass(k_tile(0), qa_t, tiles, s_ref.at[0], tq),
                _score_pass(k_tile(LANES), qb_t, tiles, s_ref.at[1], tq))

    def prob_pv(j, last, state, mx):
        ma, la, mb, lb = state
        tiles = plan(last)
        k0 = pl.multiple_of(j * tk, tk)
        ma, la, aa = _prob_pass(s_ref.at[0], p_ref.at[0], ma, la, mx[0], tiles, tq)
        mb, lb, ab = _prob_pass(s_ref.at[1], p_ref.at[1], mb, lb, mx[1], tiles, tq)
        _prob_times_v(acc_ref.at[0], aa, vt_ref, slice(0, LANES), k0, p_ref.at[0], tiles, tq)
        _prob_times_v(acc_ref.at[1], ab, vt_ref, slice(LANES, 2 * LANES), k0, p_ref.at[1], tiles, tq)
        return ma, la, mb, lb

    def finalize(state):
        _, la, _, lb = state
        o_t = jnp.concatenate([acc_ref[0] * (1.0 / la), acc_ref[1] * (1.0 / lb)], axis=0)
        o_ref[0] = o_t.T.astype(BF16)

    _key_block_pipeline(n_full, score, prob_pv, finalize, tk)


def _mla_attn(qn_t, qr_t, kn, kr, v_t, *, tq, tk):
    assert tk == 2 * tq
    b, s, _ = kn.shape
    w = 2 * LANES
    nq = s // tk
    kv_spec = pl.BlockSpec((1, s, w), lambda bi, h, i: (bi, 0, h))
    return pl.pallas_call(
        functools.partial(_mla_attn_body, tq=tq, tk=tk),
        grid=(b, MLA_HEADS // 2, nq),
        in_specs=[
            pl.BlockSpec((w, tk), lambda bi, h, i: (h, bi * nq + i)),
            pl.BlockSpec((LANES, tk), lambda bi, h, i: (h, bi * nq + i)),
            kv_spec, kv_spec,
            pl.BlockSpec((w, s), lambda bi, h, i: (h, bi)),
        ],
        out_specs=pl.BlockSpec((1, tk, w), lambda bi, h, i: (bi, i, h)),
        out_shape=jax.ShapeDtypeStruct((b, s, MLA_HEADS * LANES), BF16),
        scratch_shapes=[pltpu.VMEM((2, LANES, tk), F32),
                        pltpu.VMEM((2, tk, tk), F32), pltpu.VMEM((2, tk, tk), BF16),
                        pltpu.VMEM((tq // MXU_WIDTH, MXU_WIDTH, tq), F32)],
        compiler_params=_cparams(("parallel", "parallel", "arbitrary")),
        name="mla_attn",
    )(qn_t, qr_t, kn, kr, v_t)


def _merge_body(x_ref, ya_ref, yb_ref, gate_a_ref, gate_b_ref, wa_ref, wb_ref, wo_ref, o_ref):
    @pl.when(pl.program_id(1) == 0)
    def _():
        o_ref[...] = x_ref[...]

    a = jnp.dot(ya_ref[...], wa_ref[...], preferred_element_type=F32)
    b = jnp.dot(yb_ref[...], wb_ref[...], preferred_element_type=F32)
    m = (gate_a_ref[...].astype(F32) * a + gate_b_ref[...].astype(F32) * b).astype(BF16)
    o_ref[...] += jnp.dot(m, wo_ref[...], preferred_element_type=F32)


def _merge(x, ya, yb, gates, wa, wb, wo, *, tm, tn):
    t, d = x.shape
    nj = d // tn
    tok = pl.BlockSpec((tm, d), lambda i, j: (i, 0))
    once = dict(pipeline_mode=pl.Buffered(1)) if nj == 1 else {}
    return pl.pallas_call(
        _merge_body,
        grid=(t // tm, nj),
        in_specs=[
            tok, tok, tok,
            pl.BlockSpec((tm, tn), lambda i, j: (i, j)),
            pl.BlockSpec((tm, tn), lambda i, j: (i, j + nj)),
            pl.BlockSpec((d, tn), lambda i, j: (0, j), **once),
            pl.BlockSpec((d, tn), lambda i, j: (0, j), **once),
            pl.BlockSpec((tn, d), lambda i, j: (j, 0), **once),
        ],
        out_specs=tok,
        out_shape=jax.ShapeDtypeStruct((t, d), F32),
        compiler_params=_cparams(("parallel", "arbitrary")),
        name="merge",
    )(x, ya, yb, gates, gates, wa, wb, wo)


def kernel(x, ffn1_norm_g, ffn1_w1, ffn1_w3, ffn1_w2, mix_norm_g, w_in, da_q_norm_g, da_k_norm_g, da_lambda_q1, da_lambda_k1, da_lambda_q2, da_lambda_k2, da_subln_g, mla_q_a_norm_g, mla_w_qb, mla_kv_a_norm_g, mla_w_kvb, mla_q_norm_g, mla_k_norm_g, w_branch_a, w_branch_b, w_out, ffn2_norm_g, ffn2_w1, ffn2_w3, ffn2_w2):
    bsz, seq, d = x.shape
    t = bsz * seq
    depth = ffn1_norm_g.shape[0]
    tm = _pick_tile(t, 512)
    tm_proj = _pick_tile(t, 1024)
    tm_mla = _pick_tile(seq, 256)
    tq = _pick_tile(seq, 512)
    tk = 2 * tq
    row = lambda v: v.reshape(1, -1).astype(F32)
    col_tile = lambda v, n: jnp.broadcast_to(v.astype(F32)[:, None], (v.shape[0], n))

    inv = ROPE_THETA ** (-jnp.arange(0, MLA_ROPE_DIM, 2, dtype=F32) / MLA_ROPE_DIM)
    ang = jnp.arange(seq, dtype=F32)[:, None] * inv[None, :]
    cos2 = jnp.tile(jnp.cos(ang), (1, 4))
    sin2 = jnp.tile(jnp.sin(ang), (1, 4))
    cos_t, sin_t = cos2.T, sin2.T
    slopes = (2.0 ** (-8.0 * jnp.arange(1, DA_HEADS + 1, dtype=F32) / DA_HEADS)) * LOG2E

    xt = x.reshape(t, d)
    for l in range(depth):
        lambda_init = 0.8 - 0.6 * math.exp(-0.3 * l)

        xt = _ffn(xt, row(ffn1_norm_g[l]), ffn1_w1[l].astype(BF16), ffn1_w3[l].astype(BF16),
                  (0.5 * ffn1_w2[l]).astype(BF16), tm=tm, tf=512)

        w_qkv, w_lat, w_gate = _split_w_in(w_in[l])
        q_scale = DA_HEAD_DIM ** -0.5 * LOG2E
        q_da_t, k_da, v_da_t, lat, gates = _in_proj(
            xt, row(mix_norm_g[l]), w_qkv, w_lat, w_gate,
            col_tile(da_q_norm_g[l] * q_scale, tm_proj), row(da_k_norm_g[l]), tm=tm_proj)

        wq = mla_w_qb[l].reshape(MLA_Q_RANK, MLA_HEADS, MLA_QK_DIM)
        wq_nope = wq[:, :, :MLA_NOPE_DIM].reshape(MLA_Q_RANK, MLA_HEADS // 2, 2 * LANES)
        wq_rope = wq[:, :, MLA_NOPE_DIM:]
        wq_rot = _rot_cols(wq_rope).reshape(MLA_Q_RANK, MLA_HEADS // 2, LANES)
        wq_rope = wq_rope.reshape(MLA_Q_RANK, MLA_HEADS // 2, LANES)
        wq_packed = jnp.concatenate([wq_nope, wq_rope, wq_rot], axis=-1).reshape(
            MLA_Q_RANK, MLA_HEADS * 2 * LANES).T.astype(BF16)
        wkv = mla_w_kvb[l].reshape(MLA_KV_RANK, MLA_HEADS, MLA_NOPE_DIM + MLA_V_DIM)
        wkn = wkv[:, :, :MLA_NOPE_DIM].reshape(MLA_KV_RANK, MLA_HEADS * MLA_NOPE_DIM).astype(BF16)
        wv = wkv[:, :, MLA_NOPE_DIM:].reshape(MLA_KV_RANK, MLA_HEADS * MLA_V_DIM).T.astype(BF16)
        mla_scale = MLA_QK_DIM ** -0.5 * LOG2E
        gq, gk = mla_q_norm_g[l], mla_k_norm_g[l]
        q_gains_t = (
            col_tile(gq[:MLA_NOPE_DIM] * mla_scale, tm_mla),
            col_tile(jnp.tile(gq[MLA_NOPE_DIM:], 2) * mla_scale, tm_mla),
            col_tile(jnp.tile(_swap_halves(gq[MLA_NOPE_DIM:]), 2) * mla_scale, tm_mla),
        )
        k_gains = (
            row(gk[:MLA_NOPE_DIM]),
            row(jnp.tile(gk[MLA_NOPE_DIM:], 2)),
            row(jnp.tile(_swap_halves(gk[MLA_NOPE_DIM:]), 2)),
        )
        qn_t, qr_t, kn, kr, v_mla_t = _mla_proj(
            lat, row(mla_q_a_norm_g[l]), row(mla_kv_a_norm_g[l]), wq_packed, wkn, wv,
            q_gains_t, k_gains, cos2, sin2, cos_t, sin_t, tm=tm_mla, seq=seq)

        b3 = lambda a: a.reshape(bsz, seq, a.shape[-1])
        lams = (row(da_lambda_q1[l]), row(da_lambda_k1[l]), row(da_lambda_q2[l]), row(da_lambda_k2[l]))
        y_a = _da_attn(slopes, lams, col_tile(da_subln_g[l], tk), q_da_t, b3(k_da), v_da_t,
                       tq=tq, tk=tk, lambda_init=lambda_init)
        y_b = _mla_attn(qn_t, qr_t, b3(kn), b3(kr), v_mla_t, tq=tq, tk=tk)

        xt = _merge(xt, y_a.reshape(t, -1), y_b.reshape(t, -1), gates,
                    w_branch_a[l].astype(BF16), w_branch_b[l].astype(BF16), w_out[l].astype(BF16),
                    tm=_pick_tile(t, 256), tn=d)

        xt = _ffn(xt, row(ffn2_norm_g[l]), ffn2_w1[l].astype(BF16), ffn2_w3[l].astype(BF16),
                  (0.5 * ffn2_w2[l]).astype(BF16), tm=tm, tf=512)
    return xt.reshape(bsz, seq, d)
```
